```python
import math
import jax, jax.numpy as jnp
from jax import lax
import numpy as np

D_MODEL = 1024
BATCH = 32
SEQ = 256
DEPTH = 2
DEC_BATCH = 4
DEC_SEQ = 1024
PAST_LEN = 512

GRID_W = 64
HEAD_DIM = 64
A_HEADS = 4
A_KV = 2
WINDOW = 128
B_HEADS = 4
B_KV = 2
Q_BLOCK = 128
ROPE_THETA = 10000.0
ROPE_QUARTER = HEAD_DIM // 4
SSM_HEADS = 8
SSM_HEAD_DIM = 64
D_SSM = SSM_HEADS * SSM_HEAD_DIM
SSM_GROUPS = 2
SSM_STATE = 64
CONV_W = 3
CHUNK = 128
XBC_DIM = D_SSM + 2 * SSM_GROUPS * SSM_STATE
D_MIX = A_HEADS * HEAD_DIM + B_HEADS * HEAD_DIM + D_SSM
IN_SIZES = (A_HEADS * HEAD_DIM, A_KV * HEAD_DIM, A_KV * HEAD_DIM,
            B_HEADS * HEAD_DIM, B_KV * HEAD_DIM, B_KV * HEAD_DIM,
            D_SSM, XBC_DIM, 2 * SSM_HEADS)
D_IN = (A_HEADS + 2 * A_KV + B_HEADS + 2 * B_KV) * HEAD_DIM + D_SSM + XBC_DIM + 2 * SSM_HEADS
N_EGROUPS = 4
EXPERTS_PER_GROUP = 4
N_EXPERTS = N_EGROUPS * EXPERTS_PER_GROUP
TOP_K_IN_GROUP = 2
D_EXPERT = 256
EPS = 1e-6

kernel_name = 'hymba_prefix_diffusion_step'

f32 = jnp.float32


def rmsnorm(x, g):
    xf = x.astype(f32)
    y = xf * lax.rsqrt(jnp.mean(xf * xf, axis=-1, keepdims=True) + EPS)
    return y.astype(x.dtype) * g


def modulate(h, shift, scale):
    return h * (1 + scale) + shift


def axial_rope_tables(n_rows):
    rows = jnp.repeat(jnp.arange(n_rows), GRID_W).astype(f32)
    cols = jnp.tile(jnp.arange(GRID_W), n_rows).astype(f32)
    inv = ROPE_THETA ** (-jnp.arange(ROPE_QUARTER, dtype=f32) / ROPE_QUARTER)
    ang_r = rows[:, None] * inv
    ang_c = cols[:, None] * inv
    ang = jnp.concatenate([ang_r, ang_r, ang_c, ang_c], axis=-1)
    return jnp.cos(ang), jnp.sin(ang)


def apply_rope(x, cos, sin):
    x0, x1, x2, x3 = jnp.split(x, 4, axis=-1)
    rot = jnp.concatenate([-x1, x0, -x3, x2], axis=-1)
    c = cos.astype(x.dtype)[None, :, None, :]
    s = sin.astype(x.dtype)[None, :, None, :]
    return x * c + rot * s


def softmax_with_sink(s, sink):
    if sink is None:
        return jax.nn.softmax(s, axis=-1)
    kv, g = s.shape[-4], s.shape[-3]
    sk = jnp.broadcast_to(sink.astype(f32).reshape(kv, g, 1, 1), s.shape[:-1] + (1,))
    p = jax.nn.softmax(jnp.concatenate([s, sk], axis=-1), axis=-1)
    return p[..., :-1]


def attend_blocks(q, k, v, sink):
    b, n, h, d = q.shape
    kv = k.shape[2]
    g = h // kv
    nb = n // Q_BLOCK
    qb = q.reshape(b, nb, Q_BLOCK, kv, g, d).transpose(1, 0, 2, 3, 4, 5)

    def one_block(qblk):
        s = jnp.einsum('bqkgd,bskd->bkgqs', qblk, k).astype(f32) * (d ** -0.5)
        p = softmax_with_sink(s, sink)
        return jnp.einsum('bkgqs,bskd->bqkgd', p.astype(v.dtype), v)

    o = lax.map(one_block, qb)
    return o.transpose(1, 0, 2, 3, 4, 5).reshape(b, n, h, d)


def banded_attention(q, k, v, k_ctx, v_ctx, sink):
    b, n, h, d = q.shape
    kv = k.shape[2]
    g = h // kv
    W = WINDOW
    nb = n // W
    pad = ((0, 0), (W, W), (0, 0), (0, 0))
    kp = jnp.pad(k, pad).reshape(b, nb + 2, W, kv, d)
    vp = jnp.pad(v, pad).reshape(b, nb + 2, W, kv, d)
    kband = jnp.concatenate([kp[:, :-2], kp[:, 1:-1], kp[:, 2:]], axis=2)
    vband = jnp.concatenate([vp[:, :-2], vp[:, 1:-1], vp[:, 2:]], axis=2)
    qb = q.reshape(b, nb, W, kv, g, d)
    scale = d ** -0.5
    s_loc = jnp.einsum('bnqkgd,bnskd->bnkgqs', qb, kband).astype(f32) * scale
    s_ctx = jnp.einsum('bnqkgd,bskd->bnkgqs', qb, k_ctx).astype(f32) * scale
    blk = jnp.arange(nb)[:, None, None]
    qi = blk * W + jnp.arange(W)[None, :, None]
    ki = (blk - 1) * W + jnp.arange(3 * W)[None, None, :]
    valid = (jnp.abs(ki - qi) <= W) & (ki >= 0) & (ki < n)
    s_loc = jnp.where(valid[None, :, None, None, :, :], s_loc, -jnp.inf)
    p = softmax_with_sink(jnp.concatenate([s_loc, s_ctx], axis=-1), sink)
    p_loc = p[..., :3 * W].astype(v.dtype)
    p_ctx = p[..., 3 * W:].astype(v.dtype)
    o = (jnp.einsum('bnkgqs,bnskd->bnqkgd', p_loc, vband)
         + jnp.einsum('bnkgqs,bskd->bnqkgd', p_ctx, v_ctx))
    return o.reshape(b, n, h, d)


def conv_centred(x, w, bias):
    c = x.shape[-1]
    y = lax.conv_general_dilated(x, w[:, None, :].astype(x.dtype), window_strides=(1,),
                                 padding=[(CONV_W // 2, CONV_W // 2)],
                                 dimension_numbers=('NWC', 'WIO', 'NWC'), feature_group_count=c)
    return y + bias


def ssd_chunked(x, dt, a, bm, cm, h0):
    b, l, h, p = x.shape
    g = bm.shape[2]
    nc = l // CHUNK
    rep = h // g
    x = x.astype(f32)
    dt = dt.astype(f32)
    bm = jnp.repeat(bm.astype(f32), rep, axis=2).reshape(b, nc, CHUNK, h, -1)
    cm = jnp.repeat(cm.astype(f32), rep, axis=2).reshape(b, nc, CHUNK, h, -1)
    xd = (x * dt[..., None]).reshape(b, nc, CHUNK, h, p)
    da = (dt * a).reshape(b, nc, CHUNK, h).transpose(0, 3, 1, 2)
    cs = jnp.cumsum(da, axis=-1)
    seg = cs[..., :, None] - cs[..., None, :]
    lower = jnp.tril(jnp.ones((CHUNK, CHUNK), dtype=bool))
    lmat = jnp.exp(jnp.where(lower, seg, -jnp.inf))
    scores = jnp.einsum('bclhn,bcshn->bhcls', cm, bm) * lmat
    y_diag = jnp.einsum('bhcls,bcshp->bclhp', scores, xd)
    decay_to_end = jnp.exp(cs[..., -1:] - cs)
    chunk_states = jnp.einsum('bclhn,bhcl,bclhp->bchpn', bm, decay_to_end, xd)
    chunk_decay = jnp.exp(cs[..., -1])

    def step(hc, inp):
        st, dec = inp
        return hc * dec[..., None, None] + st, hc

    h_final, h_enter = lax.scan(step, h0.astype(f32),
                                (chunk_states.transpose(1, 0, 2, 3, 4), chunk_decay.transpose(2, 0, 1)))
    h_enter = h_enter.transpose(1, 0, 2, 3, 4)
    y_off = jnp.einsum('bclhn,bchpn,bhcl->bclhp', cm, h_enter, jnp.exp(cs))
    return (y_diag + y_off).reshape(b, l, h, p), h_final


def mamba_mix(z, xbc, dt_raw, conv_w, conv_b, dt_bias, a_log, d_skip, ssm_g, h0_f, h0_b):
    b, n, _ = xbc.shape
    xbc = jax.nn.silu(conv_centred(xbc, conv_w, conv_b))
    xs, bm, cm = jnp.split(xbc, [D_SSM, D_SSM + SSM_GROUPS * SSM_STATE], axis=-1)
    xs = xs.reshape(b, n, SSM_HEADS, SSM_HEAD_DIM)
    bm = bm.reshape(b, n, SSM_GROUPS, SSM_STATE)
    cm = cm.reshape(b, n, SSM_GROUPS, SSM_STATE)
    dt = jax.nn.softplus(dt_raw.astype(f32).reshape(b, n, 2, SSM_HEADS) + dt_bias.astype(f32))
    a = -jnp.exp(a_log.astype(f32))
    rev = lambda t: jnp.flip(t, axis=1)
    y_f, h_f = ssd_chunked(xs, dt[:, :, 0], a[0], bm, cm, h0_f)
    y_b, h_b = ssd_chunked(rev(xs), rev(dt[:, :, 1]), a[1], rev(bm), rev(cm), h0_b)
    y = y_f + rev(y_b) + xs.astype(f32) * d_skip.astype(f32)[:, None]
    y = y.astype(z.dtype).reshape(b, n, D_SSM)
    return rmsnorm(y * jax.nn.silu(z), ssm_g), h_f, h_b


def mix_layer(h, w_in, a_sink, q_norm_g, k_norm_g, conv_w, conv_b, dt_bias, a_log, d_skip, ssm_norm_g,
              w_out, cache):
    b, n, _ = h.shape
    points = np.cumsum(IN_SIZES)[:-1].tolist()
    a_q, a_k, a_v, b_q, b_k, b_v, z, xbc, dt_raw = jnp.split(h @ w_in, points, axis=-1)
    heads = lambda t, nh: t.reshape(b, n, nh, HEAD_DIM)
    a_q, a_k, a_v = heads(a_q, A_HEADS), heads(a_k, A_KV), heads(a_v, A_KV)
    b_q = rmsnorm(heads(b_q, B_HEADS), q_norm_g)
    b_k = rmsnorm(heads(b_k, B_KV), k_norm_g)
    b_v = heads(b_v, B_KV)
    if cache is None:
        o_a = attend_blocks(a_q, a_k, a_v, a_sink)
        o_b = attend_blocks(b_q, b_k, b_v, None)
        zeros = jnp.zeros((b, SSM_HEADS, SSM_HEAD_DIM, SSM_STATE), f32)
        o_c, h_f, h_b = mamba_mix(z, xbc, dt_raw, conv_w, conv_b, dt_bias, a_log, d_skip, ssm_norm_g,
                                  zeros, zeros)
        new = (a_k, a_v, b_k, b_v, h_f, h_b)
    else:
        ck_a, cv_a, ck_b, cv_b, h0_f, h0_b, cos, sin = cache
        a_q, a_k = apply_rope(a_q, cos, sin), apply_rope(a_k, cos, sin)
        b_q, b_k = apply_rope(b_q, cos, sin), apply_rope(b_k, cos, sin)
        o_a = banded_attention(a_q, a_k, a_v, ck_a.astype(a_k.dtype), cv_a.astype(a_v.dtype), a_sink)
        o_b = attend_blocks(b_q, jnp.concatenate([b_k, ck_b.astype(b_k.dtype)], axis=1),
                            jnp.concatenate([b_v, cv_b.astype(b_v.dtype)], axis=1), None)
        o_c, _, _ = mamba_mix(z, xbc, dt_raw, conv_w, conv_b, dt_bias, a_log, d_skip, ssm_norm_g,
                              h0_f, h0_b)
        new = None
    o = jnp.concatenate([o_a.reshape(b, n, -1), o_b.reshape(b, n, -1), o_c.astype(o_a.dtype)], axis=-1)
    return o @ w_out, new


def hier_moe(h, w_rg, b_rg, w_re, b_re, w_gate, w_up, w_down):
    shp = h.shape
    t = h.reshape(-1, shp[-1])
    n_tok = t.shape[0]
    tok = jnp.arange(n_tok)
    g_logit = (t @ w_rg + b_rg).astype(f32)
    g_prob = jax.nn.softmax(g_logit, axis=-1)
    g_sel = jnp.argmax(g_logit, axis=-1)
    e_logit = (t @ w_re + b_re).astype(f32).reshape(n_tok, N_EGROUPS, EXPERTS_PER_GROUP)
    e_in = e_logit[tok, g_sel]
    top_val, top_idx = lax.top_k(e_in, TOP_K_IN_GROUP)
    w_pair = jax.nn.softmax(top_val, axis=-1) * g_prob[tok, g_sel][:, None]
    expert = g_sel[:, None] * EXPERTS_PER_GROUP + top_idx
    combine = jnp.einsum('tk,tke->te', w_pair, jax.nn.one_hot(expert, N_EXPERTS, dtype=f32))
    hid = jax.nn.silu(jnp.einsum('td,edf->tef', t, w_gate)) * jnp.einsum('td,edf->tef', t, w_up)
    y = jnp.einsum('tef,efd->td', hid * combine[..., None].astype(hid.dtype), w_down)
    return y.reshape(shp).astype(h.dtype)


def layer_step(x, mod, n1, n2, w_in, a_sink, q_norm_g, k_norm_g, conv_w, conv_b, dt_bias, a_log, d_skip,
               ssm_norm_g, w_out, w_rg, b_rg, w_re, b_re, w_gate, w_up, w_down, cache):
    sh1, sc1, g1, sh2, sc2, g2 = jnp.split(mod, 6, axis=-1)
    o, new = mix_layer(modulate(rmsnorm(x, n1), sh1, sc1), w_in, a_sink, q_norm_g, k_norm_g, conv_w, conv_b,
                       dt_bias, a_log, d_skip, ssm_norm_g, w_out, cache)
    x = x + g1 * o
    x = x + g2 * hier_moe(modulate(rmsnorm(x, n2), sh2, sc2), w_rg, b_rg, w_re, b_re, w_gate, w_up, w_down)
    return x, new


def setup_inputs(seed: int = 0) -> dict:
    key = jax.random.key(seed)
    ks = jax.random.split(key, 40)
    nrm = lambda k, shape, s: jax.random.normal(k, shape, f32) * s
    dt0 = jnp.exp(jax.random.uniform(ks[20], (DEPTH, 2, SSM_HEADS), f32, math.log(1e-3), math.log(1e-1)))
    return {
        'x_prompt': nrm(ks[0], (BATCH, SEQ, D_MODEL), 1.0),
        'x_sample': nrm(ks[1], (DEC_BATCH, DEC_SEQ, D_MODEL), 1.0),
        'cache_a_k': nrm(ks[2], (DEC_BATCH, DEPTH, PAST_LEN, A_KV, HEAD_DIM), 1.0),
        'cache_a_v': nrm(ks[3], (DEC_BATCH, DEPTH, PAST_LEN, A_KV, HEAD_DIM), 1.0),
        'cache_b_k': nrm(ks[4], (DEC_BATCH, DEPTH, PAST_LEN, B_KV, HEAD_DIM), 1.0),
        'cache_b_v': nrm(ks[5], (DEC_BATCH, DEPTH, PAST_LEN, B_KV, HEAD_DIM), 1.0),
        'state_ssm_fwd': nrm(ks[6], (DEC_BATCH, DEPTH, SSM_HEADS, SSM_HEAD_DIM, SSM_STATE), 0.1),
        'state_ssm_bwd': nrm(ks[7], (DEC_BATCH, DEPTH, SSM_HEADS, SSM_HEAD_DIM, SSM_STATE), 0.1),
        'c': nrm(ks[8], (DEC_BATCH, D_MODEL), 1.0),
        'c_ctx': nrm(ks[9], (D_MODEL,), 1.0),
        'norm1_g': 1.0 + nrm(ks[10], (DEPTH, D_MODEL), 0.02),
        'norm2_g': 1.0 + nrm(ks[11], (DEPTH, D_MODEL), 0.02),
        'final_norm_g': 1.0 + nrm(ks[12], (D_MODEL,), 0.02),
        'w_ada': nrm(ks[13], (DEPTH, D_MODEL, 6 * D_MODEL), 0.5 * D_MODEL ** -0.5),
        'b_ada': nrm(ks[14], (DEPTH, 6 * D_MODEL), 0.02),
        'w_in': nrm(ks[15], (DEPTH, D_MODEL, D_IN), D_MODEL ** -0.5),
        'a_sink': nrm(ks[16], (DEPTH, A_HEADS), 0.5),
        'q_norm_g': 1.0 + nrm(ks[17], (DEPTH, HEAD_DIM), 0.02),
        'k_norm_g': 1.0 + nrm(ks[18], (DEPTH, HEAD_DIM), 0.02),
        'conv_w': nrm(ks[19], (DEPTH, CONV_W, XBC_DIM), CONV_W ** -0.5),
        'conv_b': nrm(ks[21], (DEPTH, XBC_DIM), 0.02),
        'dt_bias': dt0 + jnp.log(-jnp.expm1(-dt0)),
        'a_log': jnp.log(jax.random.uniform(ks[22], (DEPTH, 2, SSM_HEADS), f32, 1.0, 16.0)),
        'd_skip': 1.0 + nrm(ks[23], (DEPTH, SSM_HEADS), 0.1),
        'ssm_norm_g': 1.0 + nrm(ks[24], (DEPTH, D_SSM), 0.02),
        'w_out': nrm(ks[25], (DEPTH, D_MIX, D_MODEL), D_MIX ** -0.5),
        'w_router_group': nrm(ks[26], (DEPTH, D_MODEL, N_EGROUPS), D_MODEL ** -0.5),
        'b_router_group': nrm(ks[27], (DEPTH, N_EGROUPS), 0.01),
        'w_router_expert': nrm(ks[28], (DEPTH, D_MODEL, N_EXPERTS), D_MODEL ** -0.5),
        'b_router_expert': nrm(ks[29], (DEPTH, N_EXPERTS), 0.01),
        'w_gate': nrm(ks[30], (DEPTH, N_EXPERTS, D_MODEL, D_EXPERT), D_MODEL ** -0.5),
        'w_up': nrm(ks[31], (DEPTH, N_EXPERTS, D_MODEL, D_EXPERT), D_MODEL ** -0.5),
        'w_down': nrm(ks[32], (DEPTH, N_EXPERTS, D_EXPERT, D_MODEL), D_EXPERT ** -0.5),
    }


def reference(x_prompt, x_sample, cache_a_k, cache_a_v, cache_b_k, cache_b_v, state_ssm_fwd, state_ssm_bwd,
              c, c_ctx, norm1_g, norm2_g, final_norm_g, w_ada, b_ada, w_in, a_sink, q_norm_g, k_norm_g,
              conv_w, conv_b, dt_bias, a_log, d_skip, ssm_norm_g, w_out, w_router_group, b_router_group,
              w_router_expert, b_router_expert, w_gate, w_up, w_down):
    n_rows = x_sample.shape[1] // GRID_W
    cos, sin = axial_rope_tables(n_rows)
    xp, xs = x_prompt, x_sample
    a_ks, a_vs, b_ks, b_vs, h_fs, h_bs = [], [], [], [], [], []
    for l in range(DEPTH):
        lw = (norm1_g[l], norm2_g[l], w_in[l], a_sink[l], q_norm_g[l], k_norm_g[l], conv_w[l], conv_b[l],
              dt_bias[l], a_log[l], d_skip[l], ssm_norm_g[l], w_out[l], w_router_group[l], b_router_group[l],
              w_router_expert[l], b_router_expert[l], w_gate[l], w_up[l], w_down[l])
        mod_ctx = (jax.nn.silu(c_ctx) @ w_ada[l] + b_ada[l])[None, None, :]
        xp, new = layer_step(xp, mod_ctx, *lw, None)
        a_ks.append(new[0]); a_vs.append(new[1]); b_ks.append(new[2]); b_vs.append(new[3])
        h_fs.append(new[4]); h_bs.append(new[5])
        mod_lat = (jax.nn.silu(c) @ w_ada[l] + b_ada[l])[:, None, :]
        cache = (cache_a_k[:, l], cache_a_v[:, l], cache_b_k[:, l], cache_b_v[:, l],
                 state_ssm_fwd[:, l], state_ssm_bwd[:, l], cos, sin)
        xs, _ = layer_step(xs, mod_lat, *lw, cache)
    y_prompt = rmsnorm(xp, final_norm_g)
    y_sample = rmsnorm(xs, final_norm_g)
    new_a_k = jnp.stack(a_ks, axis=1)
    new_a_v = jnp.stack(a_vs, axis=1)
    new_b_k = jnp.stack(b_ks, axis=1)
    new_b_v = jnp.stack(b_vs, axis=1)
    new_ssm_fwd = jnp.stack(h_fs, axis=1)
    new_ssm_bwd = jnp.stack(h_bs, axis=1)
    return (y_prompt, y_sample, new_a_k, new_a_v, new_b_k, new_b_v, new_ssm_fwd, new_ssm_bwd)
```

```python
import functools

import jax
import jax.numpy as jnp
import numpy as np
from jax import lax
from jax.experimental import pallas as pl
from jax.experimental.pallas import tpu as pltpu

f32 = jnp.float32
bf16 = jnp.bfloat16
HIGHEST = lax.Precision.HIGHEST

D_MODEL = 1024
GRID_W = 64
HEAD_DIM = 64
A_HEADS = 4
A_KV = 2
WINDOW = 128
B_HEADS = 4
B_KV = 2
ROPE_THETA = 10000.0
ROPE_QUARTER = HEAD_DIM // 4
SSM_HEADS = 8
SSM_HEAD_DIM = 64
D_SSM = SSM_HEADS * SSM_HEAD_DIM
SSM_GROUPS = 2
SSM_STATE = 64
CHUNK = 128
XBC_DIM = D_SSM + 2 * SSM_GROUPS * SSM_STATE
D_AB = (A_HEADS + B_HEADS) * HEAD_DIM
N_EGROUPS = 4
EXPERTS_PER_GROUP = 4
N_EXPERTS = N_EGROUPS * EXPERTS_PER_GROUP
D_EXPERT = 256
EPS = 1e-6

LANES = 128
D_IN = 2320
D_IN_PAD = 2432
C_AQ, C_AK, C_AV, C_BQ, C_BK, C_BV, C_Z, C_XBC, C_DT, C_END = (
    0, 256, 384, 512, 768, 896, 1024, 1536, 2304, 2432)

VMEM_LIMIT = 56 * 1024 * 1024


def _params(sem, vmem=VMEM_LIMIT):
    return pltpu.CompilerParams(dimension_semantics=sem, vmem_limit_bytes=vmem)


def _dot(a, b, **kw):
    return jnp.dot(a, b, preferred_element_type=f32, **kw)


def _dot_nt(a, b):
    return lax.dot_general(a, b, (((1,), (1,)), ((), ())), preferred_element_type=f32)


def _dot_tn(a, b):
    return lax.dot_general(a, b, (((0,), (0,)), ((), ())), preferred_element_type=f32)


def _silu(x):
    return x / (1.0 + jnp.exp(-x))


def _softplus(x):
    return jnp.maximum(x, 0.0) + jnp.log1p(jnp.exp(-jnp.abs(x)))


def _mod_kernel(c_ref, w_ref, b_ref, o_ref):
    s = _silu(c_ref[...])
    o_ref[...] = _dot(s, w_ref[...], precision=HIGHEST) + b_ref[...]


def _modulation(cvec, w_ada, b_ada):
    depth = w_ada.shape[0]
    n = w_ada.shape[2]
    tn = 1536
    return pl.pallas_call(
        _mod_kernel,
        grid=(depth, n // tn),
        in_specs=[
            pl.BlockSpec((8, D_MODEL), lambda l, j: (0, 0)),
            pl.BlockSpec((None, D_MODEL, tn), lambda l, j: (l, 0, j)),
            pl.BlockSpec((None, 1, tn), lambda l, j: (l, 0, j)),
        ],
        out_specs=pl.BlockSpec((None, 8, tn), lambda l, j: (l, 0, j)),
        out_shape=jax.ShapeDtypeStruct((depth, 8, n), f32),
        compiler_params=_params(("arbitrary", "arbitrary")),
        name="adaln_mod",
    )(cvec, w_ada, b_ada.reshape(depth, 1, n))


def _rope(x, cos, sin_even, sin_odd):
    w = x.shape[-1]
    nxt = pltpu.roll(x, w - ROPE_QUARTER, 1)
    prv = pltpu.roll(x, ROPE_QUARTER, 1)
    return x * cos + nxt * sin_even + prv * sin_odd


def _inproj_kernel(x_ref, sh_ref, sc_ref, g_ref, w_ref, qg_ref, kg_ref, dtb_ref, hm_ref,
                   cos_ref, se_ref, so_ref,
                   qa_ref, ka_ref, va_ref, qb_ref, kb_ref, vb_ref, z_ref, xbc_ref, dt_ref,
                   *, n_p_tiles):
    i = pl.program_id(0)
    x = x_ref[...]
    ms = jnp.mean(x * x, axis=-1, keepdims=True)
    h = x * lax.rsqrt(ms + EPS) * g_ref[...]
    h = h * (1.0 + sc_ref[...]) + sh_ref[...]
    hb = h.astype(bf16)

    def proj(lo, hi):
        return _dot(hb, w_ref[:, lo:hi])

    def head_norm(t, gain):
        w = t.shape[-1]
        ms_h = _dot(t * t, hm_ref[0:w, 0:w], precision=HIGHEST)
        return t * lax.rsqrt(ms_h + EPS) * gain

    qa = proj(C_AQ, C_AK)
    ka = proj(C_AK, C_AV)
    qb = head_norm(proj(C_BQ, C_BK), qg_ref[...])
    kb = head_norm(proj(C_BK, C_BV), kg_ref[:, 0:2 * HEAD_DIM])
    va_ref[...] = proj(C_AV, C_BQ)
    vb_ref[...] = proj(C_BV, C_Z)
    z_ref[...] = proj(C_Z, C_XBC)
    xbc_ref[...] = proj(C_XBC, C_DT)
    dt_ref[...] = _softplus(proj(C_DT, C_END) + dtb_ref[...])

    @pl.when(i < n_p_tiles)
    def _():
        qa_ref[...] = qa
        ka_ref[...] = ka
        qb_ref[...] = qb
        kb_ref[...] = kb

    @pl.when(i >= n_p_tiles)
    def _():
        cos, se, so = cos_ref[...], se_ref[...], so_ref[...]
        kw = 2 * HEAD_DIM
        qa_ref[...] = _rope(qa, cos, se, so)
        qb_ref[...] = _rope(qb, cos, se, so)
        ka_ref[...] = _rope(ka, cos[:, :kw], se[:, :kw], so[:, :kw])
        kb_ref[...] = _rope(kb, cos[:, :kw], se[:, :kw], so[:, :kw])


def _tile_mod_row(i, n_p_tiles, tiles_per_seq):
    return jnp.where(i < n_p_tiles, 0, 1 + (i - n_p_tiles) // tiles_per_seq)


def _in_projection(x, mod4, norm_g, w_in, qg, kg, dtb, headmat, rope, n_p_tok, dec_seq, tm=512):
    t = x.shape[0]
    n_p_tiles = n_p_tok // tm
    tps = dec_seq // tm
    row = functools.partial(_tile_mod_row, n_p_tiles=n_p_tiles, tiles_per_seq=tps)
    cos, se, so = rope

    def rope_idx(i):
        return (jnp.where(i < n_p_tiles, 0, (i - n_p_tiles) % tps), 0)

    widths = (256, 128, 128, 256, 128, 128, 512, 768, 128)
    return pl.pallas_call(
        functools.partial(_inproj_kernel, n_p_tiles=n_p_tiles),
        grid=(t // tm,),
        in_specs=[
            pl.BlockSpec((tm, D_MODEL), lambda i: (i, 0)),
            pl.BlockSpec((None, None, 1, D_MODEL), lambda i: (row(i), 0, 0, 0)),
            pl.BlockSpec((None, None, 1, D_MODEL), lambda i: (row(i), 1, 0, 0)),
            pl.BlockSpec((1, D_MODEL), lambda i: (0, 0)),
            pl.BlockSpec((D_MODEL, D_IN_PAD), lambda i: (0, 0)),
            pl.BlockSpec((1, 256), lambda i: (0, 0)),
            pl.BlockSpec((1, 256), lambda i: (0, 0)),
            pl.BlockSpec((1, LANES), lambda i: (0, 0)),
            pl.BlockSpec((256, 256), lambda i: (0, 0)),
            pl.BlockSpec((tm, 256), rope_idx),
            pl.BlockSpec((tm, 256), rope_idx),
            pl.BlockSpec((tm, 256), rope_idx),
        ],
        out_specs=[pl.BlockSpec((tm, w), lambda i: (i, 0)) for w in widths],
        out_shape=[jax.ShapeDtypeStruct((t, w), f32) for w in widths],
        compiler_params=_params(("arbitrary",)),
        name="norm_mod_inproj",
    )(x, mod4, mod4, norm_g, w_in, qg, kg, dtb, headmat, cos, se, so)


def _softmax_pv(scores, values, sink):
    m = scores[0].max(axis=-1, keepdims=True)
    for s in scores[1:]:
        m = jnp.maximum(m, s.max(axis=-1, keepdims=True))
    if sink is not None:
        m = jnp.maximum(m, sink)
    den = None
    acc = None
    for s, v in zip(scores, values):
        p = jnp.exp(s - m)
        d = p.sum(axis=-1, keepdims=True)
        o = _dot(p.astype(bf16), v)
        den = d if den is None else den + d
        acc = o if acc is None else acc + o
    if sink is not None:
        den = den + jnp.exp(sink - m)
    return acc / den


def _attn_ctx_kernel(sink_ref, qa_ref, ka_ref, va_ref, qb_ref, kb_ref, vb_ref, o_ref):
    scale = HEAD_DIM ** -0.5
    for mixer, (q_ref, k_ref, v_ref) in enumerate(((qa_ref, ka_ref, va_ref), (qb_ref, kb_ref, vb_ref))):
        for kv in range(2):
            ks = slice(kv * HEAD_DIM, (kv + 1) * HEAD_DIM)
            k = k_ref[:, ks].astype(bf16)
            v = v_ref[:, ks].astype(bf16)
            for g in range(2):
                hd = kv * 2 + g
                hs = slice(hd * HEAD_DIM, (hd + 1) * HEAD_DIM)
                q = (q_ref[:, hs] * scale).astype(bf16)
                s = _dot_nt(q, k)
                sink = sink_ref[hd] if mixer == 0 else None
                o = _softmax_pv([s], [v], sink)
                os_ = slice(mixer * 256 + hd * HEAD_DIM, mixer * 256 + (hd + 1) * HEAD_DIM)
                o_ref[:, os_] = o


def _attention_ctx(sink, qa, ka, va, qb, kb, vb, n_batch, seq):
    qspec = pl.BlockSpec((seq, 256), lambda b: (b, 0))
    kspec = pl.BlockSpec((seq, 128), lambda b: (b, 0))
    return pl.pallas_call(
        _attn_ctx_kernel,
        grid=(n_batch,),
        in_specs=[pl.BlockSpec(memory_space=pltpu.SMEM), qspec, kspec, kspec, qspec, kspec, kspec],
        out_specs=pl.BlockSpec((seq, D_AB), lambda b: (b, 0)),
        out_shape=jax.ShapeDtypeStruct((n_batch * seq, D_AB), f32),
        compiler_params=_params(("arbitrary",)),
        name="attn_context",
    )(sink, qa, ka, va, qb, kb, vb)


def _attn_lat_kernel(sink_ref, qa_ref, ka_ref, va_ref, cka_ref, cva_ref,
                     qb_ref, kb_ref, vb_ref, ckb_ref, cvb_ref, o_ref, *, seq):
    j = pl.program_id(1)
    scale = HEAD_DIM ** -0.5
    w = WINDOW
    start = pl.multiple_of(jnp.clip((j - 1) * w, 0, seq - 3 * w), w)
    qi = j * w + lax.broadcasted_iota(jnp.int32, (w, 3 * w), 0)
    ki = start + lax.broadcasted_iota(jnp.int32, (w, 3 * w), 1)
    valid = jnp.abs(ki - qi) <= w
    for kv in range(2):
        ks = slice(kv * HEAD_DIM, (kv + 1) * HEAD_DIM)
        ka = ka_ref[pl.ds(start, 3 * w), ks].astype(bf16)
        va = va_ref[pl.ds(start, 3 * w), ks].astype(bf16)
        cka = cka_ref[:, ks].astype(bf16)
        cva = cva_ref[:, ks].astype(bf16)
        kb = kb_ref[:, ks].astype(bf16)
        vb = vb_ref[:, ks].astype(bf16)
        ckb = ckb_ref[:, ks].astype(bf16)
        cvb = cvb_ref[:, ks].astype(bf16)
        for g in range(2):
            hd = kv * 2 + g
            hs = slice(hd * HEAD_DIM, (hd + 1) * HEAD_DIM)
            q = (qa_ref[:, hs] * scale).astype(bf16)
            s_loc = jnp.where(valid, _dot_nt(q, ka), -jnp.inf)
            s_ctx = _dot_nt(q, cka)
            o_ref[:, hs] = _softmax_pv([s_loc, s_ctx], [va, cva], sink_ref[hd])
            q = (qb_ref[:, hs] * scale).astype(bf16)
            o = _softmax_pv([_dot_nt(q, kb), _dot_nt(q, ckb)], [vb, cvb], None)
            o_ref[:, slice(256 + hd * HEAD_DIM, 256 + (hd + 1) * HEAD_DIM)] = o


def _attention_lat(sink, qa, ka, va, cka, cva, qb, kb, vb, ckb, cvb, layer, n_p_tok, n_batch, seq):
    w = WINDOW
    nq = seq // w
    q0 = n_p_tok // w
    s0 = n_p_tok // seq
    past = cka.shape[2]
    qspec = pl.BlockSpec((w, 256), lambda b, j: (q0 + b * nq + j, 0))
    kspec = pl.BlockSpec((seq, 128), lambda b, j: (s0 + b, 0))
    cspec = pl.BlockSpec((None, None, past, 128), lambda b, j: (b, layer, 0, 0))
    return pl.pallas_call(
        functools.partial(_attn_lat_kernel, seq=seq),
        grid=(n_batch, nq),
        in_specs=[pl.BlockSpec(memory_space=pltpu.SMEM), qspec, kspec, kspec, cspec, cspec,
                  qspec, kspec, kspec, cspec, cspec],
        out_specs=pl.BlockSpec((w, D_AB), lambda b, j: (b * nq + j, 0)),
        out_shape=jax.ShapeDtypeStruct((n_batch * seq, D_AB), f32),
        compiler_params=_params(("arbitrary", "arbitrary")),
        name="attn_latent",
    )(sink, qa, ka, va, cka, cva, qb, kb, vb, ckb, cvb)


def _ssd_kernel(xbc_ref, z_ref, dt_ref, h0f_ref, h0b_ref, cw_ref, cb_ref, alog_ref, dskip_ref, g_ref,
                self_ref, selb_ref, selft_ref, selbt_ref,
                o_ref, hf_ref, hb_ref, xc_scr, y_scr):
    n = xbc_ref.shape[0]
    nc = n // CHUNK
    L = CHUNK
    ns = SSM_STATE
    gw = D_SSM // SSM_GROUPS
    hpg = SSM_HEADS // SSM_GROUPS

    x = xbc_ref[...]
    t_idx = lax.broadcasted_iota(jnp.int32, x.shape, 0)
    prv = jnp.where(t_idx == 0, 0.0, pltpu.roll(x, 1, 0))
    nxt = jnp.where(t_idx == n - 1, 0.0, pltpu.roll(x, n - 1, 0))
    y = prv * cw_ref[0:1, :] + x * cw_ref[1:2, :] + nxt * cw_ref[2:3, :] + cb_ref[...]
    xc_scr[...] = _silu(y)

    hf_ref[...] = h0f_ref[...]
    hb_ref[...] = h0b_ref[...]

    a_neg = -jnp.exp(alog_ref[...])
    r_i = lax.broadcasted_iota(jnp.int32, (L, L), 0)
    c_i = lax.broadcasted_iota(jnp.int32, (L, L), 1)
    lower = r_i >= c_i
    upper = r_i <= c_i
    tril = lower.astype(f32)
    triu = upper.astype(f32)

    def chunk_refs(c):
        r0 = pl.multiple_of(c * L, L)
        rows = pl.ds(r0, L)
        xs = xc_scr[rows, 0:D_SSM]
        bm = xc_scr[rows, D_SSM:D_SSM + SSM_GROUPS * ns]
        cm = xc_scr[rows, D_SSM + SSM_GROUPS * ns:XBC_DIM]
        dt = dt_ref[rows, :]
        return rows, xs, bm, cm, dt

    def state_pass(c, xs, bm, cm, dt, cum, cum_t, edge, sel_ref, selt_ref, h_ref, rows):
        e_in = _dot(jnp.exp(cum), sel_ref[...], precision=HIGHEST)
        hb16 = h_ref[...].astype(bf16)
        cb16 = cm.astype(bf16)
        for g in range(SSM_GROUPS):
            cols = slice(g * gw, (g + 1) * gw)
            yo = _dot_nt(cb16[:, g * ns:(g + 1) * ns], hb16[g * gw:(g + 1) * gw, :])
            y_scr[rows, cols] += yo * e_in[:, cols]
        wts = jnp.exp(cum[edge:edge + 1, :] - cum) * dt
        xw = (xs * _dot(wts, sel_ref[...], precision=HIGHEST)).astype(bf16)
        dec = jnp.broadcast_to(jnp.exp(cum_t[:, edge:edge + 1]), (LANES, ns))
        dmat = _dot(selt_ref[...], dec, precision=HIGHEST)
        bb16 = bm.astype(bf16)
        for g in range(SSM_GROUPS):
            hrows = slice(g * gw, (g + 1) * gw)
            st = _dot_tn(xw[:, hrows], bb16[:, g * ns:(g + 1) * ns])
            h_ref[hrows, :] = h_ref[hrows, :] * dmat[hrows, :] + st

    def fwd_body(c, carry):
        rows, xs, bm, cm, dt = chunk_refs(c)
        da = dt * a_neg
        cs = _dot(tril, da, precision=HIGHEST)
        suf = _dot(triu, da, precision=HIGHEST)
        cs_t, suf_t, dt_t = cs.T, suf.T, dt.T
        xb16 = xs.astype(bf16)
        cb16 = cm.astype(bf16)
        bb16 = bm.astype(bf16)
        for g in range(SSM_GROUPS):
            cb = _dot_nt(cb16[:, g * ns:(g + 1) * ns], bb16[:, g * ns:(g + 1) * ns])
            for hh in range(hpg):
                hd = g * hpg + hh
                hb_ = SSM_HEADS + hd
                lf = jnp.exp(jnp.where(lower, cs[:, hd:hd + 1] - cs_t[hd:hd + 1, :], -jnp.inf))
                lb = jnp.exp(jnp.where(upper, suf[:, hb_:hb_ + 1] - suf_t[hb_:hb_ + 1, :], -jnp.inf))
                m = cb * (lf * dt_t[hd:hd + 1, :] + lb * dt_t[hb_:hb_ + 1, :])
                cols = slice(hd * SSM_HEAD_DIM, (hd + 1) * SSM_HEAD_DIM)
                y_scr[rows, cols] = _dot(m.astype(bf16), xb16[:, cols])
        state_pass(c, xs, bm, cm, dt, cs, cs_t, L - 1, self_ref, selft_ref, hf_ref, rows)
        return carry

    def bwd_body(k, carry):
        c = nc - 1 - k
        rows, xs, bm, cm, dt = chunk_refs(c)
        da = dt * a_neg
        suf = _dot(triu, da, precision=HIGHEST)
        state_pass(c, xs, bm, cm, dt, suf, suf.T, 0, selb_ref, selbt_ref, hb_ref, rows)
        return carry

    lax.fori_loop(0, nc, fwd_body, 0)
    lax.fori_loop(0, nc, bwd_body, 0)

    yv = y_scr[...] + xc_scr[:, 0:D_SSM] * dskip_ref[...]
    yv = yv * _silu(z_ref[...])
    ms = jnp.mean(yv * yv, axis=-1, keepdims=True)
    o_ref[...] = yv * lax.rsqrt(ms + EPS) * g_ref[...]


def _ssd(xbc, z, dt, h0f, h0b, cw, cb, alog, dskip, g, sels, tok0, n_batch, seq):
    b0 = tok0 // seq

    def tok(width):
        return pl.BlockSpec((seq, width), lambda b: (b0 + b, 0))

    def const(shape):
        return pl.BlockSpec(shape, lambda b: tuple(0 for _ in shape))

    st = pl.BlockSpec((None, D_SSM, SSM_STATE), lambda b: (b, 0, 0))
    self_, selb, selft, selbt = sels
    return pl.pallas_call(
        _ssd_kernel,
        grid=(n_batch,),
        in_specs=[tok(XBC_DIM), tok(D_SSM), tok(LANES), st, st,
                  const((3, XBC_DIM)), const((1, XBC_DIM)), const((1, LANES)), const((1, D_SSM)),
                  const((1, D_SSM)),
                  const((LANES, D_SSM)), const((LANES, D_SSM)), const((D_SSM, LANES)), const((D_SSM, LANES))],
        out_specs=[pl.BlockSpec((seq, D_SSM), lambda b: (b, 0)), st, st],
        out_shape=[jax.ShapeDtypeStruct((n_batch * seq, D_SSM), f32),
                   jax.ShapeDtypeStruct((n_batch, D_SSM, SSM_STATE), f32),
                   jax.ShapeDtypeStruct((n_batch, D_SSM, SSM_STATE), f32)],
        scratch_shapes=[pltpu.VMEM((seq, XBC_DIM), f32), pltpu.VMEM((seq, D_SSM), f32)],
        compiler_params=_params(("arbitrary",)),
        name="ssd_bidir",
    )(xbc, z, dt, h0f, h0b, cw, cb, alog, dskip, g, self_, selb, selft, selbt)


def _outproj_kernel(oab_ref, oc_ref, x_ref, g1_ref, sh_ref, sc_ref, n2_ref, w_ref, wr_ref, br_ref,
                    x1_ref, h2_ref, comb_ref):
    o = _dot(oab_ref[...].astype(bf16), w_ref[0:D_AB, :]) + _dot(oc_ref[...].astype(bf16), w_ref[D_AB:, :])
    x1 = x_ref[...] + g1_ref[...] * o
    x1_ref[...] = x1
    ms = jnp.mean(x1 * x1, axis=-1, keepdims=True)
    h2 = x1 * lax.rsqrt(ms + EPS) * n2_ref[...]
    h2 = h2 * (1.0 + sc_ref[...]) + sh_ref[...]
    h2_ref[...] = h2.astype(bf16)

    logits = _dot(h2, wr_ref[...], precision=HIGHEST) + br_ref[...]
    lane = lax.broadcasted_iota(jnp.int32, logits.shape, 1).astype(f32)
    big = float(LANES)
    neg = -jnp.inf
    gmask = (lane >= N_EXPERTS) & (lane < N_EXPERTS + N_EGROUPS)
    gl = jnp.where(gmask, logits, neg)
    gmax = gl.max(axis=-1, keepdims=True)
    gsel = jnp.where(gl == gmax, lane, big).min(axis=-1, keepdims=True) - N_EXPERTS
    gprob = 1.0 / jnp.where(gmask, jnp.exp(logits - gmax), 0.0).sum(axis=-1, keepdims=True)
    emask = (lane >= gsel * EXPERTS_PER_GROUP) & (lane < (gsel + 1) * EXPERTS_PER_GROUP)
    el = jnp.where(emask, logits, neg)
    v1 = el.max(axis=-1, keepdims=True)
    i1 = jnp.where(el == v1, lane, big).min(axis=-1, keepdims=True)
    el2 = jnp.where(lane == i1, neg, el)
    v2 = el2.max(axis=-1, keepdims=True)
    i2 = jnp.where(el2 == v2, lane, big).min(axis=-1, keepdims=True)
    e2 = jnp.exp(v2 - v1)
    den = 1.0 + e2
    comb_ref[...] = (jnp.where(lane == i1, gprob / den, 0.0)
                     + jnp.where(lane == i2, gprob * e2 / den, 0.0))


def _out_projection(oab, oc, x, mod4, norm_g, w_out, wr, br, n_p_tok, dec_seq, tm=512):
    t = x.shape[0]
    n_p_tiles = n_p_tok // tm
    tps = dec_seq // tm
    row = functools.partial(_tile_mod_row, n_p_tiles=n_p_tiles, tiles_per_seq=tps)

    def modspec(k):
        return pl.BlockSpec((None, None, 1, D_MODEL), lambda i: (row(i), k, 0, 0))

    return pl.pallas_call(
        _outproj_kernel,
        grid=(t // tm,),
        in_specs=[
            pl.BlockSpec((tm, D_AB), lambda i: (i, 0)),
            pl.BlockSpec((tm, D_SSM), lambda i: (i, 0)),
            pl.BlockSpec((tm, D_MODEL), lambda i: (i, 0)),
            modspec(2), modspec(3), modspec(4),
            pl.BlockSpec((1, D_MODEL), lambda i: (0, 0)),
            pl.BlockSpec((D_AB + D_SSM, D_MODEL), lambda i: (0, 0)),
            pl.BlockSpec((D_MODEL, LANES), lambda i: (0, 0)),
            pl.BlockSpec((1, LANES), lambda i: (0, 0)),
        ],
        out_specs=[pl.BlockSpec((tm, D_MODEL), lambda i: (i, 0)),
                   pl.BlockSpec((tm, D_MODEL), lambda i: (i, 0)),
                   pl.BlockSpec((tm, LANES), lambda i: (i, 0))],
        out_shape=[jax.ShapeDtypeStruct((t, D_MODEL), f32),
                   jax.ShapeDtypeStruct((t, D_MODEL), bf16),
                   jax.ShapeDtypeStruct((t, LANES), f32)],
        compiler_params=_params(("arbitrary",)),
        name="outproj_norm_router",
    )(oab, oc, x, mod4, mod4, mod4, norm_g, w_out, wr, br)


def _moe_kernel(h2_ref, comb_ref, wg_ref, wu_ref, wd_ref, x1_ref, g2_ref, fg_ref, o_ref, acc_ref,
                *, final_norm):
    e = pl.program_id(1)
    h = h2_ref[...]
    a = _dot(h, wg_ref[...])
    u = _dot(h, wu_ref[...])
    comb = comb_ref[...]
    lane = lax.broadcasted_iota(jnp.int32, comb.shape, 1)
    ce = jnp.where(lane == e, comb, 0.0).sum(axis=-1, keepdims=True)
    hid = _silu(a) * u * ce
    y = _dot(hid.astype(bf16), wd_ref[...])

    @pl.when(e == 0)
    def _():
        acc_ref[...] = y

    @pl.when(e > 0)
    def _():
        acc_ref[...] += y

    @pl.when(e == N_EXPERTS - 1)
    def _():
        x2 = x1_ref[...] + g2_ref[...] * acc_ref[...]
        if final_norm:
            ms = jnp.mean(x2 * x2, axis=-1, keepdims=True)
            x2 = x2 * lax.rsqrt(ms + EPS) * fg_ref[...]
        o_ref[...] = x2


def _moe(h2, comb, wg, wu, wd, x1, mod4, final_g, n_p_tok, dec_seq, final_norm, tm=1024):
    t = h2.shape[0]
    n_p_tiles = n_p_tok // tm
    tps = dec_seq // tm
    row = functools.partial(_tile_mod_row, n_p_tiles=n_p_tiles, tiles_per_seq=tps)
    return pl.pallas_call(
        functools.partial(_moe_kernel, final_norm=final_norm),
        grid=(t // tm, N_EXPERTS),
        in_specs=[
            pl.BlockSpec((tm, D_MODEL), lambda i, e: (i, 0)),
            pl.BlockSpec((tm, LANES), lambda i, e: (i, 0)),
            pl.BlockSpec((None, D_MODEL, D_EXPERT), lambda i, e: (e, 0, 0)),
            pl.BlockSpec((None, D_MODEL, D_EXPERT), lambda i, e: (e, 0, 0)),
            pl.BlockSpec((None, D_EXPERT, D_MODEL), lambda i, e: (e, 0, 0)),
            pl.BlockSpec((tm, D_MODEL), lambda i, e: (i, 0)),
            pl.BlockSpec((None, None, 1, D_MODEL), lambda i, e: (row(i), 5, 0, 0)),
            pl.BlockSpec((1, D_MODEL), lambda i, e: (0, 0)),
        ],
        out_specs=pl.BlockSpec((tm, D_MODEL), lambda i, e: (i, 0)),
        out_shape=jax.ShapeDtypeStruct((t, D_MODEL), f32),
        scratch_shapes=[pltpu.VMEM((tm, D_MODEL), f32)],
        compiler_params=_params(("arbitrary", "arbitrary")),
        name="moe_ffn",
    )(h2, comb, wg, wu, wd, x1, mod4, final_g)


def _rope_tables(n_rows):
    rows = jnp.repeat(jnp.arange(n_rows), GRID_W).astype(f32)
    cols = jnp.tile(jnp.arange(GRID_W), n_rows).astype(f32)
    inv = ROPE_THETA ** (-jnp.arange(ROPE_QUARTER, dtype=f32) / ROPE_QUARTER)
    ang_r = rows[:, None] * inv
    ang_c = cols[:, None] * inv
    ang = jnp.concatenate([ang_r, ang_r, ang_c, ang_c], axis=-1)
    cos, sin = jnp.cos(ang), jnp.sin(ang)
    even = (np.arange(HEAD_DIM) // ROPE_QUARTER) % 2 == 0
    sin_even = jnp.where(even, -sin, 0.0)
    sin_odd = jnp.where(even, 0.0, sin)
    rep = 256 // HEAD_DIM
    return tuple(jnp.tile(t, (1, rep)) for t in (cos, sin_even, sin_odd))


def _head_select():
    sel = np.zeros((2, LANES, D_SSM), np.float32)
    for d in range(2):
        for h in range(SSM_HEADS):
            sel[d, d * SSM_HEADS + h, h * SSM_HEAD_DIM:(h + 1) * SSM_HEAD_DIM] = 1.0
    return (jnp.asarray(sel[0]), jnp.asarray(sel[1]),
            jnp.asarray(sel[0].T.copy()), jnp.asarray(sel[1].T.copy()))


def _head_mean_matrix():
    m = np.zeros((256, 256), np.float32)
    for h in range(256 // HEAD_DIM):
        m[h * HEAD_DIM:(h + 1) * HEAD_DIM, h * HEAD_DIM:(h + 1) * HEAD_DIM] = 1.0 / HEAD_DIM
    return jnp.asarray(m)


def _pad_lanes(v, width=LANES):
    v = v.reshape(1, -1)
    return jnp.pad(v, ((0, 0), (0, width - v.shape[1])))


@jax.jit
def kernel(x_prompt, x_sample, cache_a_k, cache_a_v, cache_b_k, cache_b_v, state_ssm_fwd, state_ssm_bwd, c, c_ctx, norm1_g, norm2_g, final_norm_g, w_ada, b_ada, w_in, a_sink, q_norm_g, k_norm_g, conv_w, conv_b, dt_bias, a_log, d_skip, ssm_norm_g, w_out, w_router_group, b_router_group, w_router_expert, b_router_expert, w_gate, w_up, w_down):
    batch, seq, _ = x_prompt.shape
    dec_batch, dec_seq, _ = x_sample.shape
    depth = w_in.shape[0]
    past = cache_a_k.shape[2]
    n_p_tok = batch * seq
    n_s_tok = dec_batch * dec_seq

    x = jnp.concatenate([x_prompt.reshape(n_p_tok, D_MODEL), x_sample.reshape(n_s_tok, D_MODEL)], axis=0)

    cvec = jnp.concatenate([c_ctx[None, :], c, jnp.zeros((8 - 1 - dec_batch, D_MODEL), f32)], axis=0)
    mod = _modulation(cvec, w_ada, b_ada).reshape(depth, 8, 6, 1, D_MODEL)

    rope = _rope_tables(dec_seq // GRID_W)
    sels = _head_select()
    headmat = _head_mean_matrix()
    zeros_state = jnp.zeros((batch, D_SSM, SSM_STATE), f32)

    cache_a_k = cache_a_k.reshape(dec_batch, depth, past, A_KV * HEAD_DIM)
    cache_a_v = cache_a_v.reshape(dec_batch, depth, past, A_KV * HEAD_DIM)
    cache_b_k = cache_b_k.reshape(dec_batch, depth, past, B_KV * HEAD_DIM)
    cache_b_v = cache_b_v.reshape(dec_batch, depth, past, B_KV * HEAD_DIM)

    new = [[] for _ in range(6)]
    for l in range(depth):
        w_in_l = jnp.pad(w_in[l], ((0, 0), (0, D_IN_PAD - D_IN))).astype(bf16)
        qg = jnp.tile(q_norm_g[l], 256 // HEAD_DIM).reshape(1, 256)
        kg = jnp.tile(k_norm_g[l], 256 // HEAD_DIM).reshape(1, 256)
        dtb = _pad_lanes(dt_bias[l])
        qa, ka, va, qb, kb, vb, z, xbc, dt = _in_projection(
            x, mod[l], norm1_g[l].reshape(1, D_MODEL), w_in_l, qg, kg, dtb, headmat, rope, n_p_tok, dec_seq)

        sink = a_sink[l]
        o_p = _attention_ctx(sink, qa, ka, va, qb, kb, vb, batch, seq)
        o_s = _attention_lat(sink, qa, ka, va, cache_a_k, cache_a_v, qb, kb, vb, cache_b_k, cache_b_v,
                             l, n_p_tok, dec_batch, dec_seq)
        oab = jnp.concatenate([o_p, o_s], axis=0)

        alog = _pad_lanes(a_log[l])
        dskip = jnp.repeat(d_skip[l], SSM_HEAD_DIM).reshape(1, D_SSM)
        sg = ssm_norm_g[l].reshape(1, D_SSM)
        cw, cb = conv_w[l], conv_b[l].reshape(1, XBC_DIM)
        oc_p, hf, hb = _ssd(xbc, z, dt, zeros_state, zeros_state, cw, cb, alog, dskip, sg, sels,
                            0, batch, seq)
        oc_s, _, _ = _ssd(xbc, z, dt,
                          state_ssm_fwd[:, l].reshape(dec_batch, D_SSM, SSM_STATE),
                          state_ssm_bwd[:, l].reshape(dec_batch, D_SSM, SSM_STATE),
                          cw, cb, alog, dskip, sg, sels, n_p_tok, dec_batch, dec_seq)
        oc = jnp.concatenate([oc_p, oc_s], axis=0)

        wr = jnp.pad(jnp.concatenate([w_router_expert[l], w_router_group[l]], axis=1),
                     ((0, 0), (0, LANES - N_EXPERTS - N_EGROUPS)))
        br = _pad_lanes(jnp.concatenate([b_router_expert[l], b_router_group[l]]))
        x1, h2, comb = _out_projection(oab, oc, x, mod[l], norm2_g[l].reshape(1, D_MODEL),
                                       w_out[l].astype(bf16), wr, br, n_p_tok, dec_seq)

        x = _moe(h2, comb, w_gate[l].astype(bf16), w_up[l].astype(bf16), w_down[l].astype(bf16),
                 x1, mod[l], final_norm_g.reshape(1, D_MODEL), n_p_tok, dec_seq, final_norm=(l == depth - 1))

        kv_shape = (batch, seq, A_KV, HEAD_DIM)
        new[0].append(ka[:n_p_tok].reshape(kv_shape))
        new[1].append(va[:n_p_tok].reshape(kv_shape))
        new[2].append(kb[:n_p_tok].reshape(kv_shape))
        new[3].append(vb[:n_p_tok].reshape(kv_shape))
        new[4].append(hf.reshape(batch, SSM_HEADS, SSM_HEAD_DIM, SSM_STATE))
        new[5].append(hb.reshape(batch, SSM_HEADS, SSM_HEAD_DIM, SSM_STATE))

    y_prompt = x[:n_p_tok].reshape(batch, seq, D_MODEL)
    y_sample = x[n_p_tok:].reshape(dec_batch, dec_seq, D_MODEL)
    return (y_prompt, y_sample) + tuple(jnp.stack(v, axis=1) for v in new)
```

```python
import functools

import jax
import jax.numpy as jnp
import numpy as np
from jax import lax
from jax.experimental import pallas as pl
from jax.experimental.pallas import tpu as pltpu

f32 = jnp.float32
bf16 = jnp.bfloat16
HIGHEST = lax.Precision.HIGHEST

D_MODEL = 1024
GRID_W = 64
HEAD_DIM = 64
A_HEADS = 4
A_KV = 2
WINDOW = 128
B_HEADS = 4
B_KV = 2
ROPE_THETA = 10000.0
ROPE_QUARTER = HEAD_DIM // 4
SSM_HEADS = 8
SSM_HEAD_DIM = 64
D_SSM = SSM_HEADS * SSM_HEAD_DIM
SSM_GROUPS = 2
SSM_STATE = 64
CHUNK = 128
XBC_DIM = D_SSM + 2 * SSM_GROUPS * SSM_STATE
D_AB = (A_HEADS + B_HEADS) * HEAD_DIM
D_Q = A_HEADS * HEAD_DIM
D_KV = A_KV * HEAD_DIM
N_EGROUPS = 4
EXPERTS_PER_GROUP = 4
N_EXPERTS = N_EGROUPS * EXPERTS_PER_GROUP
D_EXPERT = 256
EPS = 1e-6

LANES = 128
C_AQ, C_AK, C_AV, C_BQ, C_BK, C_BV, C_Z, C_XBC, C_DT = 0, 256, 384, 512, 768, 896, 1024, 1536, 2304

VMEM_LIMIT = 56 * 1024 * 1024
ANY = pl.BlockSpec(memory_space=pl.ANY)
SMEM = pl.BlockSpec(memory_space=pltpu.SMEM)


def _params(sem, vmem=VMEM_LIMIT):
    return pltpu.CompilerParams(dimension_semantics=sem, vmem_limit_bytes=vmem)


def _dot(a, b, **kw):
    return jnp.dot(a, b, preferred_element_type=f32, **kw)


def _dot_nt(a, b):
    return lax.dot_general(a, b, (((1,), (1,)), ((), ())), preferred_element_type=f32)


def _dot_tn(a, b):
    return lax.dot_general(a, b, (((0,), (0,)), ((), ())), preferred_element_type=f32)


def _silu(x):
    return x / (1.0 + jnp.exp(-x))


def _softplus(x):
    return jnp.maximum(x, 0.0) + jnp.log1p(jnp.exp(-jnp.abs(x)))


def _rms(x):
    return x * lax.rsqrt(jnp.mean(x * x, axis=-1, keepdims=True) + EPS)


class _Tiling:
    def __init__(self, n_p_tok, n_s_tok, dec_seq, tm):
        self.tm = tm
        self.n_p = n_p_tok // tm
        self.n_s = n_s_tok // tm
        self.n = self.n_p + self.n_s
        self.per_seq = dec_seq // tm

    def p_idx(self, i):
        return jnp.minimum(i, self.n_p - 1)

    def s_idx(self, i):
        return jnp.maximum(i - self.n_p, 0)

    def mod_row(self, i):
        return jnp.where(i < self.n_p, 0, 1 + (i - self.n_p) // self.per_seq)

    def seq_pos(self, i):
        return jnp.where(i < self.n_p, 0, (i - self.n_p) % self.per_seq)


def _x_specs(til, split):
    tm = til.tm
    if split:
        return [pl.BlockSpec((tm, D_MODEL), lambda i, *_: (til.p_idx(i), 0)),
                pl.BlockSpec((tm, D_MODEL), lambda i, *_: (til.s_idx(i), 0))]
    return [pl.BlockSpec((tm, D_MODEL), lambda i, *_: (i, 0))]


def _load_x(refs, i, n_p):
    if len(refs) == 2:
        return jnp.where(i < n_p, refs[0][...], refs[1][...])
    return refs[0][...]


def _mod_spec(til, layer, k):
    return pl.BlockSpec((None, None, None, 1, D_MODEL), lambda i, *_: (layer, til.mod_row(i), k, 0, 0))


def _mod_kernel(c_ref, w_ref, b_ref, o_ref):
    s = _silu(c_ref[...])
    o_ref[...] = _dot(s, w_ref[...], precision=HIGHEST) + b_ref[...]


def _modulation(cvec, w_ada, b_ada):
    depth = w_ada.shape[0]
    n = w_ada.shape[2]
    tn = 1536
    return pl.pallas_call(
        _mod_kernel,
        grid=(depth, n // tn),
        in_specs=[
            pl.BlockSpec((8, D_MODEL), lambda l, j: (0, 0)),
            pl.BlockSpec((None, D_MODEL, tn), lambda l, j: (l, 0, j)),
            pl.BlockSpec((None, 1, tn), lambda l, j: (l, 0, j)),
        ],
        out_specs=pl.BlockSpec((None, 8, tn), lambda l, j: (l, 0, j)),
        out_shape=jax.ShapeDtypeStruct((depth, 8, n), f32),
        compiler_params=_params(("arbitrary", "arbitrary")),
        name="adaln_mod",
    )(cvec, w_ada, b_ada.reshape(depth, 1, n))


def _rope(x, cos, sin_even, sin_odd):
    w = x.shape[-1]
    nxt = pltpu.roll(x, w - ROPE_QUARTER, 1)
    prv = pltpu.roll(x, ROPE_QUARTER, 1)
    return x * cos + nxt * sin_even + prv * sin_odd


def _inproj_kernel(*refs, n_x, n_alias, n_p, seqs_per_tile):
    x_refs = refs[:n_x]
    (sh_ref, sc_ref, g_ref, w_ref, wdt_ref, qg_ref, kg_ref, dtb_ref, hm_ref,
     cos_ref, se_ref, so_ref) = refs[n_x:n_x + 12]
    (qa_ref, qb_ref, akp_ref, avp_ref, bkp_ref, bvp_ref, aks_ref, avs_ref, bks_ref, bvs_ref,
     z_ref, xbc_ref, dt_ref) = refs[n_x + 12 + n_alias:]
    i = pl.program_id(0)
    h = _rms(_load_x(x_refs, i, n_p)) * g_ref[...]
    h = h * (1.0 + sc_ref[...]) + sh_ref[...]
    hb = h.astype(bf16)

    def proj(lo, hi):
        return _dot(hb, w_ref[:, lo:hi])

    def head_norm(t, gain):
        w = t.shape[-1]
        ms_h = _dot(t * t, hm_ref[0:w, 0:w], precision=HIGHEST)
        return t * lax.rsqrt(ms_h + EPS) * gain

    qa = proj(C_AQ, C_AK)
    ka = proj(C_AK, C_AV)
    va = proj(C_AV, C_BQ)
    qb = head_norm(proj(C_BQ, C_BK), qg_ref[...])
    kb = head_norm(proj(C_BK, C_BV), kg_ref[:, 0:D_KV])
    vb = proj(C_BV, C_Z)
    z_ref[...] = proj(C_Z, C_XBC).astype(bf16)
    xbc_ref[...] = proj(C_XBC, C_DT).astype(bf16)
    dt_ref[...] = _softplus(_dot(hb, wdt_ref[...]) + dtb_ref[...])

    @pl.when(i < n_p)
    def _():
        qa_ref[...] = qa.astype(bf16)
        qb_ref[...] = qb.astype(bf16)
        shp = (seqs_per_tile, -1, D_KV)
        akp_ref[...] = ka.reshape(shp)
        avp_ref[...] = va.reshape(shp)
        bkp_ref[...] = kb.reshape(shp)
        bvp_ref[...] = vb.reshape(shp)

    @pl.when(i >= n_p)
    def _():
        cos, se, so = cos_ref[...], se_ref[...], so_ref[...]
        qa_ref[...] = _rope(qa, cos, se, so).astype(bf16)
        qb_ref[...] = _rope(qb, cos, se, so).astype(bf16)
        aks_ref[...] = _rope(ka, cos[:, :D_KV], se[:, :D_KV], so[:, :D_KV]).astype(bf16)
        bks_ref[...] = _rope(kb, cos[:, :D_KV], se[:, :D_KV], so[:, :D_KV]).astype(bf16)
        avs_ref[...] = va.astype(bf16)
        bvs_ref[...] = vb.astype(bf16)


def _in_projection(xs, mod, layer, norm_g, w_main, w_dt, qg, kg, dtb, headmat, rope, kv_prev, dims, tm=512):
    batch, seq, dec_batch, dec_seq, depth = dims
    n_p_tok, n_s_tok = batch * seq, dec_batch * dec_seq
    t = n_p_tok + n_s_tok
    til = _Tiling(n_p_tok, n_s_tok, dec_seq, tm)
    spt = tm // seq
    cos, se, so = rope
    n_alias = len(kv_prev)

    def c2(shape):
        return pl.BlockSpec(shape, lambda i: (0, 0))

    def lspec(shape):
        return pl.BlockSpec((None,) + shape, lambda i: (layer,) + (0,) * len(shape))

    rope_spec = pl.BlockSpec((tm, D_Q), lambda i: (til.seq_pos(i), 0))
    tok = lambda w: pl.BlockSpec((tm, w), lambda i: (i, 0))
    kvp = pl.BlockSpec((spt, None, seq, D_KV), lambda i: (til.p_idx(i), layer, 0, 0))
    kvs = pl.BlockSpec((tm, D_KV), lambda i: (til.s_idx(i), 0))
    kvp_shape = jax.ShapeDtypeStruct((batch, depth, seq, D_KV), f32)
    kvs_shape = jax.ShapeDtypeStruct((n_s_tok, D_KV), bf16)
    n_in = len(xs) + 12
    return pl.pallas_call(
        functools.partial(_inproj_kernel, n_x=len(xs), n_alias=n_alias, n_p=til.n_p, seqs_per_tile=spt),
        grid=(til.n,),
        in_specs=_x_specs(til, len(xs) == 2) + [
            _mod_spec(til, layer, 0), _mod_spec(til, layer, 1),
            lspec((1, D_MODEL)), lspec((D_MODEL, C_DT)), lspec((D_MODEL, LANES)),
            lspec((1, D_Q)), lspec((1, D_Q)), lspec((1, LANES)), c2((D_Q, D_Q)),
            rope_spec, rope_spec, rope_spec] + [ANY] * n_alias,
        out_specs=[tok(D_Q), tok(D_Q), kvp, kvp, kvp, kvp, kvs, kvs, kvs, kvs,
                   tok(D_SSM), tok(XBC_DIM), tok(LANES)],
        out_shape=[jax.ShapeDtypeStruct((t, D_Q), bf16), jax.ShapeDtypeStruct((t, D_Q), bf16),
                   kvp_shape, kvp_shape, kvp_shape, kvp_shape,
                   kvs_shape, kvs_shape, kvs_shape, kvs_shape,
                   jax.ShapeDtypeStruct((t, D_SSM), bf16), jax.ShapeDtypeStruct((t, XBC_DIM), bf16),
                   jax.ShapeDtypeStruct((t, LANES), f32)],
        input_output_aliases={n_in + k: 2 + k for k in range(n_alias)},
        compiler_params=_params(("arbitrary",)),
        name="norm_mod_inproj",
    )(*xs, mod, mod, norm_g, w_main, w_dt, qg, kg, dtb, headmat, cos, se, so, *kv_prev)


def _softmax_pv(scores, values, sink):
    m = scores[0].max(axis=-1, keepdims=True)
    for s in scores[1:]:
        m = jnp.maximum(m, s.max(axis=-1, keepdims=True))
    if sink is not None:
        m = jnp.maximum(m, sink)
    den = None
    acc = None
    for s, v in zip(scores, values):
        p = jnp.exp(s - m)
        d = p.sum(axis=-1, keepdims=True)
        o = _dot(p.astype(bf16), v)
        den = d if den is None else den + d
        acc = o if acc is None else acc + o
    if sink is not None:
        den = den + jnp.exp(sink - m)
    return acc / den


def _attn_ctx_kernel(sink_ref, qa_ref, ka_ref, va_ref, qb_ref, kb_ref, vb_ref, o_ref, *, layer):
    scale = HEAD_DIM ** -0.5
    for mixer, (q_ref, k_ref, v_ref) in enumerate(((qa_ref, ka_ref, va_ref), (qb_ref, kb_ref, vb_ref))):
        for kv in range(A_KV):
            ks = slice(kv * HEAD_DIM, (kv + 1) * HEAD_DIM)
            k = k_ref[:, ks].astype(bf16)
            v = v_ref[:, ks].astype(bf16)
            for g in range(A_HEADS // A_KV):
                hd = kv * 2 + g
                hs = slice(hd * HEAD_DIM, (hd + 1) * HEAD_DIM)
                q = q_ref[:, hs] * scale
                s = _dot_nt(q, k)
                sink = sink_ref[layer, hd] if mixer == 0 else None
                o = _softmax_pv([s], [v], sink)
                os_ = slice(mixer * D_Q + hd * HEAD_DIM, mixer * D_Q + (hd + 1) * HEAD_DIM)
                o_ref[:, os_] = o.astype(bf16)


def _attention_ctx(sink, qa, qb, kvp, layer, dims):
    batch, seq, dec_batch, dec_seq, depth = dims
    t = batch * seq + dec_batch * dec_seq
    qspec = pl.BlockSpec((seq, D_Q), lambda b: (b, 0))
    kspec = pl.BlockSpec((None, None, seq, D_KV), lambda b: (b, layer, 0, 0))
    akp, avp, bkp, bvp = kvp
    return pl.pallas_call(
        functools.partial(_attn_ctx_kernel, layer=layer),
        grid=(batch,),
        in_specs=[SMEM, qspec, kspec, kspec, qspec, kspec, kspec],
        out_specs=pl.BlockSpec((seq, D_AB), lambda b: (b, 0)),
        out_shape=jax.ShapeDtypeStruct((t, D_AB), bf16),
        compiler_params=_params(("arbitrary",)),
        name="attn_context",
    )(sink, qa, akp, avp, qb, bkp, bvp)


def _attn_lat_kernel(sink_ref, qa_ref, ka_ref, va_ref, cka_ref, cva_ref,
                     qb_ref, kb_ref, vb_ref, ckb_ref, cvb_ref, alias_ref, o_ref, *, seq, layer):
    del alias_ref
    j = pl.program_id(1)
    scale = HEAD_DIM ** -0.5
    w = WINDOW
    start = pl.multiple_of(jnp.clip((j - 1) * w, 0, seq - 3 * w), w)
    qi = j * w + lax.broadcasted_iota(jnp.int32, (w, 3 * w), 0)
    ki = start + lax.broadcasted_iota(jnp.int32, (w, 3 * w), 1)
    valid = jnp.abs(ki - qi) <= w
    for kv in range(A_KV):
        ks = slice(kv * HEAD_DIM, (kv + 1) * HEAD_DIM)
        ka = ka_ref[pl.ds(start, 3 * w), ks]
        va = va_ref[pl.ds(start, 3 * w), ks]
        cka = cka_ref[:, ks].astype(bf16)
        cva = cva_ref[:, ks].astype(bf16)
        kb = kb_ref[:, ks]
        vb = vb_ref[:, ks]
        ckb = ckb_ref[:, ks].astype(bf16)
        cvb = cvb_ref[:, ks].astype(bf16)
        for g in range(A_HEADS // A_KV):
            hd = kv * 2 + g
            hs = slice(hd * HEAD_DIM, (hd + 1) * HEAD_DIM)
            q = qa_ref[:, hs] * scale
            s_loc = jnp.where(valid, _dot_nt(q, ka), -jnp.inf)
            s_ctx = _dot_nt(q, cka)
            o_ref[:, hs] = _softmax_pv([s_loc, s_ctx], [va, cva], sink_ref[layer, hd]).astype(bf16)
            q = qb_ref[:, hs] * scale
            o = _softmax_pv([_dot_nt(q, kb), _dot_nt(q, ckb)], [vb, cvb], None)
            o_ref[:, slice(D_Q + hd * HEAD_DIM, D_Q + (hd + 1) * HEAD_DIM)] = o.astype(bf16)


def _attention_lat(sink, qa, qb, kvs, caches, oab, layer, dims):
    batch, seq, dec_batch, dec_seq, depth = dims
    w = WINDOW
    nq = dec_seq // w
    q0 = batch * seq // w
    past = caches[0].shape[2]
    qspec = pl.BlockSpec((w, D_Q), lambda b, j: (q0 + b * nq + j, 0))
    kspec = pl.BlockSpec((dec_seq, D_KV), lambda b, j: (b, 0))
    cspec = pl.BlockSpec((None, None, past, D_KV), lambda b, j: (b, layer, 0, 0))
    aks, avs, bks, bvs = kvs
    cka, cva, ckb, cvb = caches
    return pl.pallas_call(
        functools.partial(_attn_lat_kernel, seq=dec_seq, layer=layer),
        grid=(dec_batch, nq),
        in_specs=[SMEM, qspec, kspec, kspec, cspec, cspec, qspec, kspec, kspec, cspec, cspec, ANY],
        out_specs=pl.BlockSpec((w, D_AB), lambda b, j: (q0 + b * nq + j, 0)),
        out_shape=jax.ShapeDtypeStruct(oab.shape, oab.dtype),
        input_output_aliases={11: 0},
        compiler_params=_params(("arbitrary", "arbitrary")),
        name="attn_latent",
    )(sink, qa, aks, avs, cka, cva, qb, bks, bvs, ckb, cvb, oab)


def _ssd_kernel(*refs, latent, n_alias):
    xbc_ref, z_ref, dt_ref, cw_ref, cb_ref, alog_ref, dskip_ref, g_ref = refs[:8]
    if latent:
        h0f_ref, h0b_ref = refs[8:10]
        o_ref, xc_scr, y_scr, hf_ref, hb_ref = refs[10 + n_alias:]
    else:
        o_ref, hf_ref, hb_ref, xc_scr, y_scr = refs[8 + n_alias:]
    n = xbc_ref.shape[0]
    nc = n // CHUNK
    L = CHUNK
    ns = SSM_STATE
    gw = D_SSM // SSM_GROUPS
    hpg = SSM_HEADS // SSM_GROUPS
    hd_w = SSM_HEAD_DIM

    x = xbc_ref[...].astype(f32)
    t_idx = lax.broadcasted_iota(jnp.int32, x.shape, 0)
    prv = jnp.where(t_idx == 0, 0.0, pltpu.roll(x, 1, 0))
    nxt = jnp.where(t_idx == n - 1, 0.0, pltpu.roll(x, n - 1, 0))
    y = prv * cw_ref[0:1, :] + x * cw_ref[1:2, :] + nxt * cw_ref[2:3, :] + cb_ref[...]
    xc_scr[...] = _silu(y)

    if latent:
        hf_ref[...] = h0f_ref[...]
        hb_ref[...] = h0b_ref[...]
    else:
        hf_ref[...] = jnp.zeros(hf_ref.shape, f32)
        hb_ref[...] = jnp.zeros(hb_ref.shape, f32)

    a_neg = -jnp.exp(alog_ref[...])
    r_i = lax.broadcasted_iota(jnp.int32, (L, L), 0)
    c_i = lax.broadcasted_iota(jnp.int32, (L, L), 1)
    lower = r_i >= c_i
    upper = r_i <= c_i
    tril = lower.astype(f32)
    triu = upper.astype(f32)

    def chunk_vals(c):
        rows = pl.ds(pl.multiple_of(c * L, L), L)
        xs = xc_scr[rows, 0:D_SSM]
        bm = xc_scr[rows, D_SSM:D_SSM + SSM_GROUPS * ns].astype(bf16)
        cm = xc_scr[rows, D_SSM + SSM_GROUPS * ns:XBC_DIM].astype(bf16)
        dt = dt_ref[rows, :]
        return rows, xs, bm, cm, dt

    def state_pass(rows, xs, bm, cm, dt, cum, cum_t, edge, lane0, h_ref):
        e_in = jnp.exp(cum)
        wts = jnp.exp(cum[edge:edge + 1, :] - cum) * dt
        dec = jnp.exp(cum_t[:, edge:edge + 1])
        h16 = h_ref[...].astype(bf16)
        for g in range(SSM_GROUPS):
            yo = _dot_nt(cm[:, g * ns:(g + 1) * ns], h16[g * gw:(g + 1) * gw, :])
            xw = []
            for hh in range(hpg):
                hd = g * hpg + hh
                ln = lane0 + hd
                cols = slice(hd * hd_w, (hd + 1) * hd_w)
                y_scr[rows, cols] += yo[:, hh * hd_w:(hh + 1) * hd_w] * e_in[:, ln:ln + 1]
                xw.append(xs[:, cols] * wts[:, ln:ln + 1])
            st = _dot_tn(jnp.concatenate(xw, axis=1).astype(bf16), bm[:, g * ns:(g + 1) * ns])
            for hh in range(hpg):
                hd = g * hpg + hh
                ln = lane0 + hd
                hrows = slice(hd * hd_w, (hd + 1) * hd_w)
                h_ref[hrows, :] = h_ref[hrows, :] * dec[ln:ln + 1, :] + st[hh * hd_w:(hh + 1) * hd_w, :]

    def fwd_body(c, carry):
        rows, xs, bm, cm, dt = chunk_vals(c)
        da = dt * a_neg
        cs = _dot(tril, da, precision=HIGHEST)
        suf = _dot(triu, da, precision=HIGHEST)
        cs_t, suf_t, dt_t = cs.T, suf.T, dt.T
        xb16 = xs.astype(bf16)
        for g in range(SSM_GROUPS):
            cb = _dot_nt(cm[:, g * ns:(g + 1) * ns], bm[:, g * ns:(g + 1) * ns])
            for hh in range(hpg):
                hd = g * hpg + hh
                hb_ = SSM_HEADS + hd
                lf = jnp.exp(jnp.where(lower, cs[:, hd:hd + 1] - cs_t[hd:hd + 1, :], -jnp.inf))
                lb = jnp.exp(jnp.where(upper, suf[:, hb_:hb_ + 1] - suf_t[hb_:hb_ + 1, :], -jnp.inf))
                m = cb * (lf * dt_t[hd:hd + 1, :] + lb * dt_t[hb_:hb_ + 1, :])
                cols = slice(hd * hd_w, (hd + 1) * hd_w)
                y_scr[rows, cols] = _dot(m.astype(bf16), xb16[:, cols])
        state_pass(rows, xs, bm, cm, dt, cs, cs_t, L - 1, 0, hf_ref)
        return carry

    def bwd_body(k, carry):
        rows, xs, bm, cm, dt = chunk_vals(nc - 1 - k)
        suf = _dot(triu, dt * a_neg, precision=HIGHEST)
        state_pass(rows, xs, bm, cm, dt, suf, suf.T, 0, SSM_HEADS, hb_ref)
        return carry

    lax.fori_loop(0, nc, fwd_body, 0)
    lax.fori_loop(0, nc, bwd_body, 0)

    yv = y_scr[...] + xc_scr[:, 0:D_SSM] * dskip_ref[...]
    yv = yv * _silu(z_ref[...].astype(f32))
    o_ref[...] = (_rms(yv) * g_ref[...]).astype(bf16)


def _ssd(xbc, z, dt, consts, layer, dims, *, latent, init=None, oc=None, st_prev=()):
    batch, seq, dec_batch, dec_seq, depth = dims
    t = batch * seq + dec_batch * dec_seq
    if latent:
        n_b, n, b0 = dec_batch, dec_seq, batch * seq // dec_seq
    else:
        n_b, n, b0 = batch, seq, 0

    def tok(width):
        return pl.BlockSpec((n, width), lambda b: (b0 + b, 0))

    def lspec(shape):
        return pl.BlockSpec((None,) + shape, lambda b: (layer,) + (0,) * len(shape))

    st = pl.BlockSpec((None, None, D_SSM, SSM_STATE), lambda b: (b, layer, 0, 0))
    in_specs = [tok(XBC_DIM), tok(D_SSM), tok(LANES),
                lspec((3, XBC_DIM)), lspec((1, XBC_DIM)), lspec((1, LANES)), lspec((1, D_SSM)),
                lspec((1, D_SSM))]
    args = [xbc, z, dt, *consts]
    oc_shape = jax.ShapeDtypeStruct((t, D_SSM), bf16)
    oc_spec = pl.BlockSpec((n, D_SSM), lambda b: (b0 + b, 0))
    scratch = [pltpu.VMEM((n, XBC_DIM), f32), pltpu.VMEM((n, D_SSM), f32)]
    if latent:
        in_specs += [st, st, ANY]
        args += [init[0], init[1], oc]
        out_specs, out_shape = [oc_spec], [oc_shape]
        aliases = {10: 0}
        n_alias = 1
        scratch += [pltpu.VMEM((D_SSM, SSM_STATE), f32), pltpu.VMEM((D_SSM, SSM_STATE), f32)]
    else:
        n_alias = len(st_prev)
        in_specs += [ANY] * n_alias
        args += list(st_prev)
        st_shape = jax.ShapeDtypeStruct((batch, depth, D_SSM, SSM_STATE), f32)
        out_specs, out_shape = [oc_spec, st, st], [oc_shape, st_shape, st_shape]
        aliases = {8 + k: 1 + k for k in range(n_alias)}
    return pl.pallas_call(
        functools.partial(_ssd_kernel, latent=latent, n_alias=n_alias),
        grid=(n_b,),
        in_specs=in_specs, out_specs=out_specs, out_shape=out_shape,
        scratch_shapes=scratch,
        input_output_aliases=aliases,
        compiler_params=_params(("arbitrary",)),
        name="ssd_latent" if latent else "ssd_context",
    )(*args)


def _outproj_kernel(*refs, n_x, n_p):
    x_refs = refs[:n_x]
    (oab_ref, oc_ref, g1_ref, sh_ref, sc_ref, n2_ref, w_ref, wr_ref, br_ref,
     x1_ref, h2_ref, comb_ref) = refs[n_x:]
    i = pl.program_id(0)
    o = _dot(oab_ref[...], w_ref[0:D_AB, :]) + _dot(oc_ref[...], w_ref[D_AB:, :])
    x1 = _load_x(x_refs, i, n_p) + g1_ref[...] * o
    x1_ref[...] = x1
    h2 = _rms(x1) * n2_ref[...]
    h2 = h2 * (1.0 + sc_ref[...]) + sh_ref[...]
    h2_ref[...] = h2.astype(bf16)

    logits = _dot(h2, wr_ref[...], precision=HIGHEST) + br_ref[...]
    lane = lax.broadcasted_iota(jnp.int32, logits.shape, 1).astype(f32)
    big = float(LANES)
    neg = -jnp.inf
    gmask = (lane >= N_EXPERTS) & (lane < N_EXPERTS + N_EGROUPS)
    gl = jnp.where(gmask, logits, neg)
    gmax = gl.max(axis=-1, keepdims=True)
    gsel = jnp.where(gl == gmax, lane, big).min(axis=-1, keepdims=True) - N_EXPERTS
    gprob = 1.0 / jnp.where(gmask, jnp.exp(logits - gmax), 0.0).sum(axis=-1, keepdims=True)
    emask = (lane >= gsel * EXPERTS_PER_GROUP) & (lane < (gsel + 1) * EXPERTS_PER_GROUP)
    el = jnp.where(emask, logits, neg)
    v1 = el.max(axis=-1, keepdims=True)
    i1 = jnp.where(el == v1, lane, big).min(axis=-1, keepdims=True)
    el2 = jnp.where(lane == i1, neg, el)
    v2 = el2.max(axis=-1, keepdims=True)
    i2 = jnp.where(el2 == v2, lane, big).min(axis=-1, keepdims=True)
    e2 = jnp.exp(v2 - v1)
    den = 1.0 + e2
    comb_ref[...] = (jnp.where(lane == i1, gprob / den, 0.0)
                     + jnp.where(lane == i2, gprob * e2 / den, 0.0))


def _out_projection(oab, oc, xs, mod, layer, norm_g, w_out, wr, br, dims, tm=512):
    batch, seq, dec_batch, dec_seq, depth = dims
    n_p_tok, n_s_tok = batch * seq, dec_batch * dec_seq
    t = n_p_tok + n_s_tok
    til = _Tiling(n_p_tok, n_s_tok, dec_seq, tm)

    def lspec(shape):
        return pl.BlockSpec((None,) + shape, lambda i: (layer,) + (0,) * len(shape))

    tok = lambda w: pl.BlockSpec((tm, w), lambda i: (i, 0))
    return pl.pallas_call(
        functools.partial(_outproj_kernel, n_x=len(xs), n_p=til.n_p),
        grid=(til.n,),
        in_specs=_x_specs(til, len(xs) == 2) + [
            tok(D_AB), tok(D_SSM),
            _mod_spec(til, layer, 2), _mod_spec(til, layer, 3), _mod_spec(til, layer, 4),
            lspec((1, D_MODEL)), lspec((D_AB + D_SSM, D_MODEL)), lspec((D_MODEL, LANES)), lspec((1, LANES))],
        out_specs=[tok(D_MODEL), tok(D_MODEL), tok(LANES)],
        out_shape=[jax.ShapeDtypeStruct((t, D_MODEL), f32),
                   jax.ShapeDtypeStruct((t, D_MODEL), bf16),
                   jax.ShapeDtypeStruct((t, LANES), f32)],
        compiler_params=_params(("arbitrary",)),
        name="outproj_norm_router",
    )(*xs, oab, oc, mod, mod, mod, norm_g, w_out, wr, br)


def _moe_kernel(h2_ref, comb_ref, wg_ref, wu_ref, wd_ref, x1_ref, g2_ref, fg_ref, *rest, final, n_p):
    i = pl.program_id(0)
    e = pl.program_id(1)
    acc_ref = rest[-1]
    h = h2_ref[...]
    a = _dot(h, wg_ref[...])
    u = _dot(h, wu_ref[...])
    comb = comb_ref[...]
    lane = lax.broadcasted_iota(jnp.int32, comb.shape, 1)
    ce = jnp.where(lane == e, comb, 0.0).sum(axis=-1, keepdims=True)
    hid = _silu(a) * u * ce
    y = _dot(hid.astype(bf16), wd_ref[...])

    @pl.when(e == 0)
    def _():
        acc_ref[...] = y

    @pl.when(e > 0)
    def _():
        acc_ref[...] += y

    last = e == N_EXPERTS - 1
    if final:
        yp_ref, ys_ref = rest[:2]

        @pl.when(last & (i < n_p))
        def _():
            yp_ref[...] = _rms(x1_ref[...] + g2_ref[...] * acc_ref[...]) * fg_ref[...]

        @pl.when(last & (i >= n_p))
        def _():
            ys_ref[...] = _rms(x1_ref[...] + g2_ref[...] * acc_ref[...]) * fg_ref[...]
    else:
        @pl.when(last)
        def _():
            rest[0][...] = x1_ref[...] + g2_ref[...] * acc_ref[...]


def _moe(h2, comb, wg, wu, wd, x1, mod, layer, final_g, dims, final, tm=1024):
    batch, seq, dec_batch, dec_seq, depth = dims
    n_p_tok, n_s_tok = batch * seq, dec_batch * dec_seq
    t = n_p_tok + n_s_tok
    til = _Tiling(n_p_tok, n_s_tok, dec_seq, tm)
    wspec = lambda a, b: pl.BlockSpec((None, None, a, b), lambda i, e: (layer, e, 0, 0))
    tok = lambda w: pl.BlockSpec((tm, w), lambda i, e: (i, 0))
    if final:
        out_specs = [pl.BlockSpec((tm, D_MODEL), lambda i, e: (til.p_idx(i), 0)),
                     pl.BlockSpec((tm, D_MODEL), lambda i, e: (til.s_idx(i), 0))]
        out_shape = [jax.ShapeDtypeStruct((n_p_tok, D_MODEL), f32),
                     jax.ShapeDtypeStruct((n_s_tok, D_MODEL), f32)]
    else:
        out_specs = [tok(D_MODEL)]
        out_shape = [jax.ShapeDtypeStruct((t, D_MODEL), f32)]
    return pl.pallas_call(
        functools.partial(_moe_kernel, final=final, n_p=til.n_p),
        grid=(til.n, N_EXPERTS),
        in_specs=[tok(D_MODEL), tok(LANES),
                  wspec(D_MODEL, D_EXPERT), wspec(D_MODEL, D_EXPERT), wspec(D_EXPERT, D_MODEL),
                  tok(D_MODEL), _mod_spec(til, layer, 5),
                  pl.BlockSpec((1, D_MODEL), lambda i, e: (0, 0))],
        out_specs=out_specs, out_shape=out_shape,
        scratch_shapes=[pltpu.VMEM((tm, D_MODEL), f32)],
        compiler_params=_params(("arbitrary", "arbitrary")),
        name="moe_ffn",
    )(h2, comb, wg, wu, wd, x1, mod, final_g)


def _rope_tables(n_rows):
    rows = jnp.repeat(jnp.arange(n_rows), GRID_W).astype(f32)
    cols = jnp.tile(jnp.arange(GRID_W), n_rows).astype(f32)
    inv = ROPE_THETA ** (-jnp.arange(ROPE_QUARTER, dtype=f32) / ROPE_QUARTER)
    ang_r = rows[:, None] * inv
    ang_c = cols[:, None] * inv
    ang = jnp.concatenate([ang_r, ang_r, ang_c, ang_c], axis=-1)
    cos, sin = jnp.cos(ang), jnp.sin(ang)
    even = (np.arange(HEAD_DIM) // ROPE_QUARTER) % 2 == 0
    sin_even = jnp.where(even, -sin, 0.0)
    sin_odd = jnp.where(even, 0.0, sin)
    return tuple(jnp.tile(t, (1, D_Q // HEAD_DIM)) for t in (cos, sin_even, sin_odd))


def _head_mean_matrix():
    m = np.zeros((D_Q, D_Q), np.float32)
    for h in range(D_Q // HEAD_DIM):
        m[h * HEAD_DIM:(h + 1) * HEAD_DIM, h * HEAD_DIM:(h + 1) * HEAD_DIM] = 1.0 / HEAD_DIM
    return jnp.asarray(m)


def _pad_last(v, width=LANES):
    pad = [(0, 0)] * (v.ndim - 1) + [(0, width - v.shape[-1])]
    return jnp.pad(v, pad)


@jax.jit
def kernel(x_prompt, x_sample, cache_a_k, cache_a_v, cache_b_k, cache_b_v, state_ssm_fwd, state_ssm_bwd, c, c_ctx, norm1_g, norm2_g, final_norm_g, w_ada, b_ada, w_in, a_sink, q_norm_g, k_norm_g, conv_w, conv_b, dt_bias, a_log, d_skip, ssm_norm_g, w_out, w_router_group, b_router_group, w_router_expert, b_router_expert, w_gate, w_up, w_down):
    batch, seq, _ = x_prompt.shape
    dec_batch, dec_seq, _ = x_sample.shape
    depth = w_in.shape[0]
    past = cache_a_k.shape[2]
    dims = (batch, seq, dec_batch, dec_seq, depth)
    n_p_tok = batch * seq
    n_s_tok = dec_batch * dec_seq

    cvec = jnp.concatenate([c_ctx[None, :], c, jnp.zeros((8 - 1 - dec_batch, D_MODEL), f32)], axis=0)
    mod = _modulation(cvec, w_ada, b_ada).reshape(depth, 8, 6, 1, D_MODEL)

    rope = _rope_tables(dec_seq // GRID_W)
    headmat = _head_mean_matrix()
    w_main = w_in[:, :, :C_DT].astype(bf16)
    w_dt = _pad_last(w_in[:, :, C_DT:]).astype(bf16)
    qg = jnp.tile(q_norm_g, (1, D_Q // HEAD_DIM)).reshape(depth, 1, D_Q)
    kg = jnp.tile(k_norm_g, (1, D_Q // HEAD_DIM)).reshape(depth, 1, D_Q)
    dtb = _pad_last(dt_bias.reshape(depth, 1, 2 * SSM_HEADS))
    n1 = norm1_g.reshape(depth, 1, D_MODEL)
    n2 = norm2_g.reshape(depth, 1, D_MODEL)
    ssd_consts = (conv_w, conv_b.reshape(depth, 1, XBC_DIM),
                  _pad_last(a_log.reshape(depth, 1, 2 * SSM_HEADS)),
                  jnp.repeat(d_skip, SSM_HEAD_DIM, axis=-1).reshape(depth, 1, D_SSM),
                  ssm_norm_g.reshape(depth, 1, D_SSM))
    w_out16 = w_out.astype(bf16)
    wr = _pad_last(jnp.concatenate([w_router_expert, w_router_group], axis=-1))
    br = _pad_last(jnp.concatenate([b_router_expert, b_router_group], axis=-1)).reshape(depth, 1, LANES)
    wg16, wu16, wd16 = w_gate.astype(bf16), w_up.astype(bf16), w_down.astype(bf16)
    fg = final_norm_g.reshape(1, D_MODEL)

    caches = tuple(t.reshape(dec_batch, depth, past, D_KV) for t in (cache_a_k, cache_a_v, cache_b_k, cache_b_v))
    init = (state_ssm_fwd.reshape(dec_batch, depth, D_SSM, SSM_STATE),
            state_ssm_bwd.reshape(dec_batch, depth, D_SSM, SSM_STATE))

    xs = (x_prompt.reshape(n_p_tok, D_MODEL), x_sample.reshape(n_s_tok, D_MODEL))
    kvp, states = (), ()
    for l in range(depth):
        outs = _in_projection(xs, mod, l, n1, w_main, w_dt, qg, kg, dtb, headmat, rope, kvp, dims)
        qa, qb = outs[0:2]
        kvp, kvs = tuple(outs[2:6]), tuple(outs[6:10])
        z, xbc, dt = outs[10:13]

        oab = _attention_ctx(a_sink, qa, qb, kvp, l, dims)
        oab = _attention_lat(a_sink, qa, qb, kvs, caches, oab, l, dims)

        oc, hf, hb = _ssd(xbc, z, dt, ssd_consts, l, dims, latent=False, st_prev=states)
        states = (hf, hb)
        oc, = _ssd(xbc, z, dt, ssd_consts, l, dims, latent=True, init=init, oc=oc)

        x1, h2, comb = _out_projection(oab, oc, xs, mod, l, n2, w_out16, wr, br, dims)
        xs = tuple(_moe(h2, comb, wg16, wu16, wd16, x1, mod, l, fg, dims, final=(l == depth - 1)))

    y_prompt = xs[0].reshape(batch, seq, D_MODEL)
    y_sample = xs[1].reshape(dec_batch, dec_seq, D_MODEL)
    kv_shape = (batch, depth, seq, A_KV, HEAD_DIM)
    st_shape = (batch, depth, SSM_HEADS, SSM_HEAD_DIM, SSM_STATE)
    return ((y_prompt, y_sample) + tuple(t.reshape(kv_shape) for t in kvp)
            + tuple(t.reshape(st_shape) for t in states))
```

```python
import functools

import jax
import jax.numpy as jnp
import numpy as np
from jax import lax
from jax.experimental import pallas as pl
from jax.experimental.pallas import tpu as pltpu

f32 = jnp.float32
bf16 = jnp.bfloat16
HIGHEST = lax.Precision.HIGHEST

D_MODEL = 1024
GRID_W = 64
HEAD_DIM = 64
A_HEADS = 4
A_KV = 2
WINDOW = 128
B_HEADS = 4
B_KV = 2
ROPE_THETA = 10000.0
ROPE_QUARTER = HEAD_DIM // 4
SSM_HEADS = 8
SSM_HEAD_DIM = 64
D_SSM = SSM_HEADS * SSM_HEAD_DIM
SSM_GROUPS = 2
SSM_STATE = 64
CHUNK = 128
XBC_DIM = D_SSM + 2 * SSM_GROUPS * SSM_STATE
D_AB = (A_HEADS + B_HEADS) * HEAD_DIM
D_Q = A_HEADS * HEAD_DIM
D_KV = A_KV * HEAD_DIM
N_EGROUPS = 4
EXPERTS_PER_GROUP = 4
N_EXPERTS = N_EGROUPS * EXPERTS_PER_GROUP
D_EXPERT = 256
EPS = 1e-6

LANES = 128
C_AQ, C_AK, C_AV, C_BQ, C_BK, C_BV, C_Z, C_XBC, C_DT = 0, 256, 384, 512, 768, 896, 1024, 1536, 2304

VMEM_LIMIT = 56 * 1024 * 1024
ANY = pl.BlockSpec(memory_space=pl.ANY)
SMEM = pl.BlockSpec(memory_space=pltpu.SMEM)


def _params(sem, vmem=VMEM_LIMIT):
    return pltpu.CompilerParams(dimension_semantics=sem, vmem_limit_bytes=vmem)


def _dot(a, b, **kw):
    return jnp.dot(a, b, preferred_element_type=f32, **kw)


def _dot_nt(a, b):
    return lax.dot_general(a, b, (((1,), (1,)), ((), ())), preferred_element_type=f32)


def _dot_tn(a, b):
    return lax.dot_general(a, b, (((0,), (0,)), ((), ())), preferred_element_type=f32)


def _silu(x):
    return x / (1.0 + jnp.exp(-x))


def _softplus(x):
    return jnp.maximum(x, 0.0) + jnp.log1p(jnp.exp(-jnp.abs(x)))


def _rms(x):
    return x * lax.rsqrt(jnp.mean(x * x, axis=-1, keepdims=True) + EPS)


class _Tiling:
    def __init__(self, n_p_tok, n_s_tok, dec_seq, tm):
        self.tm = tm
        self.n_p = n_p_tok // tm
        self.n_s = n_s_tok // tm
        self.n = self.n_p + self.n_s
        self.per_seq = dec_seq // tm

    def p_idx(self, i):
        return jnp.minimum(i, self.n_p - 1)

    def s_idx(self, i):
        return jnp.maximum(i - self.n_p, 0)

    def mod_row(self, i):
        return jnp.where(i < self.n_p, 0, 1 + (i - self.n_p) // self.per_seq)

    def seq_pos(self, i):
        return jnp.where(i < self.n_p, 0, (i - self.n_p) % self.per_seq)


def _x_specs(til, split):
    tm = til.tm
    if split:
        return [pl.BlockSpec((tm, D_MODEL), lambda i, *_: (til.p_idx(i), 0)),
                pl.BlockSpec((tm, D_MODEL), lambda i, *_: (til.s_idx(i), 0))]
    return [pl.BlockSpec((tm, D_MODEL), lambda i, *_: (i, 0))]


def _load_x(refs, i, n_p):
    if len(refs) == 2:
        return jnp.where(i < n_p, refs[0][...], refs[1][...])
    return refs[0][...]


def _mod_spec(til, layer, k):
    return pl.BlockSpec((None, None, None, 1, D_MODEL), lambda i, *_: (layer, til.mod_row(i), k, 0, 0))


def _mod_kernel(c_ref, w_ref, b_ref, o_ref):
    s = _silu(c_ref[...])
    o_ref[...] = _dot(s, w_ref[...], precision=HIGHEST) + b_ref[...]


def _modulation(cvec, w_ada, b_ada):
    depth = w_ada.shape[0]
    n = w_ada.shape[2]
    tn = 1536
    return pl.pallas_call(
        _mod_kernel,
        grid=(depth, n // tn),
        in_specs=[
            pl.BlockSpec((8, D_MODEL), lambda l, j: (0, 0)),
            pl.BlockSpec((None, D_MODEL, tn), lambda l, j: (l, 0, j)),
            pl.BlockSpec((None, 1, tn), lambda l, j: (l, 0, j)),
        ],
        out_specs=pl.BlockSpec((None, 8, tn), lambda l, j: (l, 0, j)),
        out_shape=jax.ShapeDtypeStruct((depth, 8, n), f32),
        compiler_params=_params(("arbitrary", "arbitrary")),
        name="adaln_mod",
    )(cvec, w_ada, b_ada.reshape(depth, 1, n))


def _rope(x, cos, sin_even, sin_odd):
    w = x.shape[-1]
    nxt = pltpu.roll(x, w - ROPE_QUARTER, 1)
    prv = pltpu.roll(x, ROPE_QUARTER, 1)
    return x * cos + nxt * sin_even + prv * sin_odd


def _inproj_kernel(*refs, n_x, n_alias, n_p, seqs_per_tile):
    x_refs = refs[:n_x]
    (sh_ref, sc_ref, g_ref, w_ref, wdt_ref, qg_ref, kg_ref, dtb_ref, hm_ref,
     cos_ref, se_ref, so_ref) = refs[n_x:n_x + 12]
    (qa_ref, qb_ref, akp_ref, avp_ref, bkp_ref, bvp_ref, aks_ref, avs_ref, bks_ref, bvs_ref,
     z_ref, xbc_ref, dt_ref) = refs[n_x + 12 + n_alias:]
    i = pl.program_id(0)
    h = _rms(_load_x(x_refs, i, n_p)) * g_ref[...]
    h = h * (1.0 + sc_ref[...]) + sh_ref[...]
    hb = h.astype(bf16)

    def proj(lo, hi):
        return _dot(hb, w_ref[:, lo:hi])

    def head_norm(t, gain):
        w = t.shape[-1]
        sq = t * t
        hi = sq.astype(bf16)
        lo = (sq - hi.astype(f32)).astype(bf16)
        ms_h = (_dot(hi, hm_ref[0:w, 0:w]) + _dot(lo, hm_ref[0:w, 0:w])) * (1.0 / HEAD_DIM)
        return t * lax.rsqrt(ms_h + EPS) * gain

    qa = proj(C_AQ, C_AK)
    ka = proj(C_AK, C_AV)
    va = proj(C_AV, C_BQ)
    qb = head_norm(proj(C_BQ, C_BK), qg_ref[...])
    kb = head_norm(proj(C_BK, C_BV), kg_ref[:, 0:D_KV])
    vb = proj(C_BV, C_Z)
    z_ref[...] = proj(C_Z, C_XBC).astype(bf16)
    xbc_ref[...] = proj(C_XBC, C_DT).astype(bf16)
    dt_ref[...] = _softplus(_dot(hb, wdt_ref[...]) + dtb_ref[...])

    lat = i >= n_p
    cos = jnp.where(lat, cos_ref[...], 1.0)
    se = jnp.where(lat, se_ref[...], 0.0)
    so = jnp.where(lat, so_ref[...], 0.0)
    qa_ref[...] = _rope(qa, cos, se, so).astype(bf16)
    qb_ref[...] = _rope(qb, cos, se, so).astype(bf16)
    aks_ref[...] = _rope(ka, cos[:, :D_KV], se[:, :D_KV], so[:, :D_KV]).astype(bf16)
    bks_ref[...] = _rope(kb, cos[:, :D_KV], se[:, :D_KV], so[:, :D_KV]).astype(bf16)
    avs_ref[...] = va.astype(bf16)
    bvs_ref[...] = vb.astype(bf16)

    @pl.when(i < n_p)
    def _():
        shp = (seqs_per_tile, -1, D_KV)
        akp_ref[...] = ka.reshape(shp)
        avp_ref[...] = va.reshape(shp)
        bkp_ref[...] = kb.reshape(shp)
        bvp_ref[...] = vb.reshape(shp)


def _in_projection(xs, mod, layer, norm_g, w_main, w_dt, qg, kg, dtb, headmat, rope, kv_prev, dims, tm=512):
    batch, seq, dec_batch, dec_seq, depth = dims
    n_p_tok, n_s_tok = batch * seq, dec_batch * dec_seq
    t = n_p_tok + n_s_tok
    til = _Tiling(n_p_tok, n_s_tok, dec_seq, tm)
    spt = tm // seq
    cos, se, so = rope
    n_alias = len(kv_prev)

    def c2(shape):
        return pl.BlockSpec(shape, lambda i: (0, 0))

    def lspec(shape):
        return pl.BlockSpec((None,) + shape, lambda i: (layer,) + (0,) * len(shape))

    rope_spec = pl.BlockSpec((tm, D_Q), lambda i: (til.seq_pos(i), 0))
    tok = lambda w: pl.BlockSpec((tm, w), lambda i: (i, 0))
    kvp = pl.BlockSpec((spt, None, seq, D_KV), lambda i: (til.p_idx(i), layer, 0, 0))
    kvs = pl.BlockSpec((tm, D_KV), lambda i: (til.s_idx(i), 0))
    kvp_shape = jax.ShapeDtypeStruct((batch, depth, seq, D_KV), f32)
    kvs_shape = jax.ShapeDtypeStruct((n_s_tok, D_KV), bf16)
    n_in = len(xs) + 12
    return pl.pallas_call(
        functools.partial(_inproj_kernel, n_x=len(xs), n_alias=n_alias, n_p=til.n_p, seqs_per_tile=spt),
        grid=(til.n,),
        in_specs=_x_specs(til, len(xs) == 2) + [
            _mod_spec(til, layer, 0), _mod_spec(til, layer, 1),
            lspec((1, D_MODEL)), lspec((D_MODEL, C_DT)), lspec((D_MODEL, LANES)),
            lspec((1, D_Q)), lspec((1, D_Q)), lspec((1, LANES)), c2((D_Q, D_Q)),
            rope_spec, rope_spec, rope_spec] + [ANY] * n_alias,
        out_specs=[tok(D_Q), tok(D_Q), kvp, kvp, kvp, kvp, kvs, kvs, kvs, kvs,
                   tok(D_SSM), tok(XBC_DIM), tok(LANES)],
        out_shape=[jax.ShapeDtypeStruct((t, D_Q), bf16), jax.ShapeDtypeStruct((t, D_Q), bf16),
                   kvp_shape, kvp_shape, kvp_shape, kvp_shape,
                   kvs_shape, kvs_shape, kvs_shape, kvs_shape,
                   jax.ShapeDtypeStruct((t, D_SSM), bf16), jax.ShapeDtypeStruct((t, XBC_DIM), bf16),
                   jax.ShapeDtypeStruct((t, LANES), f32)],
        input_output_aliases={n_in + k: 2 + k for k in range(n_alias)},
        compiler_params=_params(("arbitrary",)),
        name="norm_mod_inproj",
    )(*xs, mod, mod, norm_g, w_main, w_dt, qg, kg, dtb, headmat, cos, se, so, *kv_prev)


def _softmax_pv(scores, values, sink):
    m = scores[0].max(axis=-1, keepdims=True)
    for s in scores[1:]:
        m = jnp.maximum(m, s.max(axis=-1, keepdims=True))
    if sink is not None:
        m = jnp.maximum(m, sink)
    den = None
    acc = None
    for s, v in zip(scores, values):
        p = jnp.exp(s - m)
        d = p.sum(axis=-1, keepdims=True)
        o = _dot(p.astype(bf16), v)
        den = d if den is None else den + d
        acc = o if acc is None else acc + o
    if sink is not None:
        den = den + jnp.exp(sink - m)
    return acc / den


def _attn_ctx_kernel(sink_ref, qa_ref, ka_ref, va_ref, qb_ref, kb_ref, vb_ref, o_ref, *, layer):
    scale = HEAD_DIM ** -0.5
    for mixer, (q_ref, k_ref, v_ref) in enumerate(((qa_ref, ka_ref, va_ref), (qb_ref, kb_ref, vb_ref))):
        for kv in range(A_KV):
            ks = slice(kv * HEAD_DIM, (kv + 1) * HEAD_DIM)
            k = k_ref[:, ks].astype(bf16)
            v = v_ref[:, ks].astype(bf16)
            for g in range(A_HEADS // A_KV):
                hd = kv * 2 + g
                hs = slice(hd * HEAD_DIM, (hd + 1) * HEAD_DIM)
                q = q_ref[:, hs] * scale
                s = _dot_nt(q, k)
                sink = sink_ref[layer, hd] if mixer == 0 else None
                o = _softmax_pv([s], [v], sink)
                os_ = slice(mixer * D_Q + hd * HEAD_DIM, mixer * D_Q + (hd + 1) * HEAD_DIM)
                o_ref[:, os_] = o.astype(bf16)


def _attention_ctx(sink, qa, qb, kvp, layer, dims):
    batch, seq, dec_batch, dec_seq, depth = dims
    t = batch * seq + dec_batch * dec_seq
    qspec = pl.BlockSpec((seq, D_Q), lambda b: (b, 0))
    kspec = pl.BlockSpec((None, None, seq, D_KV), lambda b: (b, layer, 0, 0))
    akp, avp, bkp, bvp = kvp
    return pl.pallas_call(
        functools.partial(_attn_ctx_kernel, layer=layer),
        grid=(batch,),
        in_specs=[SMEM, qspec, kspec, kspec, qspec, kspec, kspec],
        out_specs=pl.BlockSpec((seq, D_AB), lambda b: (b, 0)),
        out_shape=jax.ShapeDtypeStruct((t, D_AB), bf16),
        compiler_params=_params(("arbitrary",)),
        name="attn_context",
    )(sink, qa, akp, avp, qb, bkp, bvp)


def _attn_lat_kernel(sink_ref, qa_ref, ka_ref, va_ref, cka_ref, cva_ref,
                     qb_ref, kb_ref, vb_ref, ckb_ref, cvb_ref, alias_ref, o_ref, *, seq, layer):
    del alias_ref
    j = pl.program_id(1)
    scale = HEAD_DIM ** -0.5
    w = WINDOW
    start = pl.multiple_of(jnp.clip((j - 1) * w, 0, seq - 3 * w), w)
    qi = j * w + lax.broadcasted_iota(jnp.int32, (w, 3 * w), 0)
    ki = start + lax.broadcasted_iota(jnp.int32, (w, 3 * w), 1)
    valid = jnp.abs(ki - qi) <= w
    for kv in range(A_KV):
        ks = slice(kv * HEAD_DIM, (kv + 1) * HEAD_DIM)
        ka = ka_ref[pl.ds(start, 3 * w), ks]
        va = va_ref[pl.ds(start, 3 * w), ks]
        cka = cka_ref[:, ks].astype(bf16)
        cva = cva_ref[:, ks].astype(bf16)
        kb = kb_ref[:, ks]
        vb = vb_ref[:, ks]
        ckb = ckb_ref[:, ks].astype(bf16)
        cvb = cvb_ref[:, ks].astype(bf16)
        for g in range(A_HEADS // A_KV):
            hd = kv * 2 + g
            hs = slice(hd * HEAD_DIM, (hd + 1) * HEAD_DIM)
            q = qa_ref[:, hs] * scale
            s_loc = jnp.where(valid, _dot_nt(q, ka), -jnp.inf)
            s_ctx = _dot_nt(q, cka)
            o_ref[:, hs] = _softmax_pv([s_loc, s_ctx], [va, cva], sink_ref[layer, hd]).astype(bf16)
            q = qb_ref[:, hs] * scale
            o = _softmax_pv([_dot_nt(q, kb), _dot_nt(q, ckb)], [vb, cvb], None)
            o_ref[:, slice(D_Q + hd * HEAD_DIM, D_Q + (hd + 1) * HEAD_DIM)] = o.astype(bf16)


def _attention_lat(sink, qa, qb, kvs, caches, oab, layer, dims):
    batch, seq, dec_batch, dec_seq, depth = dims
    w = WINDOW
    nq = dec_seq // w
    q0 = batch * seq // w
    past = caches[0].shape[2]
    qspec = pl.BlockSpec((w, D_Q), lambda b, j: (q0 + b * nq + j, 0))
    kspec = pl.BlockSpec((dec_seq, D_KV), lambda b, j: (b, 0))
    cspec = pl.BlockSpec((None, None, past, D_KV), lambda b, j: (b, layer, 0, 0))
    aks, avs, bks, bvs = kvs
    cka, cva, ckb, cvb = caches
    return pl.pallas_call(
        functools.partial(_attn_lat_kernel, seq=dec_seq, layer=layer),
        grid=(dec_batch, nq),
        in_specs=[SMEM, qspec, kspec, kspec, cspec, cspec, qspec, kspec, kspec, cspec, cspec, ANY],
        out_specs=pl.BlockSpec((w, D_AB), lambda b, j: (q0 + b * nq + j, 0)),
        out_shape=jax.ShapeDtypeStruct(oab.shape, oab.dtype),
        input_output_aliases={11: 0},
        compiler_params=_params(("arbitrary", "arbitrary")),
        name="attn_latent",
    )(sink, qa, aks, avs, cka, cva, qb, bks, bvs, ckb, cvb, oab)


def _ssd_kernel(*refs, latent, n_alias):
    xbc_ref, z_ref, dt_ref, cw_ref, cb_ref, alog_ref, dskip_ref, g_ref = refs[:8]
    if latent:
        h0f_ref, h0b_ref = refs[8:10]
        o_ref, xc_scr, y_scr, hf_ref, hb_ref = refs[10 + n_alias:]
    else:
        o_ref, hf_ref, hb_ref, xc_scr, y_scr = refs[8 + n_alias:]
    n = xbc_ref.shape[0]
    nc = n // CHUNK
    L = CHUNK
    ns = SSM_STATE
    gw = D_SSM // SSM_GROUPS
    hpg = SSM_HEADS // SSM_GROUPS
    hd_w = SSM_HEAD_DIM

    x = xbc_ref[...].astype(f32)
    t_idx = lax.broadcasted_iota(jnp.int32, x.shape, 0)
    prv = jnp.where(t_idx == 0, 0.0, pltpu.roll(x, 1, 0))
    nxt = jnp.where(t_idx == n - 1, 0.0, pltpu.roll(x, n - 1, 0))
    y = prv * cw_ref[0:1, :] + x * cw_ref[1:2, :] + nxt * cw_ref[2:3, :] + cb_ref[...]
    xc_scr[...] = _silu(y)

    if latent:
        hf_ref[...] = h0f_ref[...]
        hb_ref[...] = h0b_ref[...]
    else:
        hf_ref[...] = jnp.zeros(hf_ref.shape, f32)
        hb_ref[...] = jnp.zeros(hb_ref.shape, f32)

    a_neg = -jnp.exp(alog_ref[...])
    r_i = lax.broadcasted_iota(jnp.int32, (L, L), 0)
    c_i = lax.broadcasted_iota(jnp.int32, (L, L), 1)
    lower = r_i >= c_i
    upper = r_i <= c_i
    tril = lower.astype(f32)
    triu = upper.astype(f32)

    def chunk_vals(c):
        rows = pl.ds(pl.multiple_of(c * L, L), L)
        xs = xc_scr[rows, 0:D_SSM]
        bm = xc_scr[rows, D_SSM:D_SSM + SSM_GROUPS * ns].astype(bf16)
        cm = xc_scr[rows, D_SSM + SSM_GROUPS * ns:XBC_DIM].astype(bf16)
        dt = dt_ref[rows, :]
        return rows, xs, bm, cm, dt

    def state_pass(rows, xs, bm, cm, dt, cum, cum_t, edge, lane0, h_ref):
        e_in = jnp.exp(cum)
        wts = jnp.exp(cum[edge:edge + 1, :] - cum) * dt
        dec = jnp.exp(cum_t[:, edge:edge + 1])
        h16 = h_ref[...].astype(bf16)
        for g in range(SSM_GROUPS):
            yo = _dot_nt(cm[:, g * ns:(g + 1) * ns], h16[g * gw:(g + 1) * gw, :])
            xw = []
            for hh in range(hpg):
                hd = g * hpg + hh
                ln = lane0 + hd
                cols = slice(hd * hd_w, (hd + 1) * hd_w)
                y_scr[rows, cols] += yo[:, hh * hd_w:(hh + 1) * hd_w] * e_in[:, ln:ln + 1]
                xw.append(xs[:, cols] * wts[:, ln:ln + 1])
            st = _dot_tn(jnp.concatenate(xw, axis=1).astype(bf16), bm[:, g * ns:(g + 1) * ns])
            for hh in range(hpg):
                hd = g * hpg + hh
                ln = lane0 + hd
                hrows = slice(hd * hd_w, (hd + 1) * hd_w)
                h_ref[hrows, :] = h_ref[hrows, :] * dec[ln:ln + 1, :] + st[hh * hd_w:(hh + 1) * hd_w, :]

    def fwd_body(c, carry):
        rows, xs, bm, cm, dt = chunk_vals(c)
        da = dt * a_neg
        cs = _dot(tril, da, precision=HIGHEST)
        suf = _dot(triu, da, precision=HIGHEST)
        cs_t, suf_t, dt_t = cs.T, suf.T, dt.T
        xb16 = xs.astype(bf16)
        for g in range(SSM_GROUPS):
            cb = _dot_nt(cm[:, g * ns:(g + 1) * ns], bm[:, g * ns:(g + 1) * ns])
            for hh in range(hpg):
                hd = g * hpg + hh
                hb_ = SSM_HEADS + hd
                lf = jnp.exp(jnp.where(lower, cs[:, hd:hd + 1] - cs_t[hd:hd + 1, :], -jnp.inf))
                lb = jnp.exp(jnp.where(upper, suf[:, hb_:hb_ + 1] - suf_t[hb_:hb_ + 1, :], -jnp.inf))
                m = cb * (lf * dt_t[hd:hd + 1, :] + lb * dt_t[hb_:hb_ + 1, :])
                cols = slice(hd * hd_w, (hd + 1) * hd_w)
                y_scr[rows, cols] = _dot(m.astype(bf16), xb16[:, cols])
        state_pass(rows, xs, bm, cm, dt, cs, cs_t, L - 1, 0, hf_ref)
        return carry

    def bwd_body(k, carry):
        rows, xs, bm, cm, dt = chunk_vals(nc - 1 - k)
        suf = _dot(triu, dt * a_neg, precision=HIGHEST)
        state_pass(rows, xs, bm, cm, dt, suf, suf.T, 0, SSM_HEADS, hb_ref)
        return carry

    lax.fori_loop(0, nc, fwd_body, 0)
    lax.fori_loop(0, nc, bwd_body, 0)

    yv = y_scr[...] + xc_scr[:, 0:D_SSM] * dskip_ref[...]
    yv = yv * _silu(z_ref[...].astype(f32))
    o_ref[...] = (_rms(yv) * g_ref[...]).astype(bf16)


def _ssd(xbc, z, dt, consts, layer, dims, *, latent, init=None, oc=None, st_prev=()):
    batch, seq, dec_batch, dec_seq, depth = dims
    t = batch * seq + dec_batch * dec_seq
    if latent:
        n_b, n, b0 = dec_batch, dec_seq, batch * seq // dec_seq
    else:
        n_b, n, b0 = batch, seq, 0

    def tok(width):
        return pl.BlockSpec((n, width), lambda b: (b0 + b, 0))

    def lspec(shape):
        return pl.BlockSpec((None,) + shape, lambda b: (layer,) + (0,) * len(shape))

    st = pl.BlockSpec((None, None, D_SSM, SSM_STATE), lambda b: (b, layer, 0, 0))
    in_specs = [tok(XBC_DIM), tok(D_SSM), tok(LANES),
                lspec((3, XBC_DIM)), lspec((1, XBC_DIM)), lspec((1, LANES)), lspec((1, D_SSM)),
                lspec((1, D_SSM))]
    args = [xbc, z, dt, *consts]
    oc_shape = jax.ShapeDtypeStruct((t, D_SSM), bf16)
    oc_spec = pl.BlockSpec((n, D_SSM), lambda b: (b0 + b, 0))
    scratch = [pltpu.VMEM((n, XBC_DIM), f32), pltpu.VMEM((n, D_SSM), f32)]
    if latent:
        in_specs += [st, st, ANY]
        args += [init[0], init[1], oc]
        out_specs, out_shape = [oc_spec], [oc_shape]
        aliases = {10: 0}
        n_alias = 1
        scratch += [pltpu.VMEM((D_SSM, SSM_STATE), f32), pltpu.VMEM((D_SSM, SSM_STATE), f32)]
    else:
        n_alias = len(st_prev)
        in_specs += [ANY] * n_alias
        args += list(st_prev)
        st_shape = jax.ShapeDtypeStruct((batch, depth, D_SSM, SSM_STATE), f32)
        out_specs, out_shape = [oc_spec, st, st], [oc_shape, st_shape, st_shape]
        aliases = {8 + k: 1 + k for k in range(n_alias)}
    return pl.pallas_call(
        functools.partial(_ssd_kernel, latent=latent, n_alias=n_alias),
        grid=(n_b,),
        in_specs=in_specs, out_specs=out_specs, out_shape=out_shape,
        scratch_shapes=scratch,
        input_output_aliases=aliases,
        compiler_params=_params(("arbitrary",)),
        name="ssd_latent" if latent else "ssd_context",
    )(*args)


def _outproj_kernel(*refs, n_x, n_p):
    x_refs = refs[:n_x]
    (oab_ref, oc_ref, g1_ref, sh_ref, sc_ref, n2_ref, w_ref, wr_ref, br_ref,
     x1_ref, h2_ref, comb_ref) = refs[n_x:]
    i = pl.program_id(0)
    o = _dot(oab_ref[...], w_ref[0:D_AB, :]) + _dot(oc_ref[...], w_ref[D_AB:, :])
    x1 = _load_x(x_refs, i, n_p) + g1_ref[...] * o
    x1_ref[...] = x1
    h2 = _rms(x1) * n2_ref[...]
    h2 = h2 * (1.0 + sc_ref[...]) + sh_ref[...]
    h2b = h2.astype(bf16)
    h2_ref[...] = h2b

    logits = _dot(h2b, wr_ref[...]) + br_ref[...]
    lane = lax.broadcasted_iota(jnp.int32, logits.shape, 1).astype(f32)
    big = float(LANES)
    neg = -jnp.inf
    gmask = (lane >= N_EXPERTS) & (lane < N_EXPERTS + N_EGROUPS)
    gl = jnp.where(gmask, logits, neg)
    gmax = gl.max(axis=-1, keepdims=True)
    gsel = jnp.where(gl == gmax, lane, big).min(axis=-1, keepdims=True) - N_EXPERTS
    gprob = 1.0 / jnp.where(gmask, jnp.exp(logits - gmax), 0.0).sum(axis=-1, keepdims=True)
    emask = (lane >= gsel * EXPERTS_PER_GROUP) & (lane < (gsel + 1) * EXPERTS_PER_GROUP)
    el = jnp.where(emask, logits, neg)
    v1 = el.max(axis=-1, keepdims=True)
    i1 = jnp.where(el == v1, lane, big).min(axis=-1, keepdims=True)
    el2 = jnp.where(lane == i1, neg, el)
    v2 = el2.max(axis=-1, keepdims=True)
    i2 = jnp.where(el2 == v2, lane, big).min(axis=-1, keepdims=True)
    e2 = jnp.exp(v2 - v1)
    den = 1.0 + e2
    comb_ref[...] = (jnp.where(lane == i1, gprob / den, 0.0)
                     + jnp.where(lane == i2, gprob * e2 / den, 0.0))


def _out_projection(oab, oc, xs, mod, layer, norm_g, w_out, wr, br, dims, tm=512):
    batch, seq, dec_batch, dec_seq, depth = dims
    n_p_tok, n_s_tok = batch * seq, dec_batch * dec_seq
    t = n_p_tok + n_s_tok
    til = _Tiling(n_p_tok, n_s_tok, dec_seq, tm)

    def lspec(shape):
        return pl.BlockSpec((None,) + shape, lambda i: (layer,) + (0,) * len(shape))

    tok = lambda w: pl.BlockSpec((tm, w), lambda i: (i, 0))
    return pl.pallas_call(
        functools.partial(_outproj_kernel, n_x=len(xs), n_p=til.n_p),
        grid=(til.n,),
        in_specs=_x_specs(til, len(xs) == 2) + [
            tok(D_AB), tok(D_SSM),
            _mod_spec(til, layer, 2), _mod_spec(til, layer, 3), _mod_spec(til, layer, 4),
            lspec((1, D_MODEL)), lspec((D_AB + D_SSM, D_MODEL)), lspec((D_MODEL, LANES)), lspec((1, LANES))],
        out_specs=[tok(D_MODEL), tok(D_MODEL), tok(LANES)],
        out_shape=[jax.ShapeDtypeStruct((t, D_MODEL), f32),
                   jax.ShapeDtypeStruct((t, D_MODEL), bf16),
                   jax.ShapeDtypeStruct((t, LANES), f32)],
        compiler_params=_params(("arbitrary",)),
        name="outproj_norm_router",
    )(*xs, oab, oc, mod, mod, mod, norm_g, w_out, wr, br)


def _moe_kernel(h2_ref, comb_ref, wg_ref, wu_ref, wd_ref, x1_ref, g2_ref, fg_ref, *outs, final, n_p):
    i = pl.program_id(0)
    h = h2_ref[...]
    comb = comb_ref[...]
    hid = []
    for e in range(N_EXPERTS):
        a = _dot(h, wg_ref[e])
        u = _dot(h, wu_ref[e])
        hid.append((_silu(a) * u * comb[:, e:e + 1]).astype(bf16))
    y = _dot(jnp.concatenate(hid, axis=1), wd_ref[...])
    x2 = x1_ref[...] + g2_ref[...] * y
    if final:
        yp_ref, ys_ref = outs
        x2 = _rms(x2) * fg_ref[...]

        @pl.when(i < n_p)
        def _():
            yp_ref[...] = x2

        @pl.when(i >= n_p)
        def _():
            ys_ref[...] = x2
    else:
        outs[0][...] = x2


def _moe(h2, comb, wg, wu, wd, x1, mod, layer, final_g, dims, final, tm=512):
    batch, seq, dec_batch, dec_seq, depth = dims
    n_p_tok, n_s_tok = batch * seq, dec_batch * dec_seq
    t = n_p_tok + n_s_tok
    til = _Tiling(n_p_tok, n_s_tok, dec_seq, tm)
    once = pl.Buffered(1)
    tok = lambda w: pl.BlockSpec((tm, w), lambda i: (i, 0))
    if final:
        out_specs = [pl.BlockSpec((tm, D_MODEL), lambda i: (til.p_idx(i), 0)),
                     pl.BlockSpec((tm, D_MODEL), lambda i: (til.s_idx(i), 0))]
        out_shape = [jax.ShapeDtypeStruct((n_p_tok, D_MODEL), f32),
                     jax.ShapeDtypeStruct((n_s_tok, D_MODEL), f32)]
    else:
        out_specs = [tok(D_MODEL)]
        out_shape = [jax.ShapeDtypeStruct((t, D_MODEL), f32)]
    return pl.pallas_call(
        functools.partial(_moe_kernel, final=final, n_p=til.n_p),
        grid=(til.n,),
        in_specs=[tok(D_MODEL), tok(LANES),
                  pl.BlockSpec((None, N_EXPERTS, D_MODEL, D_EXPERT), lambda i: (layer, 0, 0, 0), once),
                  pl.BlockSpec((None, N_EXPERTS, D_MODEL, D_EXPERT), lambda i: (layer, 0, 0, 0), once),
                  pl.BlockSpec((None, N_EXPERTS * D_EXPERT, D_MODEL), lambda i: (layer, 0, 0), once),
                  tok(D_MODEL), _mod_spec(til, layer, 5),
                  pl.BlockSpec((1, D_MODEL), lambda i: (0, 0))],
        out_specs=out_specs, out_shape=out_shape,
        compiler_params=_params(("arbitrary",)),
        name="moe_ffn",
    )(h2, comb, wg, wu, wd.reshape(depth, N_EXPERTS * D_EXPERT, D_MODEL), x1, mod, final_g)


def _rope_tables(n_rows):
    rows = jnp.repeat(jnp.arange(n_rows), GRID_W).astype(f32)
    cols = jnp.tile(jnp.arange(GRID_W), n_rows).astype(f32)
    inv = ROPE_THETA ** (-jnp.arange(ROPE_QUARTER, dtype=f32) / ROPE_QUARTER)
    ang_r = rows[:, None] * inv
    ang_c = cols[:, None] * inv
    ang = jnp.concatenate([ang_r, ang_r, ang_c, ang_c], axis=-1)
    cos, sin = jnp.cos(ang), jnp.sin(ang)
    even = (np.arange(HEAD_DIM) // ROPE_QUARTER) % 2 == 0
    sin_even = jnp.where(even, -sin, 0.0)
    sin_odd = jnp.where(even, 0.0, sin)
    return tuple(jnp.tile(t, (1, D_Q // HEAD_DIM)) for t in (cos, sin_even, sin_odd))


def _head_sum_matrix():
    m = np.zeros((D_Q, D_Q), np.float32)
    for h in range(D_Q // HEAD_DIM):
        m[h * HEAD_DIM:(h + 1) * HEAD_DIM, h * HEAD_DIM:(h + 1) * HEAD_DIM] = 1.0
    return jnp.asarray(m, dtype=bf16)


def _pad_last(v, width=LANES):
    pad = [(0, 0)] * (v.ndim - 1) + [(0, width - v.shape[-1])]
    return jnp.pad(v, pad)


@jax.jit
def kernel(x_prompt, x_sample, cache_a_k, cache_a_v, cache_b_k, cache_b_v, state_ssm_fwd, state_ssm_bwd, c, c_ctx, norm1_g, norm2_g, final_norm_g, w_ada, b_ada, w_in, a_sink, q_norm_g, k_norm_g, conv_w, conv_b, dt_bias, a_log, d_skip, ssm_norm_g, w_out, w_router_group, b_router_group, w_router_expert, b_router_expert, w_gate, w_up, w_down):
    batch, seq, _ = x_prompt.shape
    dec_batch, dec_seq, _ = x_sample.shape
    depth = w_in.shape[0]
    past = cache_a_k.shape[2]
    dims = (batch, seq, dec_batch, dec_seq, depth)
    n_p_tok = batch * seq
    n_s_tok = dec_batch * dec_seq

    cvec = jnp.concatenate([c_ctx[None, :], c, jnp.zeros((8 - 1 - dec_batch, D_MODEL), f32)], axis=0)
    mod = _modulation(cvec, w_ada, b_ada).reshape(depth, 8, 6, 1, D_MODEL)

    rope = _rope_tables(dec_seq // GRID_W)
    headmat = _head_sum_matrix()
    w_main = w_in[:, :, :C_DT].astype(bf16)
    w_dt = _pad_last(w_in[:, :, C_DT:]).astype(bf16)
    qg = jnp.tile(q_norm_g, (1, D_Q // HEAD_DIM)).reshape(depth, 1, D_Q)
    kg = jnp.tile(k_norm_g, (1, D_Q // HEAD_DIM)).reshape(depth, 1, D_Q)
    dtb = _pad_last(dt_bias.reshape(depth, 1, 2 * SSM_HEADS))
    n1 = norm1_g.reshape(depth, 1, D_MODEL)
    n2 = norm2_g.reshape(depth, 1, D_MODEL)
    ssd_consts = (conv_w, conv_b.reshape(depth, 1, XBC_DIM),
                  _pad_last(a_log.reshape(depth, 1, 2 * SSM_HEADS)),
                  jnp.repeat(d_skip, SSM_HEAD_DIM, axis=-1).reshape(depth, 1, D_SSM),
                  ssm_norm_g.reshape(depth, 1, D_SSM))
    w_out16 = w_out.astype(bf16)
    wr = _pad_last(jnp.concatenate([w_router_expert, w_router_group], axis=-1)).astype(bf16)
    br = _pad_last(jnp.concatenate([b_router_expert, b_router_group], axis=-1)).reshape(depth, 1, LANES)
    wg16, wu16, wd16 = w_gate.astype(bf16), w_up.astype(bf16), w_down.astype(bf16)
    fg = final_norm_g.reshape(1, D_MODEL)

    caches = tuple(t.reshape(dec_batch, depth, past, D_KV) for t in (cache_a_k, cache_a_v, cache_b_k, cache_b_v))
    init = (state_ssm_fwd.reshape(dec_batch, depth, D_SSM, SSM_STATE),
            state_ssm_bwd.reshape(dec_batch, depth, D_SSM, SSM_STATE))

    xs = (x_prompt.reshape(n_p_tok, D_MODEL), x_sample.reshape(n_s_tok, D_MODEL))
    kvp, states = (), ()
    for l in range(depth):
        outs = _in_projection(xs, mod, l, n1, w_main, w_dt, qg, kg, dtb, headmat, rope, kvp, dims)
        qa, qb = outs[0:2]
        kvp, kvs = tuple(outs[2:6]), tuple(outs[6:10])
        z, xbc, dt = outs[10:13]

        oab = _attention_ctx(a_sink, qa, qb, kvp, l, dims)
        oab = _attention_lat(a_sink, qa, qb, kvs, caches, oab, l, dims)

        oc, hf, hb = _ssd(xbc, z, dt, ssd_consts, l, dims, latent=False, st_prev=states)
        states = (hf, hb)
        oc, = _ssd(xbc, z, dt, ssd_consts, l, dims, latent=True, init=init, oc=oc)

        x1, h2, comb = _out_projection(oab, oc, xs, mod, l, n2, w_out16, wr, br, dims)
        xs = tuple(_moe(h2, comb, wg16, wu16, wd16, x1, mod, l, fg, dims, final=(l == depth - 1)))

    y_prompt = xs[0].reshape(batch, seq, D_MODEL)
    y_sample = xs[1].reshape(dec_batch, dec_seq, D_MODEL)
    kv_shape = (batch, depth, seq, A_KV, HEAD_DIM)
    st_shape = (batch, depth, SSM_HEADS, SSM_HEAD_DIM, SSM_STATE)
    return ((y_prompt, y_sample) + tuple(t.reshape(kv_shape) for t in kvp)
            + tuple(t.reshape(st_shape) for t in states))
```

```python
import functools

import jax
import jax.numpy as jnp
import numpy as np
from jax import lax
from jax.experimental import pallas as pl
from jax.experimental.pallas import tpu as pltpu

f32 = jnp.float32
bf16 = jnp.bfloat16
HIGHEST = lax.Precision.HIGHEST

D_MODEL = 1024
GRID_W = 64
HEAD_DIM = 64
A_HEADS = 4
A_KV = 2
WINDOW = 128
B_HEADS = 4
B_KV = 2
ROPE_THETA = 10000.0
ROPE_QUARTER = HEAD_DIM // 4
SSM_HEADS = 8
SSM_HEAD_DIM = 64
D_SSM = SSM_HEADS * SSM_HEAD_DIM
SSM_GROUPS = 2
SSM_STATE = 64
CHUNK = 128
XBC_DIM = D_SSM + 2 * SSM_GROUPS * SSM_STATE
D_AB = (A_HEADS + B_HEADS) * HEAD_DIM
D_Q = A_HEADS * HEAD_DIM
D_KV = A_KV * HEAD_DIM
N_EGROUPS = 4
EXPERTS_PER_GROUP = 4
N_EXPERTS = N_EGROUPS * EXPERTS_PER_GROUP
D_EXPERT = 256
EPS = 1e-6

LANES = 128
C_AQ, C_AK, C_AV, C_BQ, C_BK, C_BV, C_Z, C_XBC, C_DT = 0, 256, 384, 512, 768, 896, 1024, 1536, 2304

SSD_CTX_SEQS = 4

VMEM_LIMIT = 56 * 1024 * 1024
ANY = pl.BlockSpec(memory_space=pl.ANY)
SMEM = pl.BlockSpec(memory_space=pltpu.SMEM)


def _params(sem, vmem=VMEM_LIMIT):
    return pltpu.CompilerParams(dimension_semantics=sem, vmem_limit_bytes=vmem)


def _dot(a, b, **kw):
    return jnp.dot(a, b, preferred_element_type=f32, **kw)


def _dot_nt(a, b):
    return lax.dot_general(a, b, (((1,), (1,)), ((), ())), preferred_element_type=f32)


def _dot_tn(a, b):
    return lax.dot_general(a, b, (((0,), (0,)), ((), ())), preferred_element_type=f32)


def _silu(x):
    return (0.5 * x) * (1.0 + jnp.tanh(0.5 * x))


def _softplus(x):
    return jnp.maximum(x, 0.0) + jnp.log1p(jnp.exp(-jnp.abs(x)))


def _rms(x):
    return x * lax.rsqrt(jnp.mean(x * x, axis=-1, keepdims=True) + EPS)


class _Tiling:
    def __init__(self, n_p_tok, n_s_tok, dec_seq, tm):
        self.tm = tm
        self.n_p = n_p_tok // tm
        self.n_s = n_s_tok // tm
        self.n = self.n_p + self.n_s
        self.per_seq = dec_seq // tm

    def p_idx(self, i):
        return jnp.minimum(i, self.n_p - 1)

    def s_idx(self, i):
        return jnp.maximum(i - self.n_p, 0)

    def mod_row(self, i):
        return jnp.where(i < self.n_p, 0, 1 + (i - self.n_p) // self.per_seq)

    def seq_pos(self, i):
        return jnp.where(i < self.n_p, 0, (i - self.n_p) % self.per_seq)


def _x_specs(til, split):
    tm = til.tm
    if split:
        return [pl.BlockSpec((tm, D_MODEL), lambda i, *_: (til.p_idx(i), 0)),
                pl.BlockSpec((tm, D_MODEL), lambda i, *_: (til.s_idx(i), 0))]
    return [pl.BlockSpec((tm, D_MODEL), lambda i, *_: (i, 0))]


def _load_x(refs, i, n_p):
    if len(refs) == 2:
        return jnp.where(i < n_p, refs[0][...], refs[1][...])
    return refs[0][...]


def _mod_spec(til, layer, k):
    return pl.BlockSpec((None, None, None, 1, D_MODEL), lambda i, *_: (layer, til.mod_row(i), k, 0, 0))


def _mod_kernel(c_ref, w_ref, b_ref, o_ref):
    s = _silu(c_ref[...])
    o_ref[...] = _dot(s, w_ref[...], precision=HIGHEST) + b_ref[...]


def _modulation(cvec, w_ada, b_ada):
    depth = w_ada.shape[0]
    n = w_ada.shape[2]
    tn = 1536
    return pl.pallas_call(
        _mod_kernel,
        grid=(depth, n // tn),
        in_specs=[
            pl.BlockSpec((8, D_MODEL), lambda l, j: (0, 0)),
            pl.BlockSpec((None, D_MODEL, tn), lambda l, j: (l, 0, j)),
            pl.BlockSpec((None, 1, tn), lambda l, j: (l, 0, j)),
        ],
        out_specs=pl.BlockSpec((None, 8, tn), lambda l, j: (l, 0, j)),
        out_shape=jax.ShapeDtypeStruct((depth, 8, n), f32),
        compiler_params=_params(("arbitrary", "arbitrary")),
        name="adaln_mod",
    )(cvec, w_ada, b_ada.reshape(depth, 1, n))


def _rope(x, cos, sin_even, sin_odd):
    w = x.shape[-1]
    nxt = pltpu.roll(x, w - ROPE_QUARTER, 1)
    prv = pltpu.roll(x, ROPE_QUARTER, 1)
    return x * cos + nxt * sin_even + prv * sin_odd


def _inproj_kernel(*refs, n_x, n_alias, n_p, seqs_per_tile):
    x_refs = refs[:n_x]
    (sh_ref, sc_ref, g_ref, w_ref, wdt_ref, qg_ref, kg_ref, dtb_ref, hm_ref,
     cos_ref, se_ref, so_ref) = refs[n_x:n_x + 12]
    (qa_ref, qb_ref, akp_ref, avp_ref, bkp_ref, bvp_ref, aks_ref, avs_ref, bks_ref, bvs_ref,
     z_ref, xbc_ref, dt_ref) = refs[n_x + 12 + n_alias:]
    i = pl.program_id(0)
    h = _rms(_load_x(x_refs, i, n_p)) * g_ref[...]
    h = h * (1.0 + sc_ref[...]) + sh_ref[...]
    hb = h.astype(bf16)

    def proj(lo, hi):
        return _dot(hb, w_ref[:, lo:hi])

    def head_norm(t, gain):
        w = t.shape[-1]
        sq = t * t
        hi = sq.astype(bf16)
        lo = (sq - hi.astype(f32)).astype(bf16)
        ms_h = (_dot(hi, hm_ref[0:w, 0:w]) + _dot(lo, hm_ref[0:w, 0:w])) * (1.0 / HEAD_DIM)
        return t * lax.rsqrt(ms_h + EPS) * gain

    qa = proj(C_AQ, C_AK)
    ka = proj(C_AK, C_AV)
    va = proj(C_AV, C_BQ)
    qb = head_norm(proj(C_BQ, C_BK), qg_ref[...])
    kb = head_norm(proj(C_BK, C_BV), kg_ref[:, 0:D_KV])
    vb = proj(C_BV, C_Z)
    z_ref[...] = proj(C_Z, C_XBC).astype(bf16)
    xbc_ref[...] = proj(C_XBC, C_DT).astype(bf16)
    dt_ref[...] = _softplus(_dot(hb, wdt_ref[...]) + dtb_ref[...])

    lat = i >= n_p
    cos = jnp.where(lat, cos_ref[...], 1.0)
    se = jnp.where(lat, se_ref[...], 0.0)
    so = jnp.where(lat, so_ref[...], 0.0)
    qa_ref[...] = _rope(qa, cos, se, so).astype(bf16)
    qb_ref[...] = _rope(qb, cos, se, so).astype(bf16)
    aks_ref[...] = _rope(ka, cos[:, :D_KV], se[:, :D_KV], so[:, :D_KV]).astype(bf16)
    bks_ref[...] = _rope(kb, cos[:, :D_KV], se[:, :D_KV], so[:, :D_KV]).astype(bf16)
    avs_ref[...] = va.astype(bf16)
    bvs_ref[...] = vb.astype(bf16)

    @pl.when(i < n_p)
    def _():
        shp = (seqs_per_tile, -1, D_KV)
        akp_ref[...] = ka.reshape(shp)
        avp_ref[...] = va.reshape(shp)
        bkp_ref[...] = kb.reshape(shp)
        bvp_ref[...] = vb.reshape(shp)


def _in_projection(xs, mod, layer, norm_g, w_main, w_dt, qg, kg, dtb, headmat, rope, kv_prev, dims, tm=512):
    batch, seq, dec_batch, dec_seq, depth = dims
    n_p_tok, n_s_tok = batch * seq, dec_batch * dec_seq
    t = n_p_tok + n_s_tok
    til = _Tiling(n_p_tok, n_s_tok, dec_seq, tm)
    spt = tm // seq
    cos, se, so = rope
    n_alias = len(kv_prev)

    def c2(shape):
        return pl.BlockSpec(shape, lambda i: (0, 0))

    def lspec(shape):
        return pl.BlockSpec((None,) + shape, lambda i: (layer,) + (0,) * len(shape))

    rope_spec = pl.BlockSpec((tm, D_Q), lambda i: (til.seq_pos(i), 0))
    tok = lambda w: pl.BlockSpec((tm, w), lambda i: (i, 0))
    kvp = pl.BlockSpec((spt, None, seq, D_KV), lambda i: (til.p_idx(i), layer, 0, 0))
    kvs = pl.BlockSpec((tm, D_KV), lambda i: (til.s_idx(i), 0))
    kvp_shape = jax.ShapeDtypeStruct((batch, depth, seq, D_KV), f32)
    kvs_shape = jax.ShapeDtypeStruct((n_s_tok, D_KV), bf16)
    n_in = len(xs) + 12
    return pl.pallas_call(
        functools.partial(_inproj_kernel, n_x=len(xs), n_alias=n_alias, n_p=til.n_p, seqs_per_tile=spt),
        grid=(til.n,),
        in_specs=_x_specs(til, len(xs) == 2) + [
            _mod_spec(til, layer, 0), _mod_spec(til, layer, 1),
            lspec((1, D_MODEL)), lspec((D_MODEL, C_DT)), lspec((D_MODEL, LANES)),
            lspec((1, D_Q)), lspec((1, D_Q)), lspec((1, LANES)), c2((D_Q, D_Q)),
            rope_spec, rope_spec, rope_spec] + [ANY] * n_alias,
        out_specs=[tok(D_Q), tok(D_Q), kvp, kvp, kvp, kvp, kvs, kvs, kvs, kvs,
                   tok(D_SSM), tok(XBC_DIM), tok(LANES)],
        out_shape=[jax.ShapeDtypeStruct((t, D_Q), bf16), jax.ShapeDtypeStruct((t, D_Q), bf16),
                   kvp_shape, kvp_shape, kvp_shape, kvp_shape,
                   kvs_shape, kvs_shape, kvs_shape, kvs_shape,
                   jax.ShapeDtypeStruct((t, D_SSM), bf16), jax.ShapeDtypeStruct((t, XBC_DIM), bf16),
                   jax.ShapeDtypeStruct((t, LANES), f32)],
        input_output_aliases={n_in + k: 2 + k for k in range(n_alias)},
        compiler_params=_params(("arbitrary",)),
        name="norm_mod_inproj",
    )(*xs, mod, mod, norm_g, w_main, w_dt, qg, kg, dtb, headmat, cos, se, so, *kv_prev)


def _softmax_pv(scores, values, sink):
    m = scores[0].max(axis=-1, keepdims=True)
    for s in scores[1:]:
        m = jnp.maximum(m, s.max(axis=-1, keepdims=True))
    if sink is not None:
        m = jnp.maximum(m, sink)
    den = None
    acc = None
    for s, v in zip(scores, values):
        p = jnp.exp(s - m)
        d = p.sum(axis=-1, keepdims=True)
        o = _dot(p.astype(bf16), v)
        den = d if den is None else den + d
        acc = o if acc is None else acc + o
    if sink is not None:
        den = den + jnp.exp(sink - m)
    return acc / den


def _attn_ctx_kernel(sink_ref, qa_ref, ka_ref, va_ref, qb_ref, kb_ref, vb_ref, o_ref, *, layer):
    scale = HEAD_DIM ** -0.5
    for mixer, (q_ref, k_ref, v_ref) in enumerate(((qa_ref, ka_ref, va_ref), (qb_ref, kb_ref, vb_ref))):
        for kv in range(A_KV):
            ks = slice(kv * HEAD_DIM, (kv + 1) * HEAD_DIM)
            k = k_ref[:, ks].astype(bf16)
            v = v_ref[:, ks].astype(bf16)
            for g in range(A_HEADS // A_KV):
                hd = kv * 2 + g
                hs = slice(hd * HEAD_DIM, (hd + 1) * HEAD_DIM)
                q = q_ref[:, hs] * scale
                s = _dot_nt(q, k)
                sink = sink_ref[layer, hd] if mixer == 0 else None
                o = _softmax_pv([s], [v], sink)
                os_ = slice(mixer * D_Q + hd * HEAD_DIM, mixer * D_Q + (hd + 1) * HEAD_DIM)
                o_ref[:, os_] = o.astype(bf16)


def _attention_ctx(sink, qa, qb, kvp, layer, dims):
    batch, seq, dec_batch, dec_seq, depth = dims
    t = batch * seq + dec_batch * dec_seq
    qspec = pl.BlockSpec((seq, D_Q), lambda b: (b, 0))
    kspec = pl.BlockSpec((None, None, seq, D_KV), lambda b: (b, layer, 0, 0))
    akp, avp, bkp, bvp = kvp
    return pl.pallas_call(
        functools.partial(_attn_ctx_kernel, layer=layer),
        grid=(batch,),
        in_specs=[SMEM, qspec, kspec, kspec, qspec, kspec, kspec],
        out_specs=pl.BlockSpec((seq, D_AB), lambda b: (b, 0)),
        out_shape=jax.ShapeDtypeStruct((t, D_AB), bf16),
        compiler_params=_params(("arbitrary",)),
        name="attn_context",
    )(sink, qa, akp, avp, qb, bkp, bvp)


def _attn_lat_kernel(sink_ref, qa_ref, ka_ref, va_ref, cka_ref, cva_ref,
                     qb_ref, kb_ref, vb_ref, ckb_ref, cvb_ref, alias_ref, o_ref, *, seq, layer):
    del alias_ref
    j = pl.program_id(1)
    scale = HEAD_DIM ** -0.5
    w = WINDOW
    start = pl.multiple_of(jnp.clip((j - 1) * w, 0, seq - 3 * w), w)
    qi = j * w + lax.broadcasted_iota(jnp.int32, (w, 3 * w), 0)
    ki = start + lax.broadcasted_iota(jnp.int32, (w, 3 * w), 1)
    valid = jnp.abs(ki - qi) <= w
    for kv in range(A_KV):
        ks = slice(kv * HEAD_DIM, (kv + 1) * HEAD_DIM)
        ka = ka_ref[pl.ds(start, 3 * w), ks]
        va = va_ref[pl.ds(start, 3 * w), ks]
        cka = cka_ref[:, ks].astype(bf16)
        cva = cva_ref[:, ks].astype(bf16)
        kb = kb_ref[:, ks]
        vb = vb_ref[:, ks]
        ckb = ckb_ref[:, ks].astype(bf16)
        cvb = cvb_ref[:, ks].astype(bf16)
        for g in range(A_HEADS // A_KV):
            hd = kv * 2 + g
            hs = slice(hd * HEAD_DIM, (hd + 1) * HEAD_DIM)
            q = qa_ref[:, hs] * scale
            s_loc = jnp.where(valid, _dot_nt(q, ka), -jnp.inf)
            s_ctx = _dot_nt(q, cka)
            o_ref[:, hs] = _softmax_pv([s_loc, s_ctx], [va, cva], sink_ref[layer, hd]).astype(bf16)
            q = qb_ref[:, hs] * scale
            o = _softmax_pv([_dot_nt(q, kb), _dot_nt(q, ckb)], [vb, cvb], None)
            o_ref[:, slice(D_Q + hd * HEAD_DIM, D_Q + (hd + 1) * HEAD_DIM)] = o.astype(bf16)


def _attention_lat(sink, qa, qb, kvs, caches, oab, layer, dims):
    batch, seq, dec_batch, dec_seq, depth = dims
    w = WINDOW
    nq = dec_seq // w
    q0 = batch * seq // w
    past = caches[0].shape[2]
    qspec = pl.BlockSpec((w, D_Q), lambda b, j: (q0 + b * nq + j, 0))
    kspec = pl.BlockSpec((dec_seq, D_KV), lambda b, j: (b, 0))
    cspec = pl.BlockSpec((None, None, past, D_KV), lambda b, j: (b, layer, 0, 0))
    aks, avs, bks, bvs = kvs
    cka, cva, ckb, cvb = caches
    return pl.pallas_call(
        functools.partial(_attn_lat_kernel, seq=dec_seq, layer=layer),
        grid=(dec_batch, nq),
        in_specs=[SMEM, qspec, kspec, kspec, cspec, cspec, qspec, kspec, kspec, cspec, cspec, ANY],
        out_specs=pl.BlockSpec((w, D_AB), lambda b, j: (q0 + b * nq + j, 0)),
        out_shape=jax.ShapeDtypeStruct(oab.shape, oab.dtype),
        input_output_aliases={11: 0},
        compiler_params=_params(("arbitrary", "arbitrary")),
        name="attn_latent",
    )(sink, qa, aks, avs, cka, cva, qb, bks, bvs, ckb, cvb, oab)


def _ssd_kernel(*refs, latent, n_alias, nseq, n):
    xbc_ref, z_ref, dt_ref, cw_ref, cb_ref, alog_ref, dskip_ref, g_ref, sel_ref = refs[:9]
    if latent:
        h0f_ref, h0b_ref = refs[9:11]
        o_ref = refs[11 + n_alias]
    else:
        o_ref, hf_ref, hb_ref = refs[9 + n_alias:12 + n_alias]
    xc_scr, st_scr, he_scr, lhs_scr, dec_scr = refs[12 + n_alias:]
    nc = n // CHUNK
    L = CHUNK
    ns = SSM_STATE
    nh = SSM_HEADS
    hpg = SSM_HEADS // SSM_GROUPS
    hd_w = SSM_HEAD_DIM

    x = xbc_ref[...].astype(f32)
    t_idx = lax.broadcasted_iota(jnp.int32, x.shape, 0) % n
    prv = jnp.where(t_idx == 0, 0.0, pltpu.roll(x, 1, 0))
    nxt = jnp.where(t_idx == n - 1, 0.0, pltpu.roll(x, nseq * n - 1, 0))
    y = prv * cw_ref[0:1, :] + x * cw_ref[1:2, :] + nxt * cw_ref[2:3, :] + cb_ref[...]
    xc_scr[...] = _silu(y)

    a_neg = -jnp.exp(alog_ref[...])
    r_i = lax.broadcasted_iota(jnp.int32, (L, L), 0)
    c_i = lax.broadcasted_iota(jnp.int32, (L, L), 1)
    lower = r_i >= c_i
    upper = r_i <= c_i
    tril = lower.astype(f32).astype(bf16)

    def split3(v):
        v0 = v.astype(bf16)
        r1 = v - v0.astype(f32)
        v1 = r1.astype(bf16)
        return v0, v1, (r1 - v1.astype(f32)).astype(bf16)

    def prefix_sum(v):
        return sum(_dot(tril, t) for t in split3(v))

    def spread(v):
        return sum(_dot(t, sel_ref[...]) for t in split3(v))

    def stage1(c, carry):
        rows = pl.ds(pl.multiple_of(c * L, L), L)
        xs16 = xc_scr[rows, 0:D_SSM].astype(bf16)
        bm = xc_scr[rows, D_SSM:D_SSM + SSM_GROUPS * ns]
        cm = xc_scr[rows, D_SSM + SSM_GROUPS * ns:XBC_DIM]
        dt = dt_ref[rows, :]
        da = dt * a_neg
        cs = prefix_sum(da)
        tot = cs[L - 1:L, :]
        suf = tot - cs + da
        cs_t, suf_t, dt_t, b_t = cs.T, suf.T, dt.T, bm.T
        tot_c = cs_t[:, L - 1:L]
        wf_t = jnp.exp(tot_c[0:nh] - cs_t[0:nh]) * dt_t[0:nh]
        wb_t = jnp.exp(tot_c[nh:2 * nh] - suf_t[nh:2 * nh]) * dt_t[nh:2 * nh]
        dec_scr[c] = spread(jnp.broadcast_to(jnp.exp(tot), (8, LANES)))
        cm16 = cm.astype(bf16)
        bm16 = bm.astype(bf16)
        for g in range(SSM_GROUPS):
            gs = slice(g * ns, (g + 1) * ns)
            cb = _dot_nt(cm16[:, gs], bm16[:, gs])
            cg = cm[:, gs]
            bg_t = b_t[gs, :]
            for hh in range(hpg):
                hd = g * hpg + hh
                hb_ = nh + hd
                cols = slice(hd * hd_w, (hd + 1) * hd_w)
                lhs1 = jnp.concatenate([bg_t * wf_t[hd:hd + 1, :], bg_t * wb_t[hd:hd + 1, :]], axis=0)
                st_scr[c, :, cols] = _dot(lhs1.astype(bf16), xs16[:, cols])
                csb = jnp.broadcast_to(cs[:, hd:hd + 1], (L, L))
                sfb = jnp.broadcast_to(suf[:, hb_:hb_ + 1], (L, L))
                lf = jnp.exp(jnp.where(lower, csb - cs_t[hd:hd + 1, :], -jnp.inf))
                lb = jnp.exp(jnp.where(upper, sfb - suf_t[hb_:hb_ + 1, :], -jnp.inf))
                m = cb * (lf * dt_t[hd:hd + 1, :] + lb * dt_t[hb_:hb_ + 1, :])
                lhs_scr[c, hd] = jnp.concatenate(
                    [m, cg * jnp.exp(csb[:, 0:ns]), cg * jnp.exp(sfb[:, 0:ns])], axis=1).astype(bf16)
        return carry

    lax.fori_loop(0, nseq * nc, stage1, 0)

    def to_t(h):
        return jnp.concatenate([h, jnp.zeros_like(h)], axis=1).T[0:ns, :]

    def from_t(ht):
        return jnp.concatenate([ht, jnp.zeros_like(ht)], axis=0).T[:, 0:ns]

    for s in range(nseq):
        if latent:
            hf, hb = to_t(h0f_ref[s]), to_t(h0b_ref[s])
        else:
            hf = hb = jnp.zeros((ns, D_SSM), f32)
        for k in range(nc):
            cf = s * nc + k
            cr = s * nc + nc - 1 - k
            he_scr[cf, 0:ns, :] = hf.astype(bf16)
            hf = hf * dec_scr[cf, 0:1, 0:D_SSM] + st_scr[cf, 0:ns, :]
            he_scr[cr, ns:2 * ns, :] = hb.astype(bf16)
            hb = hb * dec_scr[cr, 0:1, D_SSM:2 * D_SSM] + st_scr[cr, ns:2 * ns, :]
        if not latent:
            hf_ref[s] = from_t(hf)
            hb_ref[s] = from_t(hb)

    def stage3(c, carry):
        rows = pl.ds(pl.multiple_of(c * L, L), L)
        xs = xc_scr[rows, 0:D_SSM]
        xs16 = xs.astype(bf16)
        ys = []
        for hd in range(nh):
            cols = slice(hd * hd_w, (hd + 1) * hd_w)
            rhs = jnp.concatenate([xs16[:, cols], he_scr[c, 0:ns, cols], he_scr[c, ns:2 * ns, cols]], axis=0)
            ys.append(_dot(lhs_scr[c, hd], rhs))
        yv = jnp.concatenate(ys, axis=1) + xs * dskip_ref[...]
        yv = yv * _silu(z_ref[rows, :].astype(f32))
        o_ref[rows, :] = (_rms(yv) * g_ref[...]).astype(bf16)
        return carry

    lax.fori_loop(0, nseq * nc, stage3, 0, unroll=2)


def _ssd(xbc, z, dt, consts, layer, dims, *, latent, init=None, oc=None, st_prev=()):
    batch, seq, dec_batch, dec_seq, depth = dims
    t = batch * seq + dec_batch * dec_seq
    if latent:
        n_b, n, nseq = dec_batch, dec_seq, 1
    else:
        n_b, n, nseq = batch, seq, SSD_CTX_SEQS
    rows = nseq * n
    b0 = batch * seq // rows if latent else 0

    def tok(width):
        return pl.BlockSpec((rows, width), lambda b: (b0 + b, 0))

    def lspec(shape):
        return pl.BlockSpec((None,) + shape, lambda b: (layer,) + (0,) * len(shape))

    st = pl.BlockSpec((nseq, None, D_SSM, SSM_STATE), lambda b: (b, layer, 0, 0))
    in_specs = [tok(XBC_DIM), tok(D_SSM), tok(LANES),
                lspec((3, XBC_DIM)), lspec((1, XBC_DIM)), lspec((1, LANES)), lspec((1, D_SSM)),
                lspec((1, D_SSM)), pl.BlockSpec((LANES, 2 * D_SSM), lambda b: (0, 0))]
    args = [xbc, z, dt, *consts, _head_spread_matrix()]
    oc_shape = jax.ShapeDtypeStruct((t, D_SSM), bf16)
    oc_spec = pl.BlockSpec((rows, D_SSM), lambda b: (b0 + b, 0))
    nck = rows // CHUNK
    scratch = [pltpu.VMEM((rows, XBC_DIM), f32),
               pltpu.VMEM((nck, 2 * SSM_STATE, D_SSM), f32),
               pltpu.VMEM((nck, 2 * SSM_STATE, D_SSM), bf16),
               pltpu.VMEM((nck, SSM_HEADS, CHUNK, 2 * CHUNK), bf16),
               pltpu.VMEM((nck, 8, 2 * D_SSM), f32)]
    if latent:
        in_specs += [st, st, ANY]
        args += [init[0], init[1], oc]
        out_specs, out_shape = [oc_spec], [oc_shape]
        aliases = {11: 0}
        n_alias = 1
    else:
        n_alias = len(st_prev)
        in_specs += [ANY] * n_alias
        args += list(st_prev)
        st_shape = jax.ShapeDtypeStruct((batch, depth, D_SSM, SSM_STATE), f32)
        out_specs, out_shape = [oc_spec, st, st], [oc_shape, st_shape, st_shape]
        aliases = {9 + k: 1 + k for k in range(n_alias)}
    return pl.pallas_call(
        functools.partial(_ssd_kernel, latent=latent, n_alias=n_alias, nseq=nseq, n=n),
        grid=(n_b // nseq,),
        in_specs=in_specs, out_specs=out_specs, out_shape=out_shape,
        scratch_shapes=scratch,
        input_output_aliases=aliases,
        compiler_params=_params(("arbitrary",)),
        name="ssd_latent" if latent else "ssd_context",
    )(*args)


def _outproj_kernel(*refs, n_x, n_p):
    x_refs = refs[:n_x]
    (oab_ref, oc_ref, g1_ref, sh_ref, sc_ref, n2_ref, w_ref, wr_ref, br_ref,
     x1_ref, h2_ref, comb_ref) = refs[n_x:]
    i = pl.program_id(0)
    o = _dot(oab_ref[...], w_ref[0:D_AB, :]) + _dot(oc_ref[...], w_ref[D_AB:, :])
    x1 = _load_x(x_refs, i, n_p) + g1_ref[...] * o
    x1_ref[...] = x1
    h2 = _rms(x1) * n2_ref[...]
    h2 = h2 * (1.0 + sc_ref[...]) + sh_ref[...]
    h2b = h2.astype(bf16)
    h2_ref[...] = h2b

    logits = _dot(h2b, wr_ref[...]) + br_ref[...]
    lane = lax.broadcasted_iota(jnp.int32, logits.shape, 1).astype(f32)
    big = float(LANES)
    neg = -jnp.inf
    gmask = (lane >= N_EXPERTS) & (lane < N_EXPERTS + N_EGROUPS)
    gl = jnp.where(gmask, logits, neg)
    gmax = gl.max(axis=-1, keepdims=True)
    gsel = jnp.where(gl == gmax, lane, big).min(axis=-1, keepdims=True) - N_EXPERTS
    gprob = 1.0 / jnp.where(gmask, jnp.exp(logits - gmax), 0.0).sum(axis=-1, keepdims=True)
    emask = (lane >= gsel * EXPERTS_PER_GROUP) & (lane < (gsel + 1) * EXPERTS_PER_GROUP)
    el = jnp.where(emask, logits, neg)
    v1 = el.max(axis=-1, keepdims=True)
    i1 = jnp.where(el == v1, lane, big).min(axis=-1, keepdims=True)
    el2 = jnp.where(lane == i1, neg, el)
    v2 = el2.max(axis=-1, keepdims=True)
    i2 = jnp.where(el2 == v2, lane, big).min(axis=-1, keepdims=True)
    e2 = jnp.exp(v2 - v1)
    den = 1.0 + e2
    comb_ref[...] = (jnp.where(lane == i1, gprob / den, 0.0)
                     + jnp.where(lane == i2, gprob * e2 / den, 0.0))


def _out_projection(oab, oc, xs, mod, layer, norm_g, w_out, wr, br, dims, tm=512):
    batch, seq, dec_batch, dec_seq, depth = dims
    n_p_tok, n_s_tok = batch * seq, dec_batch * dec_seq
    t = n_p_tok + n_s_tok
    til = _Tiling(n_p_tok, n_s_tok, dec_seq, tm)

    def lspec(shape):
        return pl.BlockSpec((None,) + shape, lambda i: (layer,) + (0,) * len(shape))

    tok = lambda w: pl.BlockSpec((tm, w), lambda i: (i, 0))
    return pl.pallas_call(
        functools.partial(_outproj_kernel, n_x=len(xs), n_p=til.n_p),
        grid=(til.n,),
        in_specs=_x_specs(til, len(xs) == 2) + [
            tok(D_AB), tok(D_SSM),
            _mod_spec(til, layer, 2), _mod_spec(til, layer, 3), _mod_spec(til, layer, 4),
            lspec((1, D_MODEL)), lspec((D_AB + D_SSM, D_MODEL)), lspec((D_MODEL, LANES)), lspec((1, LANES))],
        out_specs=[tok(D_MODEL), tok(D_MODEL), tok(LANES)],
        out_shape=[jax.ShapeDtypeStruct((t, D_MODEL), f32),
                   jax.ShapeDtypeStruct((t, D_MODEL), bf16),
                   jax.ShapeDtypeStruct((t, LANES), f32)],
        compiler_params=_params(("arbitrary",)),
        name="outproj_norm_router",
    )(*xs, oab, oc, mod, mod, mod, norm_g, w_out, wr, br)


def _moe_kernel(h2_ref, comb_ref, wg_ref, wu_ref, wd_ref, x1_ref, g2_ref, fg_ref, *outs, final, n_p):
    i = pl.program_id(0)
    h = h2_ref[...]
    comb = comb_ref[...]
    hid = []
    for e in range(N_EXPERTS):
        a = _dot(h, wg_ref[e])
        u = _dot(h, wu_ref[e])
        hid.append((_silu(a) * u * comb[:, e:e + 1]).astype(bf16))
    y = _dot(jnp.concatenate(hid, axis=1), wd_ref[...])
    x2 = x1_ref[...] + g2_ref[...] * y
    if final:
        yp_ref, ys_ref = outs
        x2 = _rms(x2) * fg_ref[...]

        @pl.when(i < n_p)
        def _():
            yp_ref[...] = x2

        @pl.when(i >= n_p)
        def _():
            ys_ref[...] = x2
    else:
        outs[0][...] = x2


def _moe(h2, comb, wg, wu, wd, x1, mod, layer, final_g, dims, final, tm=512):
    batch, seq, dec_batch, dec_seq, depth = dims
    n_p_tok, n_s_tok = batch * seq, dec_batch * dec_seq
    t = n_p_tok + n_s_tok
    til = _Tiling(n_p_tok, n_s_tok, dec_seq, tm)
    once = pl.Buffered(1)
    tok = lambda w: pl.BlockSpec((tm, w), lambda i: (i, 0))
    if final:
        out_specs = [pl.BlockSpec((tm, D_MODEL), lambda i: (til.p_idx(i), 0)),
                     pl.BlockSpec((tm, D_MODEL), lambda i: (til.s_idx(i), 0))]
        out_shape = [jax.ShapeDtypeStruct((n_p_tok, D_MODEL), f32),
                     jax.ShapeDtypeStruct((n_s_tok, D_MODEL), f32)]
    else:
        out_specs = [tok(D_MODEL)]
        out_shape = [jax.ShapeDtypeStruct((t, D_MODEL), f32)]
    return pl.pallas_call(
        functools.partial(_moe_kernel, final=final, n_p=til.n_p),
        grid=(til.n,),
        in_specs=[tok(D_MODEL), tok(LANES),
                  pl.BlockSpec((None, N_EXPERTS, D_MODEL, D_EXPERT), lambda i: (layer, 0, 0, 0), once),
                  pl.BlockSpec((None, N_EXPERTS, D_MODEL, D_EXPERT), lambda i: (layer, 0, 0, 0), once),
                  pl.BlockSpec((None, N_EXPERTS * D_EXPERT, D_MODEL), lambda i: (layer, 0, 0), once),
                  tok(D_MODEL), _mod_spec(til, layer, 5),
                  pl.BlockSpec((1, D_MODEL), lambda i: (0, 0))],
        out_specs=out_specs, out_shape=out_shape,
        compiler_params=_params(("arbitrary",)),
        name="moe_ffn",
    )(h2, comb, wg, wu, wd.reshape(depth, N_EXPERTS * D_EXPERT, D_MODEL), x1, mod, final_g)


def _rope_tables(n_rows):
    rows = jnp.repeat(jnp.arange(n_rows), GRID_W).astype(f32)
    cols = jnp.tile(jnp.arange(GRID_W), n_rows).astype(f32)
    inv = ROPE_THETA ** (-jnp.arange(ROPE_QUARTER, dtype=f32) / ROPE_QUARTER)
    ang_r = rows[:, None] * inv
    ang_c = cols[:, None] * inv
    ang = jnp.concatenate([ang_r, ang_r, ang_c, ang_c], axis=-1)
    cos, sin = jnp.cos(ang), jnp.sin(ang)
    even = (np.arange(HEAD_DIM) // ROPE_QUARTER) % 2 == 0
    sin_even = jnp.where(even, -sin, 0.0)
    sin_odd = jnp.where(even, 0.0, sin)
    return tuple(jnp.tile(t, (1, D_Q // HEAD_DIM)) for t in (cos, sin_even, sin_odd))


def _head_sum_matrix():
    m = np.zeros((D_Q, D_Q), np.float32)
    for h in range(D_Q // HEAD_DIM):
        m[h * HEAD_DIM:(h + 1) * HEAD_DIM, h * HEAD_DIM:(h + 1) * HEAD_DIM] = 1.0
    return jnp.asarray(m, dtype=bf16)


def _head_spread_matrix():
    m = np.zeros((LANES, 2 * D_SSM), np.float32)
    for j in range(2 * SSM_HEADS):
        m[j, j * SSM_HEAD_DIM:(j + 1) * SSM_HEAD_DIM] = 1.0
    return jnp.asarray(m, dtype=bf16)


def _pad_last(v, width=LANES):
    pad = [(0, 0)] * (v.ndim - 1) + [(0, width - v.shape[-1])]
    return jnp.pad(v, pad)


@jax.jit
def kernel(x_prompt, x_sample, cache_a_k, cache_a_v, cache_b_k, cache_b_v, state_ssm_fwd, state_ssm_bwd, c, c_ctx, norm1_g, norm2_g, final_norm_g, w_ada, b_ada, w_in, a_sink, q_norm_g, k_norm_g, conv_w, conv_b, dt_bias, a_log, d_skip, ssm_norm_g, w_out, w_router_group, b_router_group, w_router_expert, b_router_expert, w_gate, w_up, w_down):
    batch, seq, _ = x_prompt.shape
    dec_batch, dec_seq, _ = x_sample.shape
    depth = w_in.shape[0]
    past = cache_a_k.shape[2]
    dims = (batch, seq, dec_batch, dec_seq, depth)
    n_p_tok = batch * seq
    n_s_tok = dec_batch * dec_seq

    cvec = jnp.concatenate([c_ctx[None, :], c, jnp.zeros((8 - 1 - dec_batch, D_MODEL), f32)], axis=0)
    mod = _modulation(cvec, w_ada, b_ada).reshape(depth, 8, 6, 1, D_MODEL)

    rope = _rope_tables(dec_seq // GRID_W)
    headmat = _head_sum_matrix()
    w_main = w_in[:, :, :C_DT].astype(bf16)
    w_dt = _pad_last(w_in[:, :, C_DT:]).astype(bf16)
    qg = jnp.tile(q_norm_g, (1, D_Q // HEAD_DIM)).reshape(depth, 1, D_Q)
    kg = jnp.tile(k_norm_g, (1, D_Q // HEAD_DIM)).reshape(depth, 1, D_Q)
    dtb = _pad_last(dt_bias.reshape(depth, 1, 2 * SSM_HEADS))
    n1 = norm1_g.reshape(depth, 1, D_MODEL)
    n2 = norm2_g.reshape(depth, 1, D_MODEL)
    ssd_consts = (conv_w, conv_b.reshape(depth, 1, XBC_DIM),
                  _pad_last(a_log.reshape(depth, 1, 2 * SSM_HEADS)),
                  jnp.repeat(d_skip, SSM_HEAD_DIM, axis=-1).reshape(depth, 1, D_SSM),
                  ssm_norm_g.reshape(depth, 1, D_SSM))
    w_out16 = w_out.astype(bf16)
    wr = _pad_last(jnp.concatenate([w_router_expert, w_router_group], axis=-1)).astype(bf16)
    br = _pad_last(jnp.concatenate([b_router_expert, b_router_group], axis=-1)).reshape(depth, 1, LANES)
    wg16, wu16, wd16 = w_gate.astype(bf16), w_up.astype(bf16), w_down.astype(bf16)
    fg = final_norm_g.reshape(1, D_MODEL)

    caches = tuple(t.reshape(dec_batch, depth, past, D_KV) for t in (cache_a_k, cache_a_v, cache_b_k, cache_b_v))
    init = (state_ssm_fwd.reshape(dec_batch, depth, D_SSM, SSM_STATE),
            state_ssm_bwd.reshape(dec_batch, depth, D_SSM, SSM_STATE))

    xs = (x_prompt.reshape(n_p_tok, D_MODEL), x_sample.reshape(n_s_tok, D_MODEL))
    kvp, states = (), ()
    for l in range(depth):
        outs = _in_projection(xs, mod, l, n1, w_main, w_dt, qg, kg, dtb, headmat, rope, kvp, dims)
        qa, qb = outs[0:2]
        kvp, kvs = tuple(outs[2:6]), tuple(outs[6:10])
        z, xbc, dt = outs[10:13]

        oab = _attention_ctx(a_sink, qa, qb, kvp, l, dims)
        oab = _attention_lat(a_sink, qa, qb, kvs, caches, oab, l, dims)

        oc, hf, hb = _ssd(xbc, z, dt, ssd_consts, l, dims, latent=False, st_prev=states)
        states = (hf, hb)
        oc, = _ssd(xbc, z, dt, ssd_consts, l, dims, latent=True, init=init, oc=oc)

        x1, h2, comb = _out_projection(oab, oc, xs, mod, l, n2, w_out16, wr, br, dims)
        xs = tuple(_moe(h2, comb, wg16, wu16, wd16, x1, mod, l, fg, dims, final=(l == depth - 1)))

    y_prompt = xs[0].reshape(batch, seq, D_MODEL)
    y_sample = xs[1].reshape(dec_batch, dec_seq, D_MODEL)
    kv_shape = (batch, depth, seq, A_KV, HEAD_DIM)
    st_shape = (batch, depth, SSM_HEADS, SSM_HEAD_DIM, SSM_STATE)
    return ((y_prompt, y_sample) + tuple(t.reshape(kv_shape) for t in kvp)
            + tuple(t.reshape(st_shape) for t in states))
```

```python
import functools

import jax
import jax.numpy as jnp
import numpy as np
from jax import lax
from jax.experimental import pallas as pl
from jax.experimental.pallas import tpu as pltpu

f32 = jnp.float32
bf16 = jnp.bfloat16
HIGHEST = lax.Precision.HIGHEST

D_MODEL = 1024
GRID_W = 64
HEAD_DIM = 64
A_HEADS = 4
A_KV = 2
WINDOW = 128
B_HEADS = 4
B_KV = 2
ROPE_THETA = 10000.0
ROPE_QUARTER = HEAD_DIM // 4
SSM_HEADS = 8
SSM_HEAD_DIM = 64
D_SSM = SSM_HEADS * SSM_HEAD_DIM
SSM_GROUPS = 2
SSM_STATE = 64
CHUNK = 128
XBC_DIM = D_SSM + 2 * SSM_GROUPS * SSM_STATE
D_AB = (A_HEADS + B_HEADS) * HEAD_DIM
D_Q = A_HEADS * HEAD_DIM
D_KV = A_KV * HEAD_DIM
N_EGROUPS = 4
EXPERTS_PER_GROUP = 4
N_EXPERTS = N_EGROUPS * EXPERTS_PER_GROUP
D_EXPERT = 256
EPS = 1e-6

LANES = 128
C_AQ, C_AK, C_AV, C_BQ, C_BK, C_BV, C_Z, C_XBC, C_DT = 0, 256, 384, 512, 768, 896, 1024, 1536, 2304

SSD_CTX_SEQS = 4

GSEL_LANE = N_EXPERTS
MOE_TM = 1024
MOE_SB = 256
MOE_CAP = 96
MOE_FIN = 256

VMEM_LIMIT = 56 * 1024 * 1024
ANY = pl.BlockSpec(memory_space=pl.ANY)
SMEM = pl.BlockSpec(memory_space=pltpu.SMEM)


def _params(sem, vmem=VMEM_LIMIT):
    return pltpu.CompilerParams(dimension_semantics=sem, vmem_limit_bytes=vmem)


def _dot(a, b, **kw):
    return jnp.dot(a, b, preferred_element_type=f32, **kw)


def _dot_nt(a, b):
    return lax.dot_general(a, b, (((1,), (1,)), ((), ())), preferred_element_type=f32)


def _dot_tn(a, b):
    return lax.dot_general(a, b, (((0,), (0,)), ((), ())), preferred_element_type=f32)


def _silu(x):
    return (0.5 * x) * (1.0 + jnp.tanh(0.5 * x))


def _softplus(x):
    return jnp.maximum(x, 0.0) + jnp.log1p(jnp.exp(-jnp.abs(x)))


def _rms(x):
    return x * lax.rsqrt(jnp.mean(x * x, axis=-1, keepdims=True) + EPS)


class _Tiling:
    def __init__(self, n_p_tok, n_s_tok, dec_seq, tm):
        self.tm = tm
        self.n_p = n_p_tok // tm
        self.n_s = n_s_tok // tm
        self.n = self.n_p + self.n_s
        self.per_seq = dec_seq // tm

    def p_idx(self, i):
        return jnp.minimum(i, self.n_p - 1)

    def s_idx(self, i):
        return jnp.maximum(i - self.n_p, 0)

    def mod_row(self, i):
        return jnp.where(i < self.n_p, 0, 1 + (i - self.n_p) // self.per_seq)

    def seq_pos(self, i):
        return jnp.where(i < self.n_p, 0, (i - self.n_p) % self.per_seq)


def _x_specs(til, split):
    tm = til.tm
    if split:
        return [pl.BlockSpec((tm, D_MODEL), lambda i, *_: (til.p_idx(i), 0)),
                pl.BlockSpec((tm, D_MODEL), lambda i, *_: (til.s_idx(i), 0))]
    return [pl.BlockSpec((tm, D_MODEL), lambda i, *_: (i, 0))]


def _load_x(refs, i, n_p):
    if len(refs) == 2:
        return jnp.where(i < n_p, refs[0][...], refs[1][...])
    return refs[0][...]


def _mod_spec(til, layer, k):
    return pl.BlockSpec((None, None, None, 1, D_MODEL), lambda i, *_: (layer, til.mod_row(i), k, 0, 0))


def _mod_kernel(c_ref, w_ref, b_ref, o_ref):
    s = _silu(c_ref[...])
    o_ref[...] = _dot(s, w_ref[...], precision=HIGHEST) + b_ref[...]


def _modulation(cvec, w_ada, b_ada):
    depth = w_ada.shape[0]
    n = w_ada.shape[2]
    tn = 1536
    return pl.pallas_call(
        _mod_kernel,
        grid=(depth, n // tn),
        in_specs=[
            pl.BlockSpec((8, D_MODEL), lambda l, j: (0, 0)),
            pl.BlockSpec((None, D_MODEL, tn), lambda l, j: (l, 0, j)),
            pl.BlockSpec((None, 1, tn), lambda l, j: (l, 0, j)),
        ],
        out_specs=pl.BlockSpec((None, 8, tn), lambda l, j: (l, 0, j)),
        out_shape=jax.ShapeDtypeStruct((depth, 8, n), f32),
        compiler_params=_params(("arbitrary", "arbitrary")),
        name="adaln_mod",
    )(cvec, w_ada, b_ada.reshape(depth, 1, n))


def _rope(x, cos, sin_even, sin_odd):
    w = x.shape[-1]
    nxt = pltpu.roll(x, w - ROPE_QUARTER, 1)
    prv = pltpu.roll(x, ROPE_QUARTER, 1)
    return x * cos + nxt * sin_even + prv * sin_odd


def _inproj_kernel(*refs, n_x, n_alias, n_p, seqs_per_tile):
    x_refs = refs[:n_x]
    (sh_ref, sc_ref, g_ref, w_ref, wdt_ref, qg_ref, kg_ref, dtb_ref, hm_ref,
     cos_ref, se_ref, so_ref) = refs[n_x:n_x + 12]
    (qa_ref, qb_ref, akp_ref, avp_ref, bkp_ref, bvp_ref, aks_ref, avs_ref, bks_ref, bvs_ref,
     z_ref, xbc_ref, dt_ref) = refs[n_x + 12 + n_alias:]
    i = pl.program_id(0)
    h = _rms(_load_x(x_refs, i, n_p)) * g_ref[...]
    h = h * (1.0 + sc_ref[...]) + sh_ref[...]
    hb = h.astype(bf16)

    def proj(lo, hi):
        return _dot(hb, w_ref[:, lo:hi])

    def head_norm(t, gain):
        w = t.shape[-1]
        sq = t * t
        hi = sq.astype(bf16)
        lo = (sq - hi.astype(f32)).astype(bf16)
        ms_h = (_dot(hi, hm_ref[0:w, 0:w]) + _dot(lo, hm_ref[0:w, 0:w])) * (1.0 / HEAD_DIM)
        return t * lax.rsqrt(ms_h + EPS) * gain

    qa = proj(C_AQ, C_AK)
    ka = proj(C_AK, C_AV)
    va = proj(C_AV, C_BQ)
    qb = head_norm(proj(C_BQ, C_BK), qg_ref[...])
    kb = head_norm(proj(C_BK, C_BV), kg_ref[:, 0:D_KV])
    vb = proj(C_BV, C_Z)
    z_ref[...] = proj(C_Z, C_XBC).astype(bf16)
    xbc_ref[...] = proj(C_XBC, C_DT).astype(bf16)
    dt_ref[...] = _softplus(_dot(hb, wdt_ref[...]) + dtb_ref[...])

    lat = i >= n_p
    cos = jnp.where(lat, cos_ref[...], 1.0)
    se = jnp.where(lat, se_ref[...], 0.0)
    so = jnp.where(lat, so_ref[...], 0.0)
    qa_ref[...] = _rope(qa, cos, se, so).astype(bf16)
    qb_ref[...] = _rope(qb, cos, se, so).astype(bf16)
    aks_ref[...] = _rope(ka, cos[:, :D_KV], se[:, :D_KV], so[:, :D_KV]).astype(bf16)
    bks_ref[...] = _rope(kb, cos[:, :D_KV], se[:, :D_KV], so[:, :D_KV]).astype(bf16)
    avs_ref[...] = va.astype(bf16)
    bvs_ref[...] = vb.astype(bf16)

    @pl.when(i < n_p)
    def _():
        shp = (seqs_per_tile, -1, D_KV)
        akp_ref[...] = ka.reshape(shp)
        avp_ref[...] = va.reshape(shp)
        bkp_ref[...] = kb.reshape(shp)
        bvp_ref[...] = vb.reshape(shp)


def _in_projection(xs, mod, layer, norm_g, w_main, w_dt, qg, kg, dtb, headmat, rope, kv_prev, dims, tm=512):
    batch, seq, dec_batch, dec_seq, depth = dims
    n_p_tok, n_s_tok = batch * seq, dec_batch * dec_seq
    t = n_p_tok + n_s_tok
    til = _Tiling(n_p_tok, n_s_tok, dec_seq, tm)
    spt = tm // seq
    cos, se, so = rope
    n_alias = len(kv_prev)

    def c2(shape):
        return pl.BlockSpec(shape, lambda i: (0, 0))

    def lspec(shape):
        return pl.BlockSpec((None,) + shape, lambda i: (layer,) + (0,) * len(shape))

    rope_spec = pl.BlockSpec((tm, D_Q), lambda i: (til.seq_pos(i), 0))
    tok = lambda w: pl.BlockSpec((tm, w), lambda i: (i, 0))
    kvp = pl.BlockSpec((spt, None, seq, D_KV), lambda i: (til.p_idx(i), layer, 0, 0))
    kvs = pl.BlockSpec((tm, D_KV), lambda i: (til.s_idx(i), 0))
    kvp_shape = jax.ShapeDtypeStruct((batch, depth, seq, D_KV), f32)
    kvs_shape = jax.ShapeDtypeStruct((n_s_tok, D_KV), bf16)
    n_in = len(xs) + 12
    return pl.pallas_call(
        functools.partial(_inproj_kernel, n_x=len(xs), n_alias=n_alias, n_p=til.n_p, seqs_per_tile=spt),
        grid=(til.n,),
        in_specs=_x_specs(til, len(xs) == 2) + [
            _mod_spec(til, layer, 0), _mod_spec(til, layer, 1),
            lspec((1, D_MODEL)), lspec((D_MODEL, C_DT)), lspec((D_MODEL, LANES)),
            lspec((1, D_Q)), lspec((1, D_Q)), lspec((1, LANES)), c2((D_Q, D_Q)),
            rope_spec, rope_spec, rope_spec] + [ANY] * n_alias,
        out_specs=[tok(D_Q), tok(D_Q), kvp, kvp, kvp, kvp, kvs, kvs, kvs, kvs,
                   tok(D_SSM), tok(XBC_DIM), tok(LANES)],
        out_shape=[jax.ShapeDtypeStruct((t, D_Q), bf16), jax.ShapeDtypeStruct((t, D_Q), bf16),
                   kvp_shape, kvp_shape, kvp_shape, kvp_shape,
                   kvs_shape, kvs_shape, kvs_shape, kvs_shape,
                   jax.ShapeDtypeStruct((t, D_SSM), bf16), jax.ShapeDtypeStruct((t, XBC_DIM), bf16),
                   jax.ShapeDtypeStruct((t, LANES), f32)],
        input_output_aliases={n_in + k: 2 + k for k in range(n_alias)},
        compiler_params=_params(("arbitrary",)),
        name="norm_mod_inproj",
    )(*xs, mod, mod, norm_g, w_main, w_dt, qg, kg, dtb, headmat, cos, se, so, *kv_prev)


def _softmax_pv(scores, values, sink):
    m = scores[0].max(axis=-1, keepdims=True)
    for s in scores[1:]:
        m = jnp.maximum(m, s.max(axis=-1, keepdims=True))
    if sink is not None:
        m = jnp.maximum(m, sink)
    den = None
    acc = None
    for s, v in zip(scores, values):
        p = jnp.exp(s - m)
        d = p.sum(axis=-1, keepdims=True)
        o = _dot(p.astype(bf16), v)
        den = d if den is None else den + d
        acc = o if acc is None else acc + o
    if sink is not None:
        den = den + jnp.exp(sink - m)
    return acc / den


def _attn_ctx_kernel(sink_ref, qa_ref, ka_ref, va_ref, qb_ref, kb_ref, vb_ref, o_ref, *, layer):
    scale = HEAD_DIM ** -0.5
    for mixer, (q_ref, k_ref, v_ref) in enumerate(((qa_ref, ka_ref, va_ref), (qb_ref, kb_ref, vb_ref))):
        for kv in range(A_KV):
            ks = slice(kv * HEAD_DIM, (kv + 1) * HEAD_DIM)
            k = k_ref[:, ks].astype(bf16)
            v = v_ref[:, ks].astype(bf16)
            for g in range(A_HEADS // A_KV):
                hd = kv * 2 + g
                hs = slice(hd * HEAD_DIM, (hd + 1) * HEAD_DIM)
                q = q_ref[:, hs] * scale
                s = _dot_nt(q, k)
                sink = sink_ref[layer, hd] if mixer == 0 else None
                o = _softmax_pv([s], [v], sink)
                os_ = slice(mixer * D_Q + hd * HEAD_DIM, mixer * D_Q + (hd + 1) * HEAD_DIM)
                o_ref[:, os_] = o.astype(bf16)


def _attention_ctx(sink, qa, qb, kvp, layer, dims):
    batch, seq, dec_batch, dec_seq, depth = dims
    t = batch * seq + dec_batch * dec_seq
    qspec = pl.BlockSpec((seq, D_Q), lambda b: (b, 0))
    kspec = pl.BlockSpec((None, None, seq, D_KV), lambda b: (b, layer, 0, 0))
    akp, avp, bkp, bvp = kvp
    return pl.pallas_call(
        functools.partial(_attn_ctx_kernel, layer=layer),
        grid=(batch,),
        in_specs=[SMEM, qspec, kspec, kspec, qspec, kspec, kspec],
        out_specs=pl.BlockSpec((seq, D_AB), lambda b: (b, 0)),
        out_shape=jax.ShapeDtypeStruct((t, D_AB), bf16),
        compiler_params=_params(("arbitrary",)),
        name="attn_context",
    )(sink, qa, akp, avp, qb, bkp, bvp)


def _attn_lat_kernel(sink_ref, qa_ref, ka_ref, va_ref, cka_ref, cva_ref,
                     qb_ref, kb_ref, vb_ref, ckb_ref, cvb_ref, alias_ref, o_ref, *, seq, layer):
    del alias_ref
    j = pl.program_id(1)
    scale = HEAD_DIM ** -0.5
    w = WINDOW
    start = pl.multiple_of(jnp.clip((j - 1) * w, 0, seq - 3 * w), w)
    qi = j * w + lax.broadcasted_iota(jnp.int32, (w, 3 * w), 0)
    ki = start + lax.broadcasted_iota(jnp.int32, (w, 3 * w), 1)
    valid = jnp.abs(ki - qi) <= w
    for kv in range(A_KV):
        ks = slice(kv * HEAD_DIM, (kv + 1) * HEAD_DIM)
        ka = ka_ref[pl.ds(start, 3 * w), ks]
        va = va_ref[pl.ds(start, 3 * w), ks]
        cka = cka_ref[:, ks].astype(bf16)
        cva = cva_ref[:, ks].astype(bf16)
        kb = kb_ref[:, ks]
        vb = vb_ref[:, ks]
        ckb = ckb_ref[:, ks].astype(bf16)
        cvb = cvb_ref[:, ks].astype(bf16)
        for g in range(A_HEADS // A_KV):
            hd = kv * 2 + g
            hs = slice(hd * HEAD_DIM, (hd + 1) * HEAD_DIM)
            q = qa_ref[:, hs] * scale
            s_loc = jnp.where(valid, _dot_nt(q, ka), -jnp.inf)
            s_ctx = _dot_nt(q, cka)
            o_ref[:, hs] = _softmax_pv([s_loc, s_ctx], [va, cva], sink_ref[layer, hd]).astype(bf16)
            q = qb_ref[:, hs] * scale
            o = _softmax_pv([_dot_nt(q, kb), _dot_nt(q, ckb)], [vb, cvb], None)
            o_ref[:, slice(D_Q + hd * HEAD_DIM, D_Q + (hd + 1) * HEAD_DIM)] = o.astype(bf16)


def _attention_lat(sink, qa, qb, kvs, caches, oab, layer, dims):
    batch, seq, dec_batch, dec_seq, depth = dims
    w = WINDOW
    nq = dec_seq // w
    q0 = batch * seq // w
    past = caches[0].shape[2]
    qspec = pl.BlockSpec((w, D_Q), lambda b, j: (q0 + b * nq + j, 0))
    kspec = pl.BlockSpec((dec_seq, D_KV), lambda b, j: (b, 0))
    cspec = pl.BlockSpec((None, None, past, D_KV), lambda b, j: (b, layer, 0, 0))
    aks, avs, bks, bvs = kvs
    cka, cva, ckb, cvb = caches
    return pl.pallas_call(
        functools.partial(_attn_lat_kernel, seq=dec_seq, layer=layer),
        grid=(dec_batch, nq),
        in_specs=[SMEM, qspec, kspec, kspec, cspec, cspec, qspec, kspec, kspec, cspec, cspec, ANY],
        out_specs=pl.BlockSpec((w, D_AB), lambda b, j: (q0 + b * nq + j, 0)),
        out_shape=jax.ShapeDtypeStruct(oab.shape, oab.dtype),
        input_output_aliases={11: 0},
        compiler_params=_params(("arbitrary", "arbitrary")),
        name="attn_latent",
    )(sink, qa, aks, avs, cka, cva, qb, bks, bvs, ckb, cvb, oab)


def _ssd_kernel(*refs, latent, n_alias, nseq, n):
    xbc_ref, z_ref, dt_ref, cw_ref, cb_ref, alog_ref, dskip_ref, g_ref, sel_ref = refs[:9]
    if latent:
        h0f_ref, h0b_ref = refs[9:11]
        o_ref = refs[11 + n_alias]
    else:
        o_ref, hf_ref, hb_ref = refs[9 + n_alias:12 + n_alias]
    xc_scr, st_scr, he_scr, lhs_scr, dec_scr = refs[12 + n_alias:]
    nc = n // CHUNK
    L = CHUNK
    ns = SSM_STATE
    nh = SSM_HEADS
    hpg = SSM_HEADS // SSM_GROUPS
    hd_w = SSM_HEAD_DIM

    x = xbc_ref[...].astype(f32)
    t_idx = lax.broadcasted_iota(jnp.int32, x.shape, 0) % n
    prv = jnp.where(t_idx == 0, 0.0, pltpu.roll(x, 1, 0))
    nxt = jnp.where(t_idx == n - 1, 0.0, pltpu.roll(x, nseq * n - 1, 0))
    y = prv * cw_ref[0:1, :] + x * cw_ref[1:2, :] + nxt * cw_ref[2:3, :] + cb_ref[...]
    xc_scr[...] = _silu(y)

    a_neg = -jnp.exp(alog_ref[...])
    r_i = lax.broadcasted_iota(jnp.int32, (L, L), 0)
    c_i = lax.broadcasted_iota(jnp.int32, (L, L), 1)
    lower = r_i >= c_i
    upper = r_i <= c_i
    tril = lower.astype(f32).astype(bf16)

    def split3(v):
        v0 = v.astype(bf16)
        r1 = v - v0.astype(f32)
        v1 = r1.astype(bf16)
        return v0, v1, (r1 - v1.astype(f32)).astype(bf16)

    def prefix_sum(v):
        return sum(_dot(tril, t) for t in split3(v))

    def spread(v):
        return sum(_dot(t, sel_ref[...]) for t in split3(v))

    def stage1(c, carry):
        rows = pl.ds(pl.multiple_of(c * L, L), L)
        xs16 = xc_scr[rows, 0:D_SSM].astype(bf16)
        bm = xc_scr[rows, D_SSM:D_SSM + SSM_GROUPS * ns]
        cm = xc_scr[rows, D_SSM + SSM_GROUPS * ns:XBC_DIM]
        dt = dt_ref[rows, :]
        da = dt * a_neg
        cs = prefix_sum(da)
        tot = cs[L - 1:L, :]
        suf = tot - cs + da
        cs_t, suf_t, dt_t, b_t = cs.T, suf.T, dt.T, bm.T
        tot_c = cs_t[:, L - 1:L]
        wf_t = jnp.exp(tot_c[0:nh] - cs_t[0:nh]) * dt_t[0:nh]
        wb_t = jnp.exp(tot_c[nh:2 * nh] - suf_t[nh:2 * nh]) * dt_t[nh:2 * nh]
        dec_scr[c] = spread(jnp.broadcast_to(jnp.exp(tot), (8, LANES)))
        cm16 = cm.astype(bf16)
        bm16 = bm.astype(bf16)
        for g in range(SSM_GROUPS):
            gs = slice(g * ns, (g + 1) * ns)
            cb = _dot_nt(cm16[:, gs], bm16[:, gs])
            cg = cm[:, gs]
            bg_t = b_t[gs, :]
            for hh in range(hpg):
                hd = g * hpg + hh
                hb_ = nh + hd
                cols = slice(hd * hd_w, (hd + 1) * hd_w)
                lhs1 = jnp.concatenate([bg_t * wf_t[hd:hd + 1, :], bg_t * wb_t[hd:hd + 1, :]], axis=0)
                st_scr[c, :, cols] = _dot(lhs1.astype(bf16), xs16[:, cols])
                csb = jnp.broadcast_to(cs[:, hd:hd + 1], (L, L))
                sfb = jnp.broadcast_to(suf[:, hb_:hb_ + 1], (L, L))
                lf = jnp.exp(jnp.where(lower, csb - cs_t[hd:hd + 1, :], -jnp.inf))
                lb = jnp.exp(jnp.where(upper, sfb - suf_t[hb_:hb_ + 1, :], -jnp.inf))
                m = cb * (lf * dt_t[hd:hd + 1, :] + lb * dt_t[hb_:hb_ + 1, :])
                lhs_scr[c, hd] = jnp.concatenate(
                    [m, cg * jnp.exp(csb[:, 0:ns]), cg * jnp.exp(sfb[:, 0:ns])], axis=1).astype(bf16)
        return carry

    lax.fori_loop(0, nseq * nc, stage1, 0)

    def to_t(h):
        return jnp.concatenate([h, jnp.zeros_like(h)], axis=1).T[0:ns, :]

    def from_t(ht):
        return jnp.concatenate([ht, jnp.zeros_like(ht)], axis=0).T[:, 0:ns]

    for s in range(nseq):
        if latent:
            hf, hb = to_t(h0f_ref[s]), to_t(h0b_ref[s])
        else:
            hf = hb = jnp.zeros((ns, D_SSM), f32)
        for k in range(nc):
            cf = s * nc + k
            cr = s * nc + nc - 1 - k
            he_scr[cf, 0:ns, :] = hf.astype(bf16)
            hf = hf * dec_scr[cf, 0:1, 0:D_SSM] + st_scr[cf, 0:ns, :]
            he_scr[cr, ns:2 * ns, :] = hb.astype(bf16)
            hb = hb * dec_scr[cr, 0:1, D_SSM:2 * D_SSM] + st_scr[cr, ns:2 * ns, :]
        if not latent:
            hf_ref[s] = from_t(hf)
            hb_ref[s] = from_t(hb)

    def stage3(c, carry):
        rows = pl.ds(pl.multiple_of(c * L, L), L)
        xs = xc_scr[rows, 0:D_SSM]
        xs16 = xs.astype(bf16)
        ys = []
        for hd in range(nh):
            cols = slice(hd * hd_w, (hd + 1) * hd_w)
            rhs = jnp.concatenate([xs16[:, cols], he_scr[c, 0:ns, cols], he_scr[c, ns:2 * ns, cols]], axis=0)
            ys.append(_dot(lhs_scr[c, hd], rhs))
        yv = jnp.concatenate(ys, axis=1) + xs * dskip_ref[...]
        yv = yv * _silu(z_ref[rows, :].astype(f32))
        o_ref[rows, :] = (_rms(yv) * g_ref[...]).astype(bf16)
        return carry

    lax.fori_loop(0, nseq * nc, stage3, 0, unroll=2)


def _ssd(xbc, z, dt, consts, layer, dims, *, latent, init=None, oc=None, st_prev=()):
    batch, seq, dec_batch, dec_seq, depth = dims
    t = batch * seq + dec_batch * dec_seq
    if latent:
        n_b, n, nseq = dec_batch, dec_seq, 1
    else:
        n_b, n, nseq = batch, seq, SSD_CTX_SEQS
    rows = nseq * n
    b0 = batch * seq // rows if latent else 0

    def tok(width):
        return pl.BlockSpec((rows, width), lambda b: (b0 + b, 0))

    def lspec(shape):
        return pl.BlockSpec((None,) + shape, lambda b: (layer,) + (0,) * len(shape))

    st = pl.BlockSpec((nseq, None, D_SSM, SSM_STATE), lambda b: (b, layer, 0, 0))
    in_specs = [tok(XBC_DIM), tok(D_SSM), tok(LANES),
                lspec((3, XBC_DIM)), lspec((1, XBC_DIM)), lspec((1, LANES)), lspec((1, D_SSM)),
                lspec((1, D_SSM)), pl.BlockSpec((LANES, 2 * D_SSM), lambda b: (0, 0))]
    args = [xbc, z, dt, *consts, _head_spread_matrix()]
    oc_shape = jax.ShapeDtypeStruct((t, D_SSM), bf16)
    oc_spec = pl.BlockSpec((rows, D_SSM), lambda b: (b0 + b, 0))
    nck = rows // CHUNK
    scratch = [pltpu.VMEM((rows, XBC_DIM), f32),
               pltpu.VMEM((nck, 2 * SSM_STATE, D_SSM), f32),
               pltpu.VMEM((nck, 2 * SSM_STATE, D_SSM), bf16),
               pltpu.VMEM((nck, SSM_HEADS, CHUNK, 2 * CHUNK), bf16),
               pltpu.VMEM((nck, 8, 2 * D_SSM), f32)]
    if latent:
        in_specs += [st, st, ANY]
        args += [init[0], init[1], oc]
        out_specs, out_shape = [oc_spec], [oc_shape]
        aliases = {11: 0}
        n_alias = 1
    else:
        n_alias = len(st_prev)
        in_specs += [ANY] * n_alias
        args += list(st_prev)
        st_shape = jax.ShapeDtypeStruct((batch, depth, D_SSM, SSM_STATE), f32)
        out_specs, out_shape = [oc_spec, st, st], [oc_shape, st_shape, st_shape]
        aliases = {9 + k: 1 + k for k in range(n_alias)}
    return pl.pallas_call(
        functools.partial(_ssd_kernel, latent=latent, n_alias=n_alias, nseq=nseq, n=n),
        grid=(n_b // nseq,),
        in_specs=in_specs, out_specs=out_specs, out_shape=out_shape,
        scratch_shapes=scratch,
        input_output_aliases=aliases,
        compiler_params=_params(("arbitrary",)),
        name="ssd_latent" if latent else "ssd_context",
    )(*args)


def _outproj_kernel(*refs, n_x, n_p):
    x_refs = refs[:n_x]
    (oab_ref, oc_ref, g1_ref, sh_ref, sc_ref, n2_ref, w_ref, wr_ref, br_ref,
     x1_ref, h2_ref, comb_ref) = refs[n_x:]
    i = pl.program_id(0)
    o = _dot(oab_ref[...], w_ref[0:D_AB, :]) + _dot(oc_ref[...], w_ref[D_AB:, :])
    x1 = _load_x(x_refs, i, n_p) + g1_ref[...] * o
    x1_ref[...] = x1
    h2 = _rms(x1) * n2_ref[...]
    h2 = h2 * (1.0 + sc_ref[...]) + sh_ref[...]
    h2b = h2.astype(bf16)
    h2_ref[...] = h2b

    logits = _dot(h2b, wr_ref[...]) + br_ref[...]
    lane = lax.broadcasted_iota(jnp.int32, logits.shape, 1).astype(f32)
    big = float(LANES)
    neg = -jnp.inf
    gmask = (lane >= N_EXPERTS) & (lane < N_EXPERTS + N_EGROUPS)
    gl = jnp.where(gmask, logits, neg)
    gmax = gl.max(axis=-1, keepdims=True)
    gsel = jnp.where(gl == gmax, lane, big).min(axis=-1, keepdims=True) - N_EXPERTS
    gprob = 1.0 / jnp.where(gmask, jnp.exp(logits - gmax), 0.0).sum(axis=-1, keepdims=True)
    emask = (lane >= gsel * EXPERTS_PER_GROUP) & (lane < (gsel + 1) * EXPERTS_PER_GROUP)
    el = jnp.where(emask, logits, neg)
    v1 = el.max(axis=-1, keepdims=True)
    i1 = jnp.where(el == v1, lane, big).min(axis=-1, keepdims=True)
    el2 = jnp.where(lane == i1, neg, el)
    v2 = el2.max(axis=-1, keepdims=True)
    i2 = jnp.where(el2 == v2, lane, big).min(axis=-1, keepdims=True)
    e2 = jnp.exp(v2 - v1)
    den = 1.0 + e2
    comb = jnp.where(lane == i1, gprob / den, 0.0) + jnp.where(lane == i2, gprob * e2 / den, 0.0)
    comb_ref[...] = jnp.where(lane == GSEL_LANE, gsel, comb)


def _out_projection(oab, oc, xs, mod, layer, norm_g, w_out, wr, br, dims, tm=512):
    batch, seq, dec_batch, dec_seq, depth = dims
    n_p_tok, n_s_tok = batch * seq, dec_batch * dec_seq
    t = n_p_tok + n_s_tok
    til = _Tiling(n_p_tok, n_s_tok, dec_seq, tm)

    def lspec(shape):
        return pl.BlockSpec((None,) + shape, lambda i: (layer,) + (0,) * len(shape))

    tok = lambda w: pl.BlockSpec((tm, w), lambda i: (i, 0))
    return pl.pallas_call(
        functools.partial(_outproj_kernel, n_x=len(xs), n_p=til.n_p),
        grid=(til.n,),
        in_specs=_x_specs(til, len(xs) == 2) + [
            tok(D_AB), tok(D_SSM),
            _mod_spec(til, layer, 2), _mod_spec(til, layer, 3), _mod_spec(til, layer, 4),
            lspec((1, D_MODEL)), lspec((D_AB + D_SSM, D_MODEL)), lspec((D_MODEL, LANES)), lspec((1, LANES))],
        out_specs=[tok(D_MODEL), tok(D_MODEL), tok(LANES)],
        out_shape=[jax.ShapeDtypeStruct((t, D_MODEL), f32),
                   jax.ShapeDtypeStruct((t, D_MODEL), bf16),
                   jax.ShapeDtypeStruct((t, LANES), f32)],
        compiler_params=_params(("arbitrary",)),
        name="outproj_norm_router",
    )(*xs, oab, oc, mod, mod, mod, norm_g, w_out, wr, br)


def _expert_ffn(h, weight_of, experts, wg_ref, wu_ref, wd_rows):
    hid = []
    for e in experts:
        a = _dot(h, wg_ref[e])
        u = _dot(h, wu_ref[e])
        hid.append((_silu(a) * u * weight_of(e)).astype(bf16))
    return _dot(jnp.concatenate(hid, axis=1), wd_rows)


def _moe_kernel(h2_ref, comb_ref, wg_ref, wu_ref, wd_ref, x1_ref, g2_ref, fg_ref, *rest, final, n_p):
    outs, y_scr = rest[:-1], rest[-1]
    i = pl.program_id(0)
    k = pl.program_id(1)
    sb, cap = MOE_SB, MOE_CAP
    nsb = MOE_TM // sb
    slots = N_EGROUPS * cap
    gh = EXPERTS_PER_GROUP * D_EXPERT

    @pl.when(k == 0)
    def _():
        comb = comb_ref[...]
        lane = lax.broadcasted_iota(jnp.int32, comb.shape, 1).astype(f32)
        gsel = comb[:, GSEL_LANE:GSEL_LANE + 1]
        mine = lane == gsel
        onehot = mine.astype(f32).astype(bf16)
        r_i = lax.broadcasted_iota(jnp.int32, (sb, sb), 0)
        c_i = lax.broadcasted_iota(jnp.int32, (sb, sb), 1)
        before = (c_i < r_i).astype(f32).astype(bf16)
        ranks = []
        for j in range(nsb):
            rows = slice(j * sb, (j + 1) * sb)
            earlier = _dot(before, onehot[rows])
            ranks.append(jnp.where(mine[rows], earlier, 0.0).sum(axis=-1, keepdims=True))
        rank = jnp.concatenate(ranks, axis=0)
        fits = jnp.max(rank) < cap

        @pl.when(fits)
        def _():
            hi = comb.astype(bf16)
            lo = (comb - hi.astype(f32)).astype(bf16)
            dest = gsel * cap + rank
            slot_i = lax.broadcasted_iota(jnp.int32, (sb, slots), 1).astype(f32)
            place, packed_h, packed_c = [], [], []
            for j in range(nsb):
                rows = slice(j * sb, (j + 1) * sb)
                pt = (slot_i == dest[rows]).astype(f32).astype(bf16)
                place.append(pt)
                hx = jnp.concatenate([h2_ref[rows, :], hi[rows], lo[rows]], axis=1)
                srt = _dot_tn(pt, hx)
                packed_h.append(srt[:, :D_MODEL].astype(bf16))
                packed_c.append(srt[:, D_MODEL:D_MODEL + LANES] + srt[:, D_MODEL + LANES:])
            ys = []
            for g in range(N_EGROUPS):
                seg = slice(g * cap, (g + 1) * cap)
                hs = jnp.concatenate([p[seg] for p in packed_h], axis=0)
                cw = jnp.concatenate([p[seg] for p in packed_c], axis=0)
                experts = range(g * EXPERTS_PER_GROUP, (g + 1) * EXPERTS_PER_GROUP)
                yg = _expert_ffn(hs, lambda e: cw[:, e:e + 1], experts, wg_ref, wu_ref,
                                 wd_ref[g * gh:(g + 1) * gh, :])
                ys.append(yg.astype(bf16))
            for j in range(nsb):
                ysrt = jnp.concatenate([yg[j * cap:(j + 1) * cap] for yg in ys], axis=0)
                y_scr[j * sb:(j + 1) * sb, :] = _dot(place[j], ysrt).astype(bf16)

        @pl.when(jnp.logical_not(fits))
        def _():
            def body(j, carry):
                rows = pl.ds(pl.multiple_of(j * sb, sb), sb)
                cj = comb_ref[rows, :]
                y = _expert_ffn(h2_ref[rows, :], lambda e: cj[:, e:e + 1], range(N_EXPERTS),
                                wg_ref, wu_ref, wd_ref[...])
                y_scr[rows, :] = y.astype(bf16)
                return carry

            lax.fori_loop(0, nsb, body, 0)

    @pl.when(k > 0)
    def _():
        r0 = pl.multiple_of((k - 1) * MOE_FIN, MOE_FIN)
        x2 = x1_ref[...] + g2_ref[...] * y_scr[pl.ds(r0, MOE_FIN), :].astype(f32)
        if final:
            yp_ref, ys_ref = outs
            x2 = _rms(x2) * fg_ref[...]

            @pl.when(i < n_p)
            def _():
                yp_ref[...] = x2

            @pl.when(i >= n_p)
            def _():
                ys_ref[...] = x2
        else:
            outs[0][...] = x2


def _moe(h2, comb, wg, wu, wd, x1, mod, layer, final_g, dims, final):
    batch, seq, dec_batch, dec_seq, depth = dims
    n_p_tok, n_s_tok = batch * seq, dec_batch * dec_seq
    t = n_p_tok + n_s_tok
    tm = MOE_TM
    til = _Tiling(n_p_tok, n_s_tok, dec_seq, tm)
    nfin = tm // MOE_FIN
    once = pl.Buffered(1)
    tok = lambda w: pl.BlockSpec((tm, w), lambda i, k: (i, 0))

    def fin_block(i, k):
        return i * nfin + jnp.maximum(k - 1, 0)

    fin = lambda idx: pl.BlockSpec((MOE_FIN, D_MODEL), lambda i, k: (idx(i, k), 0))
    if final:
        out_specs = [fin(lambda i, k: jnp.minimum(fin_block(i, k), til.n_p * nfin - 1)),
                     fin(lambda i, k: jnp.maximum(fin_block(i, k) - til.n_p * nfin, 0))]
        out_shape = [jax.ShapeDtypeStruct((n_p_tok, D_MODEL), f32),
                     jax.ShapeDtypeStruct((n_s_tok, D_MODEL), f32)]
    else:
        out_specs = [fin(fin_block)]
        out_shape = [jax.ShapeDtypeStruct((t, D_MODEL), f32)]
    return pl.pallas_call(
        functools.partial(_moe_kernel, final=final, n_p=til.n_p),
        grid=(til.n, 1 + nfin),
        in_specs=[tok(D_MODEL), tok(LANES),
                  pl.BlockSpec((None, N_EXPERTS, D_MODEL, D_EXPERT), lambda i, k: (layer, 0, 0, 0), once),
                  pl.BlockSpec((None, N_EXPERTS, D_MODEL, D_EXPERT), lambda i, k: (layer, 0, 0, 0), once),
                  pl.BlockSpec((None, N_EXPERTS * D_EXPERT, D_MODEL), lambda i, k: (layer, 0, 0), once),
                  fin(fin_block), _mod_spec(til, layer, 5),
                  pl.BlockSpec((1, D_MODEL), lambda i, k: (0, 0))],
        out_specs=out_specs, out_shape=out_shape,
        scratch_shapes=[pltpu.VMEM((tm, D_MODEL), bf16)],
        compiler_params=_params(("arbitrary", "arbitrary")),
        name="moe_ffn",
    )(h2, comb, wg, wu, wd.reshape(depth, N_EXPERTS * D_EXPERT, D_MODEL), x1, mod, final_g)


def _rope_tables(n_rows):
    rows = jnp.repeat(jnp.arange(n_rows), GRID_W).astype(f32)
    cols = jnp.tile(jnp.arange(GRID_W), n_rows).astype(f32)
    inv = ROPE_THETA ** (-jnp.arange(ROPE_QUARTER, dtype=f32) / ROPE_QUARTER)
    ang_r = rows[:, None] * inv
    ang_c = cols[:, None] * inv
    ang = jnp.concatenate([ang_r, ang_r, ang_c, ang_c], axis=-1)
    cos, sin = jnp.cos(ang), jnp.sin(ang)
    even = (np.arange(HEAD_DIM) // ROPE_QUARTER) % 2 == 0
    sin_even = jnp.where(even, -sin, 0.0)
    sin_odd = jnp.where(even, 0.0, sin)
    return tuple(jnp.tile(t, (1, D_Q // HEAD_DIM)) for t in (cos, sin_even, sin_odd))


def _head_sum_matrix():
    m = np.zeros((D_Q, D_Q), np.float32)
    for h in range(D_Q // HEAD_DIM):
        m[h * HEAD_DIM:(h + 1) * HEAD_DIM, h * HEAD_DIM:(h + 1) * HEAD_DIM] = 1.0
    return jnp.asarray(m, dtype=bf16)


def _head_spread_matrix():
    m = np.zeros((LANES, 2 * D_SSM), np.float32)
    for j in range(2 * SSM_HEADS):
        m[j, j * SSM_HEAD_DIM:(j + 1) * SSM_HEAD_DIM] = 1.0
    return jnp.asarray(m, dtype=bf16)


def _pad_last(v, width=LANES):
    pad = [(0, 0)] * (v.ndim - 1) + [(0, width - v.shape[-1])]
    return jnp.pad(v, pad)


@jax.jit
def kernel(x_prompt, x_sample, cache_a_k, cache_a_v, cache_b_k, cache_b_v, state_ssm_fwd, state_ssm_bwd, c, c_ctx, norm1_g, norm2_g, final_norm_g, w_ada, b_ada, w_in, a_sink, q_norm_g, k_norm_g, conv_w, conv_b, dt_bias, a_log, d_skip, ssm_norm_g, w_out, w_router_group, b_router_group, w_router_expert, b_router_expert, w_gate, w_up, w_down):
    batch, seq, _ = x_prompt.shape
    dec_batch, dec_seq, _ = x_sample.shape
    depth = w_in.shape[0]
    past = cache_a_k.shape[2]
    dims = (batch, seq, dec_batch, dec_seq, depth)
    n_p_tok = batch * seq
    n_s_tok = dec_batch * dec_seq

    cvec = jnp.concatenate([c_ctx[None, :], c, jnp.zeros((8 - 1 - dec_batch, D_MODEL), f32)], axis=0)
    mod = _modulation(cvec, w_ada, b_ada).reshape(depth, 8, 6, 1, D_MODEL)

    rope = _rope_tables(dec_seq // GRID_W)
    headmat = _head_sum_matrix()
    w_main = w_in[:, :, :C_DT].astype(bf16)
    w_dt = _pad_last(w_in[:, :, C_DT:]).astype(bf16)
    qg = jnp.tile(q_norm_g, (1, D_Q // HEAD_DIM)).reshape(depth, 1, D_Q)
    kg = jnp.tile(k_norm_g, (1, D_Q // HEAD_DIM)).reshape(depth, 1, D_Q)
    dtb = _pad_last(dt_bias.reshape(depth, 1, 2 * SSM_HEADS))
    n1 = norm1_g.reshape(depth, 1, D_MODEL)
    n2 = norm2_g.reshape(depth, 1, D_MODEL)
    ssd_consts = (conv_w, conv_b.reshape(depth, 1, XBC_DIM),
                  _pad_last(a_log.reshape(depth, 1, 2 * SSM_HEADS)),
                  jnp.repeat(d_skip, SSM_HEAD_DIM, axis=-1).reshape(depth, 1, D_SSM),
                  ssm_norm_g.reshape(depth, 1, D_SSM))
    w_out16 = w_out.astype(bf16)
    wr = _pad_last(jnp.concatenate([w_router_expert, w_router_group], axis=-1)).astype(bf16)
    br = _pad_last(jnp.concatenate([b_router_expert, b_router_group], axis=-1)).reshape(depth, 1, LANES)
    wg16, wu16, wd16 = w_gate.astype(bf16), w_up.astype(bf16), w_down.astype(bf16)
    fg = final_norm_g.reshape(1, D_MODEL)

    caches = tuple(t.reshape(dec_batch, depth, past, D_KV) for t in (cache_a_k, cache_a_v, cache_b_k, cache_b_v))
    init = (state_ssm_fwd.reshape(dec_batch, depth, D_SSM, SSM_STATE),
            state_ssm_bwd.reshape(dec_batch, depth, D_SSM, SSM_STATE))

    xs = (x_prompt.reshape(n_p_tok, D_MODEL), x_sample.reshape(n_s_tok, D_MODEL))
    kvp, states = (), ()
    for l in range(depth):
        outs = _in_projection(xs, mod, l, n1, w_main, w_dt, qg, kg, dtb, headmat, rope, kvp, dims)
        qa, qb = outs[0:2]
        kvp, kvs = tuple(outs[2:6]), tuple(outs[6:10])
        z, xbc, dt = outs[10:13]

        oab = _attention_ctx(a_sink, qa, qb, kvp, l, dims)
        oab = _attention_lat(a_sink, qa, qb, kvs, caches, oab, l, dims)

        oc, hf, hb = _ssd(xbc, z, dt, ssd_consts, l, dims, latent=False, st_prev=states)
        states = (hf, hb)
        oc, = _ssd(xbc, z, dt, ssd_consts, l, dims, latent=True, init=init, oc=oc)

        x1, h2, comb = _out_projection(oab, oc, xs, mod, l, n2, w_out16, wr, br, dims)
        xs = tuple(_moe(h2, comb, wg16, wu16, wd16, x1, mod, l, fg, dims, final=(l == depth - 1)))

    y_prompt = xs[0].reshape(batch, seq, D_MODEL)
    y_sample = xs[1].reshape(dec_batch, dec_seq, D_MODEL)
    kv_shape = (batch, depth, seq, A_KV, HEAD_DIM)
    st_shape = (batch, depth, SSM_HEADS, SSM_HEAD_DIM, SSM_STATE)
    return ((y_prompt, y_sample) + tuple(t.reshape(kv_shape) for t in kvp)
            + tuple(t.reshape(st_shape) for t in states))
```

```python
import functools

import jax
import jax.numpy as jnp
import numpy as np
from jax import lax
from jax.experimental import pallas as pl
from jax.experimental.pallas import tpu as pltpu

f32 = jnp.float32
bf16 = jnp.bfloat16
HIGHEST = lax.Precision.HIGHEST

D_MODEL = 1024
GRID_W = 64
HEAD_DIM = 64
A_HEADS = 4
A_KV = 2
WINDOW = 128
B_HEADS = 4
B_KV = 2
ROPE_THETA = 10000.0
ROPE_QUARTER = HEAD_DIM // 4
SSM_HEADS = 8
SSM_HEAD_DIM = 64
D_SSM = SSM_HEADS * SSM_HEAD_DIM
SSM_GROUPS = 2
SSM_STATE = 64
CHUNK = 128
XBC_DIM = D_SSM + 2 * SSM_GROUPS * SSM_STATE
D_AB = (A_HEADS + B_HEADS) * HEAD_DIM
D_Q = A_HEADS * HEAD_DIM
D_KV = A_KV * HEAD_DIM
N_EGROUPS = 4
EXPERTS_PER_GROUP = 4
N_EXPERTS = N_EGROUPS * EXPERTS_PER_GROUP
D_EXPERT = 256
EPS = 1e-6

LANES = 128
C_AQ, C_AK, C_AV, C_BQ, C_BK, C_BV, C_Z, C_XBC, C_DT = 0, 256, 384, 512, 768, 896, 1024, 1536, 2304

SSD_CTX_SEQS = 4

GSEL_LANE = N_EXPERTS
MOE_TM = 1024
MOE_SB = 256
MOE_CAP = 96
MOE_FIN = 256

VMEM_LIMIT = 56 * 1024 * 1024
ANY = pl.BlockSpec(memory_space=pl.ANY)
SMEM = pl.BlockSpec(memory_space=pltpu.SMEM)


def _params(sem, vmem=VMEM_LIMIT):
    return pltpu.CompilerParams(dimension_semantics=sem, vmem_limit_bytes=vmem)


def _dot(a, b, **kw):
    return jnp.dot(a, b, preferred_element_type=f32, **kw)


def _dot_nt(a, b):
    return lax.dot_general(a, b, (((1,), (1,)), ((), ())), preferred_element_type=f32)


def _dot_tn(a, b):
    return lax.dot_general(a, b, (((0,), (0,)), ((), ())), preferred_element_type=f32)


def _silu(x):
    return (0.5 * x) * (1.0 + jnp.tanh(0.5 * x))


def _softplus(x):
    return jnp.maximum(x, 0.0) + jnp.log1p(jnp.exp(-jnp.abs(x)))


def _rms(x):
    return x * lax.rsqrt(jnp.mean(x * x, axis=-1, keepdims=True) + EPS)


class _Tiling:
    def __init__(self, n_p_tok, n_s_tok, dec_seq, tm):
        self.tm = tm
        self.n_p = n_p_tok // tm
        self.n_s = n_s_tok // tm
        self.n = self.n_p + self.n_s
        self.per_seq = dec_seq // tm

    def p_idx(self, i):
        return jnp.minimum(i, self.n_p - 1)

    def s_idx(self, i):
        return jnp.maximum(i - self.n_p, 0)

    def mod_row(self, i):
        return jnp.where(i < self.n_p, 0, 1 + (i - self.n_p) // self.per_seq)

    def seq_pos(self, i):
        return jnp.where(i < self.n_p, 0, (i - self.n_p) % self.per_seq)


def _x_specs(til, split):
    tm = til.tm
    if split:
        return [pl.BlockSpec((tm, D_MODEL), lambda i, *_: (til.p_idx(i), 0)),
                pl.BlockSpec((tm, D_MODEL), lambda i, *_: (til.s_idx(i), 0))]
    return [pl.BlockSpec((tm, D_MODEL), lambda i, *_: (i, 0))]


def _load_x(refs, i, n_p):
    if len(refs) == 2:
        return jnp.where(i < n_p, refs[0][...], refs[1][...])
    return refs[0][...]


def _mod_spec(til, layer, k):
    return pl.BlockSpec((None, None, None, 1, D_MODEL), lambda i, *_: (layer, til.mod_row(i), k, 0, 0))


def _mod_kernel(c_ref, w_ref, b_ref, o_ref):
    s = _silu(c_ref[...])
    o_ref[...] = _dot(s, w_ref[...], precision=HIGHEST) + b_ref[...]


def _modulation(cvec, w_ada, b_ada):
    depth = w_ada.shape[0]
    n = w_ada.shape[2]
    tn = 1536
    return pl.pallas_call(
        _mod_kernel,
        grid=(depth, n // tn),
        in_specs=[
            pl.BlockSpec((8, D_MODEL), lambda l, j: (0, 0)),
            pl.BlockSpec((None, D_MODEL, tn), lambda l, j: (l, 0, j)),
            pl.BlockSpec((None, 1, tn), lambda l, j: (l, 0, j)),
        ],
        out_specs=pl.BlockSpec((None, 8, tn), lambda l, j: (l, 0, j)),
        out_shape=jax.ShapeDtypeStruct((depth, 8, n), f32),
        compiler_params=_params(("arbitrary", "arbitrary")),
        name="adaln_mod",
    )(cvec, w_ada, b_ada.reshape(depth, 1, n))


def _rope(x, cos, sin_even, sin_odd):
    w = x.shape[-1]
    nxt = pltpu.roll(x, w - ROPE_QUARTER, 1)
    prv = pltpu.roll(x, ROPE_QUARTER, 1)
    return x * cos + nxt * sin_even + prv * sin_odd


def _inproj_kernel(*refs, n_x, n_alias, n_p, seqs_per_tile):
    x_refs = refs[:n_x]
    (sh_ref, sc_ref, g_ref, w_ref, wdt_ref, qg_ref, kg_ref, dtb_ref, hm_ref,
     cos_ref, se_ref, so_ref) = refs[n_x:n_x + 12]
    (qa_ref, qb_ref, akp_ref, avp_ref, bkp_ref, bvp_ref, aks_ref, avs_ref, bks_ref, bvs_ref,
     z_ref, xbc_ref, dt_ref) = refs[n_x + 12 + n_alias:]
    i = pl.program_id(0)
    h = _rms(_load_x(x_refs, i, n_p)) * g_ref[...]
    h = h * (1.0 + sc_ref[...]) + sh_ref[...]
    hb = h.astype(bf16)

    def proj(lo, hi):
        return _dot(hb, w_ref[:, lo:hi])

    def head_norm(t, gain):
        w = t.shape[-1]
        sq = t * t
        hi = sq.astype(bf16)
        lo = (sq - hi.astype(f32)).astype(bf16)
        ms_h = (_dot(hi, hm_ref[0:w, 0:w]) + _dot(lo, hm_ref[0:w, 0:w])) * (1.0 / HEAD_DIM)
        return t * lax.rsqrt(ms_h + EPS) * gain

    qa = proj(C_AQ, C_AK)
    ka = proj(C_AK, C_AV)
    va = proj(C_AV, C_BQ)
    qb = head_norm(proj(C_BQ, C_BK), qg_ref[...])
    kb = head_norm(proj(C_BK, C_BV), kg_ref[:, 0:D_KV])
    vb = proj(C_BV, C_Z)
    z_ref[...] = proj(C_Z, C_XBC).astype(bf16)
    xbc_ref[...] = proj(C_XBC, C_DT).astype(bf16)
    dt_ref[...] = _softplus(_dot(hb, wdt_ref[...]) + dtb_ref[...])

    lat = i >= n_p
    cos = jnp.where(lat, cos_ref[...], 1.0)
    se = jnp.where(lat, se_ref[...], 0.0)
    so = jnp.where(lat, so_ref[...], 0.0)
    qa_ref[...] = _rope(qa, cos, se, so).astype(bf16)
    qb_ref[...] = _rope(qb, cos, se, so).astype(bf16)
    aks_ref[...] = _rope(ka, cos[:, :D_KV], se[:, :D_KV], so[:, :D_KV]).astype(bf16)
    bks_ref[...] = _rope(kb, cos[:, :D_KV], se[:, :D_KV], so[:, :D_KV]).astype(bf16)
    avs_ref[...] = va.astype(bf16)
    bvs_ref[...] = vb.astype(bf16)

    @pl.when(i < n_p)
    def _():
        shp = (seqs_per_tile, -1, D_KV)
        akp_ref[...] = ka.reshape(shp)
        avp_ref[...] = va.reshape(shp)
        bkp_ref[...] = kb.reshape(shp)
        bvp_ref[...] = vb.reshape(shp)


def _in_projection(xs, mod, layer, norm_g, w_main, w_dt, qg, kg, dtb, headmat, rope, kv_prev, dims, tm=512):
    batch, seq, dec_batch, dec_seq, depth = dims
    n_p_tok, n_s_tok = batch * seq, dec_batch * dec_seq
    t = n_p_tok + n_s_tok
    til = _Tiling(n_p_tok, n_s_tok, dec_seq, tm)
    spt = tm // seq
    cos, se, so = rope
    n_alias = len(kv_prev)

    def c2(shape):
        return pl.BlockSpec(shape, lambda i: (0, 0))

    def lspec(shape):
        return pl.BlockSpec((None,) + shape, lambda i: (layer,) + (0,) * len(shape))

    rope_spec = pl.BlockSpec((tm, D_Q), lambda i: (til.seq_pos(i), 0))
    tok = lambda w: pl.BlockSpec((tm, w), lambda i: (i, 0))
    kvp = pl.BlockSpec((spt, None, seq, D_KV), lambda i: (til.p_idx(i), layer, 0, 0))
    kvs = pl.BlockSpec((tm, D_KV), lambda i: (til.s_idx(i), 0))
    kvp_shape = jax.ShapeDtypeStruct((batch, depth, seq, D_KV), f32)
    kvs_shape = jax.ShapeDtypeStruct((n_s_tok, D_KV), bf16)
    n_in = len(xs) + 12
    return pl.pallas_call(
        functools.partial(_inproj_kernel, n_x=len(xs), n_alias=n_alias, n_p=til.n_p, seqs_per_tile=spt),
        grid=(til.n,),
        in_specs=_x_specs(til, len(xs) == 2) + [
            _mod_spec(til, layer, 0), _mod_spec(til, layer, 1),
            lspec((1, D_MODEL)), lspec((D_MODEL, C_DT)), lspec((D_MODEL, LANES)),
            lspec((1, D_Q)), lspec((1, D_Q)), lspec((1, LANES)), c2((D_Q, D_Q)),
            rope_spec, rope_spec, rope_spec] + [ANY] * n_alias,
        out_specs=[tok(D_Q), tok(D_Q), kvp, kvp, kvp, kvp, kvs, kvs, kvs, kvs,
                   tok(D_SSM), tok(XBC_DIM), tok(LANES)],
        out_shape=[jax.ShapeDtypeStruct((t, D_Q), bf16), jax.ShapeDtypeStruct((t, D_Q), bf16),
                   kvp_shape, kvp_shape, kvp_shape, kvp_shape,
                   kvs_shape, kvs_shape, kvs_shape, kvs_shape,
                   jax.ShapeDtypeStruct((t, D_SSM), bf16), jax.ShapeDtypeStruct((t, XBC_DIM), bf16),
                   jax.ShapeDtypeStruct((t, LANES), f32)],
        input_output_aliases={n_in + k: 2 + k for k in range(n_alias)},
        compiler_params=_params(("arbitrary",)),
        name="norm_mod_inproj",
    )(*xs, mod, mod, norm_g, w_main, w_dt, qg, kg, dtb, headmat, cos, se, so, *kv_prev)


def _softmax_pv(scores, values, sink):
    m = scores[0].max(axis=-1, keepdims=True)
    for s in scores[1:]:
        m = jnp.maximum(m, s.max(axis=-1, keepdims=True))
    if sink is not None:
        m = jnp.maximum(m, sink)
    den = None
    acc = None
    for s, v in zip(scores, values):
        p = jnp.exp(s - m)
        d = p.sum(axis=-1, keepdims=True)
        o = _dot(p.astype(bf16), v)
        den = d if den is None else den + d
        acc = o if acc is None else acc + o
    if sink is not None:
        den = den + jnp.exp(sink - m)
    return acc / den


def _attn_ctx_kernel(sink_ref, qa_ref, ka_ref, va_ref, qb_ref, kb_ref, vb_ref, o_ref, *, layer):
    scale = HEAD_DIM ** -0.5
    for mixer, (q_ref, k_ref, v_ref) in enumerate(((qa_ref, ka_ref, va_ref), (qb_ref, kb_ref, vb_ref))):
        for kv in range(A_KV):
            ks = slice(kv * HEAD_DIM, (kv + 1) * HEAD_DIM)
            k = k_ref[:, ks].astype(bf16)
            v = v_ref[:, ks].astype(bf16)
            for g in range(A_HEADS // A_KV):
                hd = kv * 2 + g
                hs = slice(hd * HEAD_DIM, (hd + 1) * HEAD_DIM)
                q = q_ref[:, hs] * scale
                s = _dot_nt(q, k)
                sink = sink_ref[layer, hd] if mixer == 0 else None
                o = _softmax_pv([s], [v], sink)
                os_ = slice(mixer * D_Q + hd * HEAD_DIM, mixer * D_Q + (hd + 1) * HEAD_DIM)
                o_ref[:, os_] = o.astype(bf16)


def _attention_ctx(sink, qa, qb, kvp, layer, dims):
    batch, seq, dec_batch, dec_seq, depth = dims
    t = batch * seq + dec_batch * dec_seq
    qspec = pl.BlockSpec((seq, D_Q), lambda b: (b, 0))
    kspec = pl.BlockSpec((None, None, seq, D_KV), lambda b: (b, layer, 0, 0))
    akp, avp, bkp, bvp = kvp
    return pl.pallas_call(
        functools.partial(_attn_ctx_kernel, layer=layer),
        grid=(batch,),
        in_specs=[SMEM, qspec, kspec, kspec, qspec, kspec, kspec],
        out_specs=pl.BlockSpec((seq, D_AB), lambda b: (b, 0)),
        out_shape=jax.ShapeDtypeStruct((t, D_AB), bf16),
        compiler_params=_params(("arbitrary",)),
        name="attn_context",
    )(sink, qa, akp, avp, qb, bkp, bvp)


def _attn_lat_kernel(sink_ref, qa_ref, ka_ref, va_ref, cka_ref, cva_ref,
                     qb_ref, kb_ref, vb_ref, ckb_ref, cvb_ref, alias_ref, o_ref, *, seq, layer):
    del alias_ref
    j = pl.program_id(1)
    scale = HEAD_DIM ** -0.5
    w = WINDOW
    start = pl.multiple_of(jnp.clip((j - 1) * w, 0, seq - 3 * w), w)
    qi = j * w + lax.broadcasted_iota(jnp.int32, (w, 3 * w), 0)
    ki = start + lax.broadcasted_iota(jnp.int32, (w, 3 * w), 1)
    valid = jnp.abs(ki - qi) <= w
    for kv in range(A_KV):
        ks = slice(kv * HEAD_DIM, (kv + 1) * HEAD_DIM)
        ka = ka_ref[pl.ds(start, 3 * w), ks]
        va = va_ref[pl.ds(start, 3 * w), ks]
        cka = cka_ref[:, ks].astype(bf16)
        cva = cva_ref[:, ks].astype(bf16)
        kb = kb_ref[:, ks]
        vb = vb_ref[:, ks]
        ckb = ckb_ref[:, ks].astype(bf16)
        cvb = cvb_ref[:, ks].astype(bf16)
        for g in range(A_HEADS // A_KV):
            hd = kv * 2 + g
            hs = slice(hd * HEAD_DIM, (hd + 1) * HEAD_DIM)
            q = qa_ref[:, hs] * scale
            s_loc = jnp.where(valid, _dot_nt(q, ka), -jnp.inf)
            s_ctx = _dot_nt(q, cka)
            o_ref[:, hs] = _softmax_pv([s_loc, s_ctx], [va, cva], sink_ref[layer, hd]).astype(bf16)
            q = qb_ref[:, hs] * scale
            o = _softmax_pv([_dot_nt(q, kb), _dot_nt(q, ckb)], [vb, cvb], None)
            o_ref[:, slice(D_Q + hd * HEAD_DIM, D_Q + (hd + 1) * HEAD_DIM)] = o.astype(bf16)


def _attention_lat(sink, qa, qb, kvs, caches, oab, layer, dims):
    batch, seq, dec_batch, dec_seq, depth = dims
    w = WINDOW
    nq = dec_seq // w
    q0 = batch * seq // w
    past = caches[0].shape[2]
    qspec = pl.BlockSpec((w, D_Q), lambda b, j: (q0 + b * nq + j, 0))
    kspec = pl.BlockSpec((dec_seq, D_KV), lambda b, j: (b, 0))
    cspec = pl.BlockSpec((None, None, past, D_KV), lambda b, j: (b, layer, 0, 0))
    aks, avs, bks, bvs = kvs
    cka, cva, ckb, cvb = caches
    return pl.pallas_call(
        functools.partial(_attn_lat_kernel, seq=dec_seq, layer=layer),
        grid=(dec_batch, nq),
        in_specs=[SMEM, qspec, kspec, kspec, cspec, cspec, qspec, kspec, kspec, cspec, cspec, ANY],
        out_specs=pl.BlockSpec((w, D_AB), lambda b, j: (q0 + b * nq + j, 0)),
        out_shape=jax.ShapeDtypeStruct(oab.shape, oab.dtype),
        input_output_aliases={11: 0},
        compiler_params=_params(("arbitrary", "arbitrary")),
        name="attn_latent",
    )(sink, qa, aks, avs, cka, cva, qb, bks, bvs, ckb, cvb, oab)


def _ssd_kernel(*refs, latent, n_alias, nseq, n):
    xbc_ref, z_ref, dt_ref, cw_ref, cb_ref, alog_ref, dskip_ref, g_ref, sel_ref = refs[:9]
    if latent:
        h0f_ref, h0b_ref = refs[9:11]
        o_ref = refs[11 + n_alias]
    else:
        o_ref, hf_ref, hb_ref = refs[9 + n_alias:12 + n_alias]
    xc_scr, st_scr, he_scr, lhs_scr, dec_scr = refs[12 + n_alias:]
    nc = n // CHUNK
    L = CHUNK
    ns = SSM_STATE
    nh = SSM_HEADS
    hpg = SSM_HEADS // SSM_GROUPS
    hd_w = SSM_HEAD_DIM

    x = xbc_ref[...].astype(f32)
    t_idx = lax.broadcasted_iota(jnp.int32, x.shape, 0) % n
    prv = jnp.where(t_idx == 0, 0.0, pltpu.roll(x, 1, 0))
    nxt = jnp.where(t_idx == n - 1, 0.0, pltpu.roll(x, nseq * n - 1, 0))
    y = prv * cw_ref[0:1, :] + x * cw_ref[1:2, :] + nxt * cw_ref[2:3, :] + cb_ref[...]
    xc_scr[...] = _silu(y)

    a_neg = -jnp.exp(alog_ref[...])
    r_i = lax.broadcasted_iota(jnp.int32, (L, L), 0)
    c_i = lax.broadcasted_iota(jnp.int32, (L, L), 1)
    lower = r_i >= c_i
    upper = r_i <= c_i
    tril = lower.astype(f32).astype(bf16)

    def split3(v):
        v0 = v.astype(bf16)
        r1 = v - v0.astype(f32)
        v1 = r1.astype(bf16)
        return v0, v1, (r1 - v1.astype(f32)).astype(bf16)

    def prefix_sum(v):
        return sum(_dot(tril, t) for t in split3(v))

    def spread(v):
        return sum(_dot(t, sel_ref[...]) for t in split3(v))

    def stage1(c, carry):
        rows = pl.ds(pl.multiple_of(c * L, L), L)
        xs16 = xc_scr[rows, 0:D_SSM].astype(bf16)
        bm = xc_scr[rows, D_SSM:D_SSM + SSM_GROUPS * ns]
        cm = xc_scr[rows, D_SSM + SSM_GROUPS * ns:XBC_DIM]
        dt = dt_ref[rows, :]
        da = dt * a_neg
        cs = prefix_sum(da)
        tot = cs[L - 1:L, :]
        suf = tot - cs + da
        cs_t, suf_t, dt_t, b_t = cs.T, suf.T, dt.T, bm.T
        tot_c = cs_t[:, L - 1:L]
        wf_t = jnp.exp(tot_c[0:nh] - cs_t[0:nh]) * dt_t[0:nh]
        wb_t = jnp.exp(tot_c[nh:2 * nh] - suf_t[nh:2 * nh]) * dt_t[nh:2 * nh]
        dec_scr[c] = spread(jnp.broadcast_to(jnp.exp(tot), (8, LANES)))
        cm16 = cm.astype(bf16)
        bm16 = bm.astype(bf16)
        for g in range(SSM_GROUPS):
            gs = slice(g * ns, (g + 1) * ns)
            cb = _dot_nt(cm16[:, gs], bm16[:, gs])
            cg = cm[:, gs]
            bg_t = b_t[gs, :]
            for hh in range(hpg):
                hd = g * hpg + hh
                hb_ = nh + hd
                cols = slice(hd * hd_w, (hd + 1) * hd_w)
                lhs1 = jnp.concatenate([bg_t * wf_t[hd:hd + 1, :], bg_t * wb_t[hd:hd + 1, :]], axis=0)
                st_scr[c, :, cols] = _dot(lhs1.astype(bf16), xs16[:, cols])
                csb = jnp.broadcast_to(cs[:, hd:hd + 1], (L, L))
                sfb = jnp.broadcast_to(suf[:, hb_:hb_ + 1], (L, L))
                lf = jnp.exp(jnp.where(lower, csb - cs_t[hd:hd + 1, :], -jnp.inf))
                lb = jnp.exp(jnp.where(upper, sfb - suf_t[hb_:hb_ + 1, :], -jnp.inf))
                m = cb * (lf * dt_t[hd:hd + 1, :] + lb * dt_t[hb_:hb_ + 1, :])
                lhs_scr[c, hd] = jnp.concatenate(
                    [m, cg * jnp.exp(csb[:, 0:ns]), cg * jnp.exp(sfb[:, 0:ns])], axis=1).astype(bf16)
        return carry

    lax.fori_loop(0, nseq * nc, stage1, 0)

    def to_t(h):
        return jnp.concatenate([h, jnp.zeros_like(h)], axis=1).T[0:ns, :]

    def from_t(ht):
        return jnp.concatenate([ht, jnp.zeros_like(ht)], axis=0).T[:, 0:ns]

    for s in range(nseq):
        if latent:
            hf, hb = to_t(h0f_ref[s]), to_t(h0b_ref[s])
        else:
            hf = hb = jnp.zeros((ns, D_SSM), f32)
        for k in range(nc):
            cf = s * nc + k
            cr = s * nc + nc - 1 - k
            he_scr[cf, 0:ns, :] = hf.astype(bf16)
            hf = hf * dec_scr[cf, 0:1, 0:D_SSM] + st_scr[cf, 0:ns, :]
            he_scr[cr, ns:2 * ns, :] = hb.astype(bf16)
            hb = hb * dec_scr[cr, 0:1, D_SSM:2 * D_SSM] + st_scr[cr, ns:2 * ns, :]
        if not latent:
            hf_ref[s] = from_t(hf)
            hb_ref[s] = from_t(hb)

    def stage3(c, carry):
        rows = pl.ds(pl.multiple_of(c * L, L), L)
        xs = xc_scr[rows, 0:D_SSM]
        xs16 = xs.astype(bf16)
        ys = []
        for hd in range(nh):
            cols = slice(hd * hd_w, (hd + 1) * hd_w)
            rhs = jnp.concatenate([xs16[:, cols], he_scr[c, 0:ns, cols], he_scr[c, ns:2 * ns, cols]], axis=0)
            ys.append(_dot(lhs_scr[c, hd], rhs))
        yv = jnp.concatenate(ys, axis=1) + xs * dskip_ref[...]
        yv = yv * _silu(z_ref[rows, :].astype(f32))
        o_ref[rows, :] = (_rms(yv) * g_ref[...]).astype(bf16)
        return carry

    lax.fori_loop(0, nseq * nc, stage3, 0, unroll=2)


def _ssd(xbc, z, dt, consts, layer, dims, *, latent, init=None, oc=None, st_prev=()):
    batch, seq, dec_batch, dec_seq, depth = dims
    t = batch * seq + dec_batch * dec_seq
    if latent:
        n_b, n, nseq = dec_batch, dec_seq, 1
    else:
        n_b, n, nseq = batch, seq, SSD_CTX_SEQS
    rows = nseq * n
    b0 = batch * seq // rows if latent else 0

    def tok(width):
        return pl.BlockSpec((rows, width), lambda b: (b0 + b, 0))

    def lspec(shape):
        return pl.BlockSpec((None,) + shape, lambda b: (layer,) + (0,) * len(shape))

    st = pl.BlockSpec((nseq, None, D_SSM, SSM_STATE), lambda b: (b, layer, 0, 0))
    in_specs = [tok(XBC_DIM), tok(D_SSM), tok(LANES),
                lspec((3, XBC_DIM)), lspec((1, XBC_DIM)), lspec((1, LANES)), lspec((1, D_SSM)),
                lspec((1, D_SSM)), pl.BlockSpec((LANES, 2 * D_SSM), lambda b: (0, 0))]
    args = [xbc, z, dt, *consts, _head_spread_matrix()]
    oc_shape = jax.ShapeDtypeStruct((t, D_SSM), bf16)
    oc_spec = pl.BlockSpec((rows, D_SSM), lambda b: (b0 + b, 0))
    nck = rows // CHUNK
    scratch = [pltpu.VMEM((rows, XBC_DIM), f32),
               pltpu.VMEM((nck, 2 * SSM_STATE, D_SSM), f32),
               pltpu.VMEM((nck, 2 * SSM_STATE, D_SSM), bf16),
               pltpu.VMEM((nck, SSM_HEADS, CHUNK, 2 * CHUNK), bf16),
               pltpu.VMEM((nck, 8, 2 * D_SSM), f32)]
    if latent:
        in_specs += [st, st, ANY]
        args += [init[0], init[1], oc]
        out_specs, out_shape = [oc_spec], [oc_shape]
        aliases = {11: 0}
        n_alias = 1
    else:
        n_alias = len(st_prev)
        in_specs += [ANY] * n_alias
        args += list(st_prev)
        st_shape = jax.ShapeDtypeStruct((batch, depth, D_SSM, SSM_STATE), f32)
        out_specs, out_shape = [oc_spec, st, st], [oc_shape, st_shape, st_shape]
        aliases = {9 + k: 1 + k for k in range(n_alias)}
    return pl.pallas_call(
        functools.partial(_ssd_kernel, latent=latent, n_alias=n_alias, nseq=nseq, n=n),
        grid=(n_b // nseq,),
        in_specs=in_specs, out_specs=out_specs, out_shape=out_shape,
        scratch_shapes=scratch,
        input_output_aliases=aliases,
        compiler_params=_params(("arbitrary",)),
        name="ssd_latent" if latent else "ssd_context",
    )(*args)


def _outproj_kernel(*refs, n_x, n_p):
    x_refs = refs[:n_x]
    (oab_ref, oc_ref, g1_ref, sh_ref, sc_ref, n2_ref, w_ref, wr_ref, br_ref,
     x1_ref, h2_ref, comb_ref) = refs[n_x:]
    i = pl.program_id(0)
    o = _dot(oab_ref[...], w_ref[0:D_AB, :]) + _dot(oc_ref[...], w_ref[D_AB:, :])
    x1 = _load_x(x_refs, i, n_p) + g1_ref[...] * o
    x1_ref[...] = x1
    h2 = _rms(x1) * n2_ref[...]
    h2 = h2 * (1.0 + sc_ref[...]) + sh_ref[...]
    h2b = h2.astype(bf16)
    h2_ref[...] = h2b

    logits = _dot(h2b, wr_ref[...]) + br_ref[...]
    lane = lax.broadcasted_iota(jnp.int32, logits.shape, 1).astype(f32)
    big = float(LANES)
    neg = -jnp.inf
    gmask = (lane >= N_EXPERTS) & (lane < N_EXPERTS + N_EGROUPS)
    gl = jnp.where(gmask, logits, neg)
    gmax = gl.max(axis=-1, keepdims=True)
    gsel = jnp.where(gl == gmax, lane, big).min(axis=-1, keepdims=True) - N_EXPERTS
    gprob = 1.0 / jnp.where(gmask, jnp.exp(logits - gmax), 0.0).sum(axis=-1, keepdims=True)
    emask = (lane >= gsel * EXPERTS_PER_GROUP) & (lane < (gsel + 1) * EXPERTS_PER_GROUP)
    el = jnp.where(emask, logits, neg)
    v1 = el.max(axis=-1, keepdims=True)
    i1 = jnp.where(el == v1, lane, big).min(axis=-1, keepdims=True)
    el2 = jnp.where(lane == i1, neg, el)
    v2 = el2.max(axis=-1, keepdims=True)
    i2 = jnp.where(el2 == v2, lane, big).min(axis=-1, keepdims=True)
    e2 = jnp.exp(v2 - v1)
    den = 1.0 + e2
    comb = jnp.where(lane == i1, gprob / den, 0.0) + jnp.where(lane == i2, gprob * e2 / den, 0.0)
    comb_ref[...] = jnp.where(lane == GSEL_LANE, gsel, comb)


def _out_projection(oab, oc, xs, mod, layer, norm_g, w_out, wr, br, dims, tm=512):
    batch, seq, dec_batch, dec_seq, depth = dims
    n_p_tok, n_s_tok = batch * seq, dec_batch * dec_seq
    t = n_p_tok + n_s_tok
    til = _Tiling(n_p_tok, n_s_tok, dec_seq, tm)

    def lspec(shape):
        return pl.BlockSpec((None,) + shape, lambda i: (layer,) + (0,) * len(shape))

    tok = lambda w: pl.BlockSpec((tm, w), lambda i: (i, 0))
    return pl.pallas_call(
        functools.partial(_outproj_kernel, n_x=len(xs), n_p=til.n_p),
        grid=(til.n,),
        in_specs=_x_specs(til, len(xs) == 2) + [
            tok(D_AB), tok(D_SSM),
            _mod_spec(til, layer, 2), _mod_spec(til, layer, 3), _mod_spec(til, layer, 4),
            lspec((1, D_MODEL)), lspec((D_AB + D_SSM, D_MODEL)), lspec((D_MODEL, LANES)), lspec((1, LANES))],
        out_specs=[tok(D_MODEL), tok(D_MODEL), tok(LANES)],
        out_shape=[jax.ShapeDtypeStruct((t, D_MODEL), f32),
                   jax.ShapeDtypeStruct((t, D_MODEL), bf16),
                   jax.ShapeDtypeStruct((t, LANES), f32)],
        compiler_params=_params(("arbitrary",)),
        name="outproj_norm_router",
    )(*xs, oab, oc, mod, mod, mod, norm_g, w_out, wr, br)


def _expert_ffn(h, weight_of, experts, wg_ref, wu_ref, wd_rows):
    hid = []
    for e in experts:
        a = _dot(h, wg_ref[e])
        u = _dot(h, wu_ref[e])
        hid.append((_silu(a) * u * weight_of(e)).astype(bf16))
    return _dot(jnp.concatenate(hid, axis=1), wd_rows)


def _moe_kernel(h2_ref, comb_ref, wg_ref, wu_ref, wd_ref, x1_ref, g2_ref, fg_ref, *rest, final, n_p):
    outs, y_scr = rest[:-1], rest[-1]
    i = pl.program_id(0)
    k = pl.program_id(1)
    sb, cap = MOE_SB, MOE_CAP
    nsb = MOE_TM // sb
    slots = N_EGROUPS * cap
    gh = EXPERTS_PER_GROUP * D_EXPERT

    @pl.when(k == 0)
    def _():
        comb = comb_ref[...]
        lane = lax.broadcasted_iota(jnp.int32, comb.shape, 1).astype(f32)
        gsel = comb[:, GSEL_LANE:GSEL_LANE + 1]
        mine = lane == gsel
        onehot = mine.astype(f32).astype(bf16)
        r_i = lax.broadcasted_iota(jnp.int32, (sb, sb), 0)
        c_i = lax.broadcasted_iota(jnp.int32, (sb, sb), 1)
        before = (c_i < r_i).astype(f32).astype(bf16)
        ranks = []
        worst = None
        for j in range(nsb):
            rows = slice(j * sb, (j + 1) * sb)
            earlier = _dot(before, onehot[rows])
            own = jnp.where(mine[rows], earlier, 0.0)
            worst = own if worst is None else jnp.maximum(worst, own)
            ranks.append(own.sum(axis=-1, keepdims=True))
        rank = jnp.concatenate(ranks, axis=0)
        fits = jnp.max(worst) < cap

        @pl.when(fits)
        def _():
            hi = comb.astype(bf16)
            lo = (comb - hi.astype(f32)).astype(bf16)
            dest = gsel * cap + rank
            slot_i = lax.broadcasted_iota(jnp.int32, (sb, slots), 1).astype(f32)
            place, packed_h, packed_c = [], [], []
            for j in range(nsb):
                rows = slice(j * sb, (j + 1) * sb)
                pt = (slot_i == dest[rows]).astype(f32).astype(bf16)
                place.append(pt)
                hx = jnp.concatenate([h2_ref[rows, :], hi[rows], lo[rows]], axis=1)
                srt = _dot_tn(pt, hx)
                packed_h.append(srt[:, :D_MODEL].astype(bf16))
                packed_c.append(srt[:, D_MODEL:D_MODEL + LANES] + srt[:, D_MODEL + LANES:])
            ys = []
            for g in range(N_EGROUPS):
                seg = slice(g * cap, (g + 1) * cap)
                hs = jnp.concatenate([p[seg] for p in packed_h], axis=0)
                cw = jnp.concatenate([p[seg] for p in packed_c], axis=0)
                experts = range(g * EXPERTS_PER_GROUP, (g + 1) * EXPERTS_PER_GROUP)
                yg = _expert_ffn(hs, lambda e: cw[:, e:e + 1], experts, wg_ref, wu_ref,
                                 wd_ref[g * gh:(g + 1) * gh, :])
                ys.append(yg.astype(bf16))
            for j in range(nsb):
                ysrt = jnp.concatenate([yg[j * cap:(j + 1) * cap] for yg in ys], axis=0)
                y_scr[j * sb:(j + 1) * sb, :] = _dot(place[j], ysrt).astype(bf16)

        @pl.when(jnp.logical_not(fits))
        def _():
            def body(j, carry):
                rows = pl.ds(pl.multiple_of(j * sb, sb), sb)
                cj = comb_ref[rows, :]
                y = _expert_ffn(h2_ref[rows, :], lambda e: cj[:, e:e + 1], range(N_EXPERTS),
                                wg_ref, wu_ref, wd_ref[...])
                y_scr[rows, :] = y.astype(bf16)
                return carry

            lax.fori_loop(0, nsb, body, 0)

    @pl.when(k > 0)
    def _():
        r0 = pl.multiple_of((k - 1) * MOE_FIN, MOE_FIN)
        x2 = x1_ref[...] + g2_ref[...] * y_scr[pl.ds(r0, MOE_FIN), :].astype(f32)
        if final:
            yp_ref, ys_ref = outs
            x2 = _rms(x2) * fg_ref[...]

            @pl.when(i < n_p)
            def _():
                yp_ref[...] = x2

            @pl.when(i >= n_p)
            def _():
                ys_ref[...] = x2
        else:
            outs[0][...] = x2


def _moe(h2, comb, wg, wu, wd, x1, mod, layer, final_g, dims, final):
    batch, seq, dec_batch, dec_seq, depth = dims
    n_p_tok, n_s_tok = batch * seq, dec_batch * dec_seq
    t = n_p_tok + n_s_tok
    tm = MOE_TM
    til = _Tiling(n_p_tok, n_s_tok, dec_seq, tm)
    nfin = tm // MOE_FIN
    once = pl.Buffered(1)
    tok = lambda w: pl.BlockSpec((tm, w), lambda i, k: (i, 0))

    def fin_block(i, k):
        return i * nfin + jnp.maximum(k - 1, 0)

    fin = lambda idx: pl.BlockSpec((MOE_FIN, D_MODEL), lambda i, k: (idx(i, k), 0))
    if final:
        out_specs = [fin(lambda i, k: jnp.minimum(fin_block(i, k), til.n_p * nfin - 1)),
                     fin(lambda i, k: jnp.maximum(fin_block(i, k) - til.n_p * nfin, 0))]
        out_shape = [jax.ShapeDtypeStruct((n_p_tok, D_MODEL), f32),
                     jax.ShapeDtypeStruct((n_s_tok, D_MODEL), f32)]
    else:
        out_specs = [fin(fin_block)]
        out_shape = [jax.ShapeDtypeStruct((t, D_MODEL), f32)]
    return pl.pallas_call(
        functools.partial(_moe_kernel, final=final, n_p=til.n_p),
        grid=(til.n, 1 + nfin),
        in_specs=[tok(D_MODEL), tok(LANES),
                  pl.BlockSpec((None, N_EXPERTS, D_MODEL, D_EXPERT), lambda i, k: (layer, 0, 0, 0), once),
                  pl.BlockSpec((None, N_EXPERTS, D_MODEL, D_EXPERT), lambda i, k: (layer, 0, 0, 0), once),
                  pl.BlockSpec((None, N_EXPERTS * D_EXPERT, D_MODEL), lambda i, k: (layer, 0, 0), once),
                  fin(fin_block), _mod_spec(til, layer, 5),
                  pl.BlockSpec((1, D_MODEL), lambda i, k: (0, 0))],
        out_specs=out_specs, out_shape=out_shape,
        scratch_shapes=[pltpu.VMEM((tm, D_MODEL), bf16)],
        compiler_params=_params(("arbitrary", "arbitrary")),
        name="moe_ffn",
    )(h2, comb, wg, wu, wd.reshape(depth, N_EXPERTS * D_EXPERT, D_MODEL), x1, mod, final_g)


def _rope_tables(n_rows):
    rows = jnp.repeat(jnp.arange(n_rows), GRID_W).astype(f32)
    cols = jnp.tile(jnp.arange(GRID_W), n_rows).astype(f32)
    inv = ROPE_THETA ** (-jnp.arange(ROPE_QUARTER, dtype=f32) / ROPE_QUARTER)
    ang_r = rows[:, None] * inv
    ang_c = cols[:, None] * inv
    ang = jnp.concatenate([ang_r, ang_r, ang_c, ang_c], axis=-1)
    cos, sin = jnp.cos(ang), jnp.sin(ang)
    even = (np.arange(HEAD_DIM) // ROPE_QUARTER) % 2 == 0
    sin_even = jnp.where(even, -sin, 0.0)
    sin_odd = jnp.where(even, 0.0, sin)
    return tuple(jnp.tile(t, (1, D_Q // HEAD_DIM)) for t in (cos, sin_even, sin_odd))


def _head_sum_matrix():
    m = np.zeros((D_Q, D_Q), np.float32)
    for h in range(D_Q // HEAD_DIM):
        m[h * HEAD_DIM:(h + 1) * HEAD_DIM, h * HEAD_DIM:(h + 1) * HEAD_DIM] = 1.0
    return jnp.asarray(m, dtype=bf16)


def _head_spread_matrix():
    m = np.zeros((LANES, 2 * D_SSM), np.float32)
    for j in range(2 * SSM_HEADS):
        m[j, j * SSM_HEAD_DIM:(j + 1) * SSM_HEAD_DIM] = 1.0
    return jnp.asarray(m, dtype=bf16)


def _pad_last(v, width=LANES):
    pad = [(0, 0)] * (v.ndim - 1) + [(0, width - v.shape[-1])]
    return jnp.pad(v, pad)


@jax.jit
def kernel(x_prompt, x_sample, cache_a_k, cache_a_v, cache_b_k, cache_b_v, state_ssm_fwd, state_ssm_bwd, c, c_ctx, norm1_g, norm2_g, final_norm_g, w_ada, b_ada, w_in, a_sink, q_norm_g, k_norm_g, conv_w, conv_b, dt_bias, a_log, d_skip, ssm_norm_g, w_out, w_router_group, b_router_group, w_router_expert, b_router_expert, w_gate, w_up, w_down):
    batch, seq, _ = x_prompt.shape
    dec_batch, dec_seq, _ = x_sample.shape
    depth = w_in.shape[0]
    past = cache_a_k.shape[2]
    dims = (batch, seq, dec_batch, dec_seq, depth)
    n_p_tok = batch * seq
    n_s_tok = dec_batch * dec_seq

    cvec = jnp.concatenate([c_ctx[None, :], c, jnp.zeros((8 - 1 - dec_batch, D_MODEL), f32)], axis=0)
    mod = _modulation(cvec, w_ada, b_ada).reshape(depth, 8, 6, 1, D_MODEL)

    rope = _rope_tables(dec_seq // GRID_W)
    headmat = _head_sum_matrix()
    w_main = w_in[:, :, :C_DT].astype(bf16)
    w_dt = _pad_last(w_in[:, :, C_DT:]).astype(bf16)
    qg = jnp.tile(q_norm_g, (1, D_Q // HEAD_DIM)).reshape(depth, 1, D_Q)
    kg = jnp.tile(k_norm_g, (1, D_Q // HEAD_DIM)).reshape(depth, 1, D_Q)
    dtb = _pad_last(dt_bias.reshape(depth, 1, 2 * SSM_HEADS))
    n1 = norm1_g.reshape(depth, 1, D_MODEL)
    n2 = norm2_g.reshape(depth, 1, D_MODEL)
    ssd_consts = (conv_w, conv_b.reshape(depth, 1, XBC_DIM),
                  _pad_last(a_log.reshape(depth, 1, 2 * SSM_HEADS)),
                  jnp.repeat(d_skip, SSM_HEAD_DIM, axis=-1).reshape(depth, 1, D_SSM),
                  ssm_norm_g.reshape(depth, 1, D_SSM))
    w_out16 = w_out.astype(bf16)
    wr = _pad_last(jnp.concatenate([w_router_expert, w_router_group], axis=-1)).astype(bf16)
    br = _pad_last(jnp.concatenate([b_router_expert, b_router_group], axis=-1)).reshape(depth, 1, LANES)
    wg16, wu16, wd16 = w_gate.astype(bf16), w_up.astype(bf16), w_down.astype(bf16)
    fg = final_norm_g.reshape(1, D_MODEL)

    caches = tuple(t.reshape(dec_batch, depth, past, D_KV) for t in (cache_a_k, cache_a_v, cache_b_k, cache_b_v))
    init = (state_ssm_fwd.reshape(dec_batch, depth, D_SSM, SSM_STATE),
            state_ssm_bwd.reshape(dec_batch, depth, D_SSM, SSM_STATE))

    xs = (x_prompt.reshape(n_p_tok, D_MODEL), x_sample.reshape(n_s_tok, D_MODEL))
    kvp, states = (), ()
    for l in range(depth):
        outs = _in_projection(xs, mod, l, n1, w_main, w_dt, qg, kg, dtb, headmat, rope, kvp, dims)
        qa, qb = outs[0:2]
        kvp, kvs = tuple(outs[2:6]), tuple(outs[6:10])
        z, xbc, dt = outs[10:13]

        oab = _attention_ctx(a_sink, qa, qb, kvp, l, dims)
        oab = _attention_lat(a_sink, qa, qb, kvs, caches, oab, l, dims)

        oc, hf, hb = _ssd(xbc, z, dt, ssd_consts, l, dims, latent=False, st_prev=states)
        states = (hf, hb)
        oc, = _ssd(xbc, z, dt, ssd_consts, l, dims, latent=True, init=init, oc=oc)

        x1, h2, comb = _out_projection(oab, oc, xs, mod, l, n2, w_out16, wr, br, dims)
        xs = tuple(_moe(h2, comb, wg16, wu16, wd16, x1, mod, l, fg, dims, final=(l == depth - 1)))

    y_prompt = xs[0].reshape(batch, seq, D_MODEL)
    y_sample = xs[1].reshape(dec_batch, dec_seq, D_MODEL)
    kv_shape = (batch, depth, seq, A_KV, HEAD_DIM)
    st_shape = (batch, depth, SSM_HEADS, SSM_HEAD_DIM, SSM_STATE)
    return ((y_prompt, y_sample) + tuple(t.reshape(kv_shape) for t in kvp)
            + tuple(t.reshape(st_shape) for t in states))
```

```python
import functools

import jax
import jax.numpy as jnp
import numpy as np
from jax import lax
from jax.experimental import pallas as pl
from jax.experimental.pallas import tpu as pltpu

f32 = jnp.float32
bf16 = jnp.bfloat16
HIGHEST = lax.Precision.HIGHEST

D_MODEL = 1024
GRID_W = 64
HEAD_DIM = 64
A_HEADS = 4
A_KV = 2
WINDOW = 128
B_HEADS = 4
B_KV = 2
ROPE_THETA = 10000.0
ROPE_QUARTER = HEAD_DIM // 4
SSM_HEADS = 8
SSM_HEAD_DIM = 64
D_SSM = SSM_HEADS * SSM_HEAD_DIM
SSM_GROUPS = 2
SSM_STATE = 64
CHUNK = 128
XBC_DIM = D_SSM + 2 * SSM_GROUPS * SSM_STATE
D_AB = (A_HEADS + B_HEADS) * HEAD_DIM
D_Q = A_HEADS * HEAD_DIM
D_KV = A_KV * HEAD_DIM
N_EGROUPS = 4
EXPERTS_PER_GROUP = 4
N_EXPERTS = N_EGROUPS * EXPERTS_PER_GROUP
D_EXPERT = 256
EPS = 1e-6

LANES = 128
C_AQ, C_AK, C_AV, C_BQ, C_BK, C_BV, C_Z, C_XBC, C_DT = 0, 256, 384, 512, 768, 896, 1024, 1536, 2304

SSD_CTX_SEQS = 4

GSEL_LANE = N_EXPERTS
MOE_TM = 1024
MOE_SB = 256
MOE_CAP = 128
MOE_FIN = 256

VMEM_LIMIT = 56 * 1024 * 1024
ANY = pl.BlockSpec(memory_space=pl.ANY)
SMEM = pl.BlockSpec(memory_space=pltpu.SMEM)


def _params(sem, vmem=VMEM_LIMIT):
    return pltpu.CompilerParams(dimension_semantics=sem, vmem_limit_bytes=vmem)


def _dot(a, b, **kw):
    return jnp.dot(a, b, preferred_element_type=f32, **kw)


def _dot_nt(a, b):
    return lax.dot_general(a, b, (((1,), (1,)), ((), ())), preferred_element_type=f32)


def _dot_tn(a, b):
    return lax.dot_general(a, b, (((0,), (0,)), ((), ())), preferred_element_type=f32)


def _silu(x):
    return (0.5 * x) * (1.0 + jnp.tanh(0.5 * x))


def _softplus(x):
    return jnp.maximum(x, 0.0) + jnp.log1p(jnp.exp(-jnp.abs(x)))


def _rms(x):
    return x * lax.rsqrt(jnp.mean(x * x, axis=-1, keepdims=True) + EPS)


class _Tiling:
    def __init__(self, n_p_tok, n_s_tok, dec_seq, tm):
        self.tm = tm
        self.n_p = n_p_tok // tm
        self.n_s = n_s_tok // tm
        self.n = self.n_p + self.n_s
        self.per_seq = dec_seq // tm

    def p_idx(self, i):
        return jnp.minimum(i, self.n_p - 1)

    def s_idx(self, i):
        return jnp.maximum(i - self.n_p, 0)

    def mod_row(self, i):
        return jnp.where(i < self.n_p, 0, 1 + (i - self.n_p) // self.per_seq)

    def seq_pos(self, i):
        return jnp.where(i < self.n_p, 0, (i - self.n_p) % self.per_seq)


def _x_specs(til, split):
    tm = til.tm
    if split:
        return [pl.BlockSpec((tm, D_MODEL), lambda i, *_: (til.p_idx(i), 0)),
                pl.BlockSpec((tm, D_MODEL), lambda i, *_: (til.s_idx(i), 0))]
    return [pl.BlockSpec((tm, D_MODEL), lambda i, *_: (i, 0))]


def _load_x(refs, i, n_p):
    if len(refs) == 2:
        return jnp.where(i < n_p, refs[0][...], refs[1][...])
    return refs[0][...]


def _mod_spec(til, layer, k):
    return pl.BlockSpec((None, None, None, 1, D_MODEL), lambda i, *_: (layer, til.mod_row(i), k, 0, 0))


def _mod_kernel(c_ref, w_ref, b_ref, o_ref):
    s = _silu(c_ref[...])
    o_ref[...] = _dot(s, w_ref[...], precision=HIGHEST) + b_ref[...]


def _modulation(cvec, w_ada, b_ada):
    depth = w_ada.shape[0]
    n = w_ada.shape[2]
    tn = 1536
    return pl.pallas_call(
        _mod_kernel,
        grid=(depth, n // tn),
        in_specs=[
            pl.BlockSpec((8, D_MODEL), lambda l, j: (0, 0)),
            pl.BlockSpec((None, D_MODEL, tn), lambda l, j: (l, 0, j)),
            pl.BlockSpec((None, 1, tn), lambda l, j: (l, 0, j)),
        ],
        out_specs=pl.BlockSpec((None, 8, tn), lambda l, j: (l, 0, j)),
        out_shape=jax.ShapeDtypeStruct((depth, 8, n), f32),
        compiler_params=_params(("arbitrary", "arbitrary")),
        name="adaln_mod",
    )(cvec, w_ada, b_ada.reshape(depth, 1, n))


def _rope(x, cos, sin_even, sin_odd):
    w = x.shape[-1]
    nxt = pltpu.roll(x, w - ROPE_QUARTER, 1)
    prv = pltpu.roll(x, ROPE_QUARTER, 1)
    return x * cos + nxt * sin_even + prv * sin_odd


def _inproj_kernel(*refs, n_x, n_alias, n_p, seqs_per_tile):
    x_refs = refs[:n_x]
    (sh_ref, sc_ref, g_ref, w_ref, wdt_ref, qg_ref, kg_ref, dtb_ref, hm_ref,
     cos_ref, se_ref, so_ref) = refs[n_x:n_x + 12]
    (qa_ref, qb_ref, akp_ref, avp_ref, bkp_ref, bvp_ref, aks_ref, avs_ref, bks_ref, bvs_ref,
     z_ref, xbc_ref, dt_ref) = refs[n_x + 12 + n_alias:]
    i = pl.program_id(0)
    h = _rms(_load_x(x_refs, i, n_p)) * g_ref[...]
    h = h * (1.0 + sc_ref[...]) + sh_ref[...]
    hb = h.astype(bf16)

    def proj(lo, hi):
        return _dot(hb, w_ref[:, lo:hi])

    def head_norm(t, gain):
        w = t.shape[-1]
        sq = t * t
        hi = sq.astype(bf16)
        lo = (sq - hi.astype(f32)).astype(bf16)
        ms_h = (_dot(hi, hm_ref[0:w, 0:w]) + _dot(lo, hm_ref[0:w, 0:w])) * (1.0 / HEAD_DIM)
        return t * lax.rsqrt(ms_h + EPS) * gain

    qa = proj(C_AQ, C_AK)
    ka = proj(C_AK, C_AV)
    va = proj(C_AV, C_BQ)
    qb = head_norm(proj(C_BQ, C_BK), qg_ref[...])
    kb = head_norm(proj(C_BK, C_BV), kg_ref[:, 0:D_KV])
    vb = proj(C_BV, C_Z)
    z_ref[...] = proj(C_Z, C_XBC).astype(bf16)
    xbc_ref[...] = proj(C_XBC, C_DT).astype(bf16)
    dt_ref[...] = _softplus(_dot(hb, wdt_ref[...]) + dtb_ref[...])

    lat = i >= n_p
    cos = jnp.where(lat, cos_ref[...], 1.0)
    se = jnp.where(lat, se_ref[...], 0.0)
    so = jnp.where(lat, so_ref[...], 0.0)
    qa_ref[...] = _rope(qa, cos, se, so).astype(bf16)
    qb_ref[...] = _rope(qb, cos, se, so).astype(bf16)
    aks_ref[...] = _rope(ka, cos[:, :D_KV], se[:, :D_KV], so[:, :D_KV]).astype(bf16)
    bks_ref[...] = _rope(kb, cos[:, :D_KV], se[:, :D_KV], so[:, :D_KV]).astype(bf16)
    avs_ref[...] = va.astype(bf16)
    bvs_ref[...] = vb.astype(bf16)

    @pl.when(i < n_p)
    def _():
        shp = (seqs_per_tile, -1, D_KV)
        akp_ref[...] = ka.reshape(shp)
        avp_ref[...] = va.reshape(shp)
        bkp_ref[...] = kb.reshape(shp)
        bvp_ref[...] = vb.reshape(shp)


def _in_projection(xs, mod, layer, norm_g, w_main, w_dt, qg, kg, dtb, headmat, rope, kv_prev, dims, tm=512):
    batch, seq, dec_batch, dec_seq, depth = dims
    n_p_tok, n_s_tok = batch * seq, dec_batch * dec_seq
    t = n_p_tok + n_s_tok
    til = _Tiling(n_p_tok, n_s_tok, dec_seq, tm)
    spt = tm // seq
    cos, se, so = rope
    n_alias = len(kv_prev)

    def c2(shape):
        return pl.BlockSpec(shape, lambda i: (0, 0))

    def lspec(shape):
        return pl.BlockSpec((None,) + shape, lambda i: (layer,) + (0,) * len(shape))

    rope_spec = pl.BlockSpec((tm, D_Q), lambda i: (til.seq_pos(i), 0))
    tok = lambda w: pl.BlockSpec((tm, w), lambda i: (i, 0))
    kvp = pl.BlockSpec((spt, None, seq, D_KV), lambda i: (til.p_idx(i), layer, 0, 0))
    kvs = pl.BlockSpec((tm, D_KV), lambda i: (til.s_idx(i), 0))
    kvp_shape = jax.ShapeDtypeStruct((batch, depth, seq, D_KV), f32)
    kvs_shape = jax.ShapeDtypeStruct((n_s_tok, D_KV), bf16)
    n_in = len(xs) + 12
    return pl.pallas_call(
        functools.partial(_inproj_kernel, n_x=len(xs), n_alias=n_alias, n_p=til.n_p, seqs_per_tile=spt),
        grid=(til.n,),
        in_specs=_x_specs(til, len(xs) == 2) + [
            _mod_spec(til, layer, 0), _mod_spec(til, layer, 1),
            lspec((1, D_MODEL)), lspec((D_MODEL, C_DT)), lspec((D_MODEL, LANES)),
            lspec((1, D_Q)), lspec((1, D_Q)), lspec((1, LANES)), c2((D_Q, D_Q)),
            rope_spec, rope_spec, rope_spec] + [ANY] * n_alias,
        out_specs=[tok(D_Q), tok(D_Q), kvp, kvp, kvp, kvp, kvs, kvs, kvs, kvs,
                   tok(D_SSM), tok(XBC_DIM), tok(LANES)],
        out_shape=[jax.ShapeDtypeStruct((t, D_Q), bf16), jax.ShapeDtypeStruct((t, D_Q), bf16),
                   kvp_shape, kvp_shape, kvp_shape, kvp_shape,
                   kvs_shape, kvs_shape, kvs_shape, kvs_shape,
                   jax.ShapeDtypeStruct((t, D_SSM), bf16), jax.ShapeDtypeStruct((t, XBC_DIM), bf16),
                   jax.ShapeDtypeStruct((t, LANES), f32)],
        input_output_aliases={n_in + k: 2 + k for k in range(n_alias)},
        compiler_params=_params(("arbitrary",)),
        name="norm_mod_inproj",
    )(*xs, mod, mod, norm_g, w_main, w_dt, qg, kg, dtb, headmat, cos, se, so, *kv_prev)


def _softmax_pv(scores, values, sink):
    m = scores[0].max(axis=-1, keepdims=True)
    for s in scores[1:]:
        m = jnp.maximum(m, s.max(axis=-1, keepdims=True))
    if sink is not None:
        m = jnp.maximum(m, sink)
    den = None
    acc = None
    for s, v in zip(scores, values):
        p = jnp.exp(s - m)
        d = p.sum(axis=-1, keepdims=True)
        o = _dot(p.astype(bf16), v)
        den = d if den is None else den + d
        acc = o if acc is None else acc + o
    if sink is not None:
        den = den + jnp.exp(sink - m)
    return acc / den


def _attn_ctx_kernel(sink_ref, qa_ref, ka_ref, va_ref, qb_ref, kb_ref, vb_ref, o_ref, *, layer):
    scale = HEAD_DIM ** -0.5
    for mixer, (q_ref, k_ref, v_ref) in enumerate(((qa_ref, ka_ref, va_ref), (qb_ref, kb_ref, vb_ref))):
        for kv in range(A_KV):
            ks = slice(kv * HEAD_DIM, (kv + 1) * HEAD_DIM)
            k = k_ref[:, ks].astype(bf16)
            v = v_ref[:, ks].astype(bf16)
            for g in range(A_HEADS // A_KV):
                hd = kv * 2 + g
                hs = slice(hd * HEAD_DIM, (hd + 1) * HEAD_DIM)
                q = q_ref[:, hs] * scale
                s = _dot_nt(q, k)
                sink = sink_ref[layer, hd] if mixer == 0 else None
                o = _softmax_pv([s], [v], sink)
                os_ = slice(mixer * D_Q + hd * HEAD_DIM, mixer * D_Q + (hd + 1) * HEAD_DIM)
                o_ref[:, os_] = o.astype(bf16)


def _attention_ctx(sink, qa, qb, kvp, layer, dims):
    batch, seq, dec_batch, dec_seq, depth = dims
    t = batch * seq + dec_batch * dec_seq
    qspec = pl.BlockSpec((seq, D_Q), lambda b: (b, 0))
    kspec = pl.BlockSpec((None, None, seq, D_KV), lambda b: (b, layer, 0, 0))
    akp, avp, bkp, bvp = kvp
    return pl.pallas_call(
        functools.partial(_attn_ctx_kernel, layer=layer),
        grid=(batch,),
        in_specs=[SMEM, qspec, kspec, kspec, qspec, kspec, kspec],
        out_specs=pl.BlockSpec((seq, D_AB), lambda b: (b, 0)),
        out_shape=jax.ShapeDtypeStruct((t, D_AB), bf16),
        compiler_params=_params(("arbitrary",)),
        name="attn_context",
    )(sink, qa, akp, avp, qb, bkp, bvp)


def _attn_lat_kernel(sink_ref, qa_ref, ka_ref, va_ref, cka_ref, cva_ref,
                     qb_ref, kb_ref, vb_ref, ckb_ref, cvb_ref, alias_ref, o_ref, *, seq, layer):
    del alias_ref
    j = pl.program_id(1)
    scale = HEAD_DIM ** -0.5
    w = WINDOW
    start = pl.multiple_of(jnp.clip((j - 1) * w, 0, seq - 3 * w), w)
    qi = j * w + lax.broadcasted_iota(jnp.int32, (w, 3 * w), 0)
    ki = start + lax.broadcasted_iota(jnp.int32, (w, 3 * w), 1)
    valid = jnp.abs(ki - qi) <= w
    for kv in range(A_KV):
        ks = slice(kv * HEAD_DIM, (kv + 1) * HEAD_DIM)
        ka = ka_ref[pl.ds(start, 3 * w), ks]
        va = va_ref[pl.ds(start, 3 * w), ks]
        cka = cka_ref[:, ks].astype(bf16)
        cva = cva_ref[:, ks].astype(bf16)
        kb = kb_ref[:, ks]
        vb = vb_ref[:, ks]
        ckb = ckb_ref[:, ks].astype(bf16)
        cvb = cvb_ref[:, ks].astype(bf16)
        for g in range(A_HEADS // A_KV):
            hd = kv * 2 + g
            hs = slice(hd * HEAD_DIM, (hd + 1) * HEAD_DIM)
            q = qa_ref[:, hs] * scale
            s_loc = jnp.where(valid, _dot_nt(q, ka), -jnp.inf)
            s_ctx = _dot_nt(q, cka)
            o_ref[:, hs] = _softmax_pv([s_loc, s_ctx], [va, cva], sink_ref[layer, hd]).astype(bf16)
            q = qb_ref[:, hs] * scale
            o = _softmax_pv([_dot_nt(q, kb), _dot_nt(q, ckb)], [vb, cvb], None)
            o_ref[:, slice(D_Q + hd * HEAD_DIM, D_Q + (hd + 1) * HEAD_DIM)] = o.astype(bf16)


def _attention_lat(sink, qa, qb, kvs, caches, oab, layer, dims):
    batch, seq, dec_batch, dec_seq, depth = dims
    w = WINDOW
    nq = dec_seq // w
    q0 = batch * seq // w
    past = caches[0].shape[2]
    qspec = pl.BlockSpec((w, D_Q), lambda b, j: (q0 + b * nq + j, 0))
    kspec = pl.BlockSpec((dec_seq, D_KV), lambda b, j: (b, 0))
    cspec = pl.BlockSpec((None, None, past, D_KV), lambda b, j: (b, layer, 0, 0))
    aks, avs, bks, bvs = kvs
    cka, cva, ckb, cvb = caches
    return pl.pallas_call(
        functools.partial(_attn_lat_kernel, seq=dec_seq, layer=layer),
        grid=(dec_batch, nq),
        in_specs=[SMEM, qspec, kspec, kspec, cspec, cspec, qspec, kspec, kspec, cspec, cspec, ANY],
        out_specs=pl.BlockSpec((w, D_AB), lambda b, j: (q0 + b * nq + j, 0)),
        out_shape=jax.ShapeDtypeStruct(oab.shape, oab.dtype),
        input_output_aliases={11: 0},
        compiler_params=_params(("arbitrary", "arbitrary")),
        name="attn_latent",
    )(sink, qa, aks, avs, cka, cva, qb, bks, bvs, ckb, cvb, oab)


def _ssd_kernel(*refs, latent, n_alias, nseq, n):
    xbc_ref, z_ref, dt_ref, cw_ref, cb_ref, alog_ref, dskip_ref, g_ref, sel_ref = refs[:9]
    if latent:
        h0f_ref, h0b_ref = refs[9:11]
        o_ref = refs[11 + n_alias]
    else:
        o_ref, hf_ref, hb_ref = refs[9 + n_alias:12 + n_alias]
    xc_scr, st_scr, he_scr, lhs_scr, dec_scr = refs[12 + n_alias:]
    nc = n // CHUNK
    L = CHUNK
    ns = SSM_STATE
    nh = SSM_HEADS
    hpg = SSM_HEADS // SSM_GROUPS
    hd_w = SSM_HEAD_DIM

    x = xbc_ref[...].astype(f32)
    t_idx = lax.broadcasted_iota(jnp.int32, x.shape, 0) % n
    prv = jnp.where(t_idx == 0, 0.0, pltpu.roll(x, 1, 0))
    nxt = jnp.where(t_idx == n - 1, 0.0, pltpu.roll(x, nseq * n - 1, 0))
    y = prv * cw_ref[0:1, :] + x * cw_ref[1:2, :] + nxt * cw_ref[2:3, :] + cb_ref[...]
    xc_scr[...] = _silu(y)

    a_neg = -jnp.exp(alog_ref[...])
    r_i = lax.broadcasted_iota(jnp.int32, (L, L), 0)
    c_i = lax.broadcasted_iota(jnp.int32, (L, L), 1)
    lower = r_i >= c_i
    upper = r_i <= c_i
    tril = lower.astype(f32).astype(bf16)

    def split3(v):
        v0 = v.astype(bf16)
        r1 = v - v0.astype(f32)
        v1 = r1.astype(bf16)
        return v0, v1, (r1 - v1.astype(f32)).astype(bf16)

    def prefix_sum(v):
        return sum(_dot(tril, t) for t in split3(v))

    def spread(v):
        return sum(_dot(t, sel_ref[...]) for t in split3(v))

    def stage1(c, carry):
        rows = pl.ds(pl.multiple_of(c * L, L), L)
        xs16 = xc_scr[rows, 0:D_SSM].astype(bf16)
        bm = xc_scr[rows, D_SSM:D_SSM + SSM_GROUPS * ns]
        cm = xc_scr[rows, D_SSM + SSM_GROUPS * ns:XBC_DIM]
        dt = dt_ref[rows, :]
        da = dt * a_neg
        cs = prefix_sum(da)
        tot = cs[L - 1:L, :]
        suf = tot - cs + da
        cs_t, suf_t, dt_t, b_t = cs.T, suf.T, dt.T, bm.T
        tot_c = cs_t[:, L - 1:L]
        wf_t = jnp.exp(tot_c[0:nh] - cs_t[0:nh]) * dt_t[0:nh]
        wb_t = jnp.exp(tot_c[nh:2 * nh] - suf_t[nh:2 * nh]) * dt_t[nh:2 * nh]
        dec_scr[c] = spread(jnp.broadcast_to(jnp.exp(tot), (8, LANES)))
        cm16 = cm.astype(bf16)
        bm16 = bm.astype(bf16)
        for g in range(SSM_GROUPS):
            gs = slice(g * ns, (g + 1) * ns)
            cb = _dot_nt(cm16[:, gs], bm16[:, gs])
            cg = cm[:, gs]
            bg_t = b_t[gs, :]
            for hh in range(hpg):
                hd = g * hpg + hh
                hb_ = nh + hd
                cols = slice(hd * hd_w, (hd + 1) * hd_w)
                lhs1 = jnp.concatenate([bg_t * wf_t[hd:hd + 1, :], bg_t * wb_t[hd:hd + 1, :]], axis=0)
                st_scr[c, :, cols] = _dot(lhs1.astype(bf16), xs16[:, cols])
                csb = jnp.broadcast_to(cs[:, hd:hd + 1], (L, L))
                sfb = jnp.broadcast_to(suf[:, hb_:hb_ + 1], (L, L))
                lf = jnp.exp(jnp.where(lower, csb - cs_t[hd:hd + 1, :], -jnp.inf))
                lb = jnp.exp(jnp.where(upper, sfb - suf_t[hb_:hb_ + 1, :], -jnp.inf))
                m = cb * (lf * dt_t[hd:hd + 1, :] + lb * dt_t[hb_:hb_ + 1, :])
                lhs_scr[c, hd] = jnp.concatenate(
                    [m, cg * jnp.exp(csb[:, 0:ns]), cg * jnp.exp(sfb[:, 0:ns])], axis=1).astype(bf16)
        return carry

    lax.fori_loop(0, nseq * nc, stage1, 0)

    def to_t(h):
        return jnp.concatenate([h, jnp.zeros_like(h)], axis=1).T[0:ns, :]

    def from_t(ht):
        return jnp.concatenate([ht, jnp.zeros_like(ht)], axis=0).T[:, 0:ns]

    for s in range(nseq):
        if latent:
            hf, hb = to_t(h0f_ref[s]), to_t(h0b_ref[s])
        else:
            hf = hb = jnp.zeros((ns, D_SSM), f32)
        for k in range(nc):
            cf = s * nc + k
            cr = s * nc + nc - 1 - k
            he_scr[cf, 0:ns, :] = hf.astype(bf16)
            hf = hf * dec_scr[cf, 0:1, 0:D_SSM] + st_scr[cf, 0:ns, :]
            he_scr[cr, ns:2 * ns, :] = hb.astype(bf16)
            hb = hb * dec_scr[cr, 0:1, D_SSM:2 * D_SSM] + st_scr[cr, ns:2 * ns, :]
        if not latent:
            hf_ref[s] = from_t(hf)
            hb_ref[s] = from_t(hb)

    def stage3(c, carry):
        rows = pl.ds(pl.multiple_of(c * L, L), L)
        xs = xc_scr[rows, 0:D_SSM]
        xs16 = xs.astype(bf16)
        ys = []
        for hd in range(nh):
            cols = slice(hd * hd_w, (hd + 1) * hd_w)
            rhs = jnp.concatenate([xs16[:, cols], he_scr[c, 0:ns, cols], he_scr[c, ns:2 * ns, cols]], axis=0)
            ys.append(_dot(lhs_scr[c, hd], rhs))
        yv = jnp.concatenate(ys, axis=1) + xs * dskip_ref[...]
        yv = yv * _silu(z_ref[rows, :].astype(f32))
        o_ref[rows, :] = (_rms(yv) * g_ref[...]).astype(bf16)
        return carry

    lax.fori_loop(0, nseq * nc, stage3, 0, unroll=2)


def _ssd(xbc, z, dt, consts, layer, dims, *, latent, init=None, oc=None, st_prev=()):
    batch, seq, dec_batch, dec_seq, depth = dims
    t = batch * seq + dec_batch * dec_seq
    if latent:
        n_b, n, nseq = dec_batch, dec_seq, 1
    else:
        n_b, n, nseq = batch, seq, SSD_CTX_SEQS
    rows = nseq * n
    b0 = batch * seq // rows if latent else 0

    def tok(width):
        return pl.BlockSpec((rows, width), lambda b: (b0 + b, 0))

    def lspec(shape):
        return pl.BlockSpec((None,) + shape, lambda b: (layer,) + (0,) * len(shape))

    st = pl.BlockSpec((nseq, None, D_SSM, SSM_STATE), lambda b: (b, layer, 0, 0))
    in_specs = [tok(XBC_DIM), tok(D_SSM), tok(LANES),
                lspec((3, XBC_DIM)), lspec((1, XBC_DIM)), lspec((1, LANES)), lspec((1, D_SSM)),
                lspec((1, D_SSM)), pl.BlockSpec((LANES, 2 * D_SSM), lambda b: (0, 0))]
    args = [xbc, z, dt, *consts, _head_spread_matrix()]
    oc_shape = jax.ShapeDtypeStruct((t, D_SSM), bf16)
    oc_spec = pl.BlockSpec((rows, D_SSM), lambda b: (b0 + b, 0))
    nck = rows // CHUNK
    scratch = [pltpu.VMEM((rows, XBC_DIM), f32),
               pltpu.VMEM((nck, 2 * SSM_STATE, D_SSM), f32),
               pltpu.VMEM((nck, 2 * SSM_STATE, D_SSM), bf16),
               pltpu.VMEM((nck, SSM_HEADS, CHUNK, 2 * CHUNK), bf16),
               pltpu.VMEM((nck, 8, 2 * D_SSM), f32)]
    if latent:
        in_specs += [st, st, ANY]
        args += [init[0], init[1], oc]
        out_specs, out_shape = [oc_spec], [oc_shape]
        aliases = {11: 0}
        n_alias = 1
    else:
        n_alias = len(st_prev)
        in_specs += [ANY] * n_alias
        args += list(st_prev)
        st_shape = jax.ShapeDtypeStruct((batch, depth, D_SSM, SSM_STATE), f32)
        out_specs, out_shape = [oc_spec, st, st], [oc_shape, st_shape, st_shape]
        aliases = {9 + k: 1 + k for k in range(n_alias)}
    return pl.pallas_call(
        functools.partial(_ssd_kernel, latent=latent, n_alias=n_alias, nseq=nseq, n=n),
        grid=(n_b // nseq,),
        in_specs=in_specs, out_specs=out_specs, out_shape=out_shape,
        scratch_shapes=scratch,
        input_output_aliases=aliases,
        compiler_params=_params(("arbitrary",)),
        name="ssd_latent" if latent else "ssd_context",
    )(*args)


def _outproj_kernel(*refs, n_x, n_p):
    x_refs = refs[:n_x]
    (oab_ref, oc_ref, g1_ref, sh_ref, sc_ref, n2_ref, w_ref, wr_ref, br_ref,
     x1_ref, h2_ref, comb_ref) = refs[n_x:]
    i = pl.program_id(0)
    o = _dot(oab_ref[...], w_ref[0:D_AB, :]) + _dot(oc_ref[...], w_ref[D_AB:, :])
    x1 = _load_x(x_refs, i, n_p) + g1_ref[...] * o
    x1_ref[...] = x1
    h2 = _rms(x1) * n2_ref[...]
    h2 = h2 * (1.0 + sc_ref[...]) + sh_ref[...]
    h2b = h2.astype(bf16)
    h2_ref[...] = h2b

    logits = _dot(h2b, wr_ref[...]) + br_ref[...]
    lane = lax.broadcasted_iota(jnp.int32, logits.shape, 1).astype(f32)
    big = float(LANES)
    neg = -jnp.inf
    gmask = (lane >= N_EXPERTS) & (lane < N_EXPERTS + N_EGROUPS)
    gl = jnp.where(gmask, logits, neg)
    gmax = gl.max(axis=-1, keepdims=True)
    gsel = jnp.where(gl == gmax, lane, big).min(axis=-1, keepdims=True) - N_EXPERTS
    gprob = 1.0 / jnp.where(gmask, jnp.exp(logits - gmax), 0.0).sum(axis=-1, keepdims=True)
    emask = (lane >= gsel * EXPERTS_PER_GROUP) & (lane < (gsel + 1) * EXPERTS_PER_GROUP)
    el = jnp.where(emask, logits, neg)
    v1 = el.max(axis=-1, keepdims=True)
    i1 = jnp.where(el == v1, lane, big).min(axis=-1, keepdims=True)
    el2 = jnp.where(lane == i1, neg, el)
    v2 = el2.max(axis=-1, keepdims=True)
    i2 = jnp.where(el2 == v2, lane, big).min(axis=-1, keepdims=True)
    e2 = jnp.exp(v2 - v1)
    den = 1.0 + e2
    comb = jnp.where(lane == i1, gprob / den, 0.0) + jnp.where(lane == i2, gprob * e2 / den, 0.0)
    comb_ref[...] = jnp.where(lane == GSEL_LANE, gsel, comb)


def _out_projection(oab, oc, xs, mod, layer, norm_g, w_out, wr, br, dims, tm=512):
    batch, seq, dec_batch, dec_seq, depth = dims
    n_p_tok, n_s_tok = batch * seq, dec_batch * dec_seq
    t = n_p_tok + n_s_tok
    til = _Tiling(n_p_tok, n_s_tok, dec_seq, tm)

    def lspec(shape):
        return pl.BlockSpec((None,) + shape, lambda i: (layer,) + (0,) * len(shape))

    tok = lambda w: pl.BlockSpec((tm, w), lambda i: (i, 0))
    return pl.pallas_call(
        functools.partial(_outproj_kernel, n_x=len(xs), n_p=til.n_p),
        grid=(til.n,),
        in_specs=_x_specs(til, len(xs) == 2) + [
            tok(D_AB), tok(D_SSM),
            _mod_spec(til, layer, 2), _mod_spec(til, layer, 3), _mod_spec(til, layer, 4),
            lspec((1, D_MODEL)), lspec((D_AB + D_SSM, D_MODEL)), lspec((D_MODEL, LANES)), lspec((1, LANES))],
        out_specs=[tok(D_MODEL), tok(D_MODEL), tok(LANES)],
        out_shape=[jax.ShapeDtypeStruct((t, D_MODEL), f32),
                   jax.ShapeDtypeStruct((t, D_MODEL), bf16),
                   jax.ShapeDtypeStruct((t, LANES), f32)],
        compiler_params=_params(("arbitrary",)),
        name="outproj_norm_router",
    )(*xs, oab, oc, mod, mod, mod, norm_g, w_out, wr, br)


def _expert_ffn(h, weight_of, experts, wg_ref, wu_ref, wd_rows):
    hid = []
    for e in experts:
        a = _dot(h, wg_ref[e])
        u = _dot(h, wu_ref[e])
        hid.append((_silu(a) * u * weight_of(e)).astype(bf16))
    return _dot(jnp.concatenate(hid, axis=1), wd_rows)


def _moe_kernel(h2_ref, comb_ref, wg_ref, wu_ref, wd_ref, x1_ref, g2_ref, fg_ref, *rest, final, n_p):
    outs, y_scr = rest[:-1], rest[-1]
    i = pl.program_id(0)
    k = pl.program_id(1)
    sb, cap = MOE_SB, MOE_CAP
    nsb = MOE_TM // sb
    slots = N_EGROUPS * cap
    gh = EXPERTS_PER_GROUP * D_EXPERT

    @pl.when(k == 0)
    def _():
        comb = comb_ref[...]
        lane = lax.broadcasted_iota(jnp.int32, comb.shape, 1).astype(f32)
        gsel = comb[:, GSEL_LANE:GSEL_LANE + 1]
        mine = lane == gsel
        onehot = mine.astype(f32).astype(bf16)
        r_i = lax.broadcasted_iota(jnp.int32, (sb, sb), 0)
        c_i = lax.broadcasted_iota(jnp.int32, (sb, sb), 1)
        before = (c_i < r_i).astype(f32).astype(bf16)
        ranks = []
        worst = None
        for j in range(nsb):
            rows = slice(j * sb, (j + 1) * sb)
            earlier = _dot(before, onehot[rows])
            own = jnp.where(mine[rows], earlier, 0.0)
            worst = own if worst is None else jnp.maximum(worst, own)
            ranks.append(own.sum(axis=-1, keepdims=True))
        rank = jnp.concatenate(ranks, axis=0)
        fits = jnp.max(worst) < cap

        @pl.when(fits)
        def _():
            hi = comb.astype(bf16)
            lo = (comb - hi.astype(f32)).astype(bf16)
            dest = gsel * cap + rank
            slot_i = lax.broadcasted_iota(jnp.int32, (sb, slots), 1).astype(f32)
            place, packed_h, packed_c = [], [], []
            for j in range(nsb):
                rows = slice(j * sb, (j + 1) * sb)
                pt = (slot_i == dest[rows]).astype(f32).astype(bf16)
                place.append(pt)
                hx = jnp.concatenate([h2_ref[rows, :], hi[rows], lo[rows]], axis=1)
                srt = _dot_tn(pt, hx)
                packed_h.append(srt[:, :D_MODEL].astype(bf16))
                packed_c.append(srt[:, D_MODEL:D_MODEL + LANES] + srt[:, D_MODEL + LANES:])
            ys = []
            for g in range(N_EGROUPS):
                seg = slice(g * cap, (g + 1) * cap)
                hs = jnp.concatenate([p[seg] for p in packed_h], axis=0)
                cw = jnp.concatenate([p[seg] for p in packed_c], axis=0)
                experts = range(g * EXPERTS_PER_GROUP, (g + 1) * EXPERTS_PER_GROUP)
                yg = _expert_ffn(hs, lambda e: cw[:, e:e + 1], experts, wg_ref, wu_ref,
                                 wd_ref[g * gh:(g + 1) * gh, :])
                ys.append(yg.astype(bf16))
            for j in range(nsb):
                ysrt = jnp.concatenate([yg[j * cap:(j + 1) * cap] for yg in ys], axis=0)
                y_scr[j * sb:(j + 1) * sb, :] = _dot(place[j], ysrt).astype(bf16)

        @pl.when(jnp.logical_not(fits))
        def _():
            def body(j, carry):
                rows = pl.ds(pl.multiple_of(j * sb, sb), sb)
                cj = comb_ref[rows, :]
                y = _expert_ffn(h2_ref[rows, :], lambda e: cj[:, e:e + 1], range(N_EXPERTS),
                                wg_ref, wu_ref, wd_ref[...])
                y_scr[rows, :] = y.astype(bf16)
                return carry

            lax.fori_loop(0, nsb, body, 0)

    @pl.when(k > 0)
    def _():
        r0 = pl.multiple_of((k - 1) * MOE_FIN, MOE_FIN)
        x2 = x1_ref[...] + g2_ref[...] * y_scr[pl.ds(r0, MOE_FIN), :].astype(f32)
        if final:
            yp_ref, ys_ref = outs
            x2 = _rms(x2) * fg_ref[...]

            @pl.when(i < n_p)
            def _():
                yp_ref[...] = x2

            @pl.when(i >= n_p)
            def _():
                ys_ref[...] = x2
        else:
            outs[0][...] = x2


def _moe(h2, comb, wg, wu, wd, x1, mod, layer, final_g, dims, final):
    batch, seq, dec_batch, dec_seq, depth = dims
    n_p_tok, n_s_tok = batch * seq, dec_batch * dec_seq
    t = n_p_tok + n_s_tok
    tm = MOE_TM
    til = _Tiling(n_p_tok, n_s_tok, dec_seq, tm)
    nfin = tm // MOE_FIN
    once = pl.Buffered(1)
    tok = lambda w: pl.BlockSpec((tm, w), lambda i, k: (i, 0))

    def fin_block(i, k):
        return i * nfin + jnp.maximum(k - 1, 0)

    fin = lambda idx: pl.BlockSpec((MOE_FIN, D_MODEL), lambda i, k: (idx(i, k), 0))
    if final:
        out_specs = [fin(lambda i, k: jnp.minimum(fin_block(i, k), til.n_p * nfin - 1)),
                     fin(lambda i, k: jnp.maximum(fin_block(i, k) - til.n_p * nfin, 0))]
        out_shape = [jax.ShapeDtypeStruct((n_p_tok, D_MODEL), f32),
                     jax.ShapeDtypeStruct((n_s_tok, D_MODEL), f32)]
    else:
        out_specs = [fin(fin_block)]
        out_shape = [jax.ShapeDtypeStruct((t, D_MODEL), f32)]
    return pl.pallas_call(
        functools.partial(_moe_kernel, final=final, n_p=til.n_p),
        grid=(til.n, 1 + nfin),
        in_specs=[tok(D_MODEL), tok(LANES),
                  pl.BlockSpec((None, N_EXPERTS, D_MODEL, D_EXPERT), lambda i, k: (layer, 0, 0, 0), once),
                  pl.BlockSpec((None, N_EXPERTS, D_MODEL, D_EXPERT), lambda i, k: (layer, 0, 0, 0), once),
                  pl.BlockSpec((None, N_EXPERTS * D_EXPERT, D_MODEL), lambda i, k: (layer, 0, 0), once),
                  fin(fin_block), _mod_spec(til, layer, 5),
                  pl.BlockSpec((1, D_MODEL), lambda i, k: (0, 0))],
        out_specs=out_specs, out_shape=out_shape,
        scratch_shapes=[pltpu.VMEM((tm, D_MODEL), bf16)],
        compiler_params=_params(("arbitrary", "arbitrary")),
        name="moe_ffn",
    )(h2, comb, wg, wu, wd.reshape(depth, N_EXPERTS * D_EXPERT, D_MODEL), x1, mod, final_g)


def _rope_tables(n_rows):
    rows = jnp.repeat(jnp.arange(n_rows), GRID_W).astype(f32)
    cols = jnp.tile(jnp.arange(GRID_W), n_rows).astype(f32)
    inv = ROPE_THETA ** (-jnp.arange(ROPE_QUARTER, dtype=f32) / ROPE_QUARTER)
    ang_r = rows[:, None] * inv
    ang_c = cols[:, None] * inv
    ang = jnp.concatenate([ang_r, ang_r, ang_c, ang_c], axis=-1)
    cos, sin = jnp.cos(ang), jnp.sin(ang)
    even = (np.arange(HEAD_DIM) // ROPE_QUARTER) % 2 == 0
    sin_even = jnp.where(even, -sin, 0.0)
    sin_odd = jnp.where(even, 0.0, sin)
    return tuple(jnp.tile(t, (1, D_Q // HEAD_DIM)) for t in (cos, sin_even, sin_odd))


def _head_sum_matrix():
    m = np.zeros((D_Q, D_Q), np.float32)
    for h in range(D_Q // HEAD_DIM):
        m[h * HEAD_DIM:(h + 1) * HEAD_DIM, h * HEAD_DIM:(h + 1) * HEAD_DIM] = 1.0
    return jnp.asarray(m, dtype=bf16)


def _head_spread_matrix():
    m = np.zeros((LANES, 2 * D_SSM), np.float32)
    for j in range(2 * SSM_HEADS):
        m[j, j * SSM_HEAD_DIM:(j + 1) * SSM_HEAD_DIM] = 1.0
    return jnp.asarray(m, dtype=bf16)


def _pad_last(v, width=LANES):
    pad = [(0, 0)] * (v.ndim - 1) + [(0, width - v.shape[-1])]
    return jnp.pad(v, pad)


@jax.jit
def kernel(x_prompt, x_sample, cache_a_k, cache_a_v, cache_b_k, cache_b_v, state_ssm_fwd, state_ssm_bwd, c, c_ctx, norm1_g, norm2_g, final_norm_g, w_ada, b_ada, w_in, a_sink, q_norm_g, k_norm_g, conv_w, conv_b, dt_bias, a_log, d_skip, ssm_norm_g, w_out, w_router_group, b_router_group, w_router_expert, b_router_expert, w_gate, w_up, w_down):
    batch, seq, _ = x_prompt.shape
    dec_batch, dec_seq, _ = x_sample.shape
    depth = w_in.shape[0]
    past = cache_a_k.shape[2]
    dims = (batch, seq, dec_batch, dec_seq, depth)
    n_p_tok = batch * seq
    n_s_tok = dec_batch * dec_seq

    cvec = jnp.concatenate([c_ctx[None, :], c, jnp.zeros((8 - 1 - dec_batch, D_MODEL), f32)], axis=0)
    mod = _modulation(cvec, w_ada, b_ada).reshape(depth, 8, 6, 1, D_MODEL)

    rope = _rope_tables(dec_seq // GRID_W)
    headmat = _head_sum_matrix()
    w_main = w_in[:, :, :C_DT].astype(bf16)
    w_dt = _pad_last(w_in[:, :, C_DT:]).astype(bf16)
    qg = jnp.tile(q_norm_g, (1, D_Q // HEAD_DIM)).reshape(depth, 1, D_Q)
    kg = jnp.tile(k_norm_g, (1, D_Q // HEAD_DIM)).reshape(depth, 1, D_Q)
    dtb = _pad_last(dt_bias.reshape(depth, 1, 2 * SSM_HEADS))
    n1 = norm1_g.reshape(depth, 1, D_MODEL)
    n2 = norm2_g.reshape(depth, 1, D_MODEL)
    ssd_consts = (conv_w, conv_b.reshape(depth, 1, XBC_DIM),
                  _pad_last(a_log.reshape(depth, 1, 2 * SSM_HEADS)),
                  jnp.repeat(d_skip, SSM_HEAD_DIM, axis=-1).reshape(depth, 1, D_SSM),
                  ssm_norm_g.reshape(depth, 1, D_SSM))
    w_out16 = w_out.astype(bf16)
    wr = _pad_last(jnp.concatenate([w_router_expert, w_router_group], axis=-1)).astype(bf16)
    br = _pad_last(jnp.concatenate([b_router_expert, b_router_group], axis=-1)).reshape(depth, 1, LANES)
    wg16, wu16, wd16 = w_gate.astype(bf16), w_up.astype(bf16), w_down.astype(bf16)
    fg = final_norm_g.reshape(1, D_MODEL)

    caches = tuple(t.reshape(dec_batch, depth, past, D_KV) for t in (cache_a_k, cache_a_v, cache_b_k, cache_b_v))
    init = (state_ssm_fwd.reshape(dec_batch, depth, D_SSM, SSM_STATE),
            state_ssm_bwd.reshape(dec_batch, depth, D_SSM, SSM_STATE))

    xs = (x_prompt.reshape(n_p_tok, D_MODEL), x_sample.reshape(n_s_tok, D_MODEL))
    kvp, states = (), ()
    for l in range(depth):
        outs = _in_projection(xs, mod, l, n1, w_main, w_dt, qg, kg, dtb, headmat, rope, kvp, dims)
        qa, qb = outs[0:2]
        kvp, kvs = tuple(outs[2:6]), tuple(outs[6:10])
        z, xbc, dt = outs[10:13]

        oab = _attention_ctx(a_sink, qa, qb, kvp, l, dims)
        oab = _attention_lat(a_sink, qa, qb, kvs, caches, oab, l, dims)

        oc, hf, hb = _ssd(xbc, z, dt, ssd_consts, l, dims, latent=False, st_prev=states)
        states = (hf, hb)
        oc, = _ssd(xbc, z, dt, ssd_consts, l, dims, latent=True, init=init, oc=oc)

        x1, h2, comb = _out_projection(oab, oc, xs, mod, l, n2, w_out16, wr, br, dims)
        xs = tuple(_moe(h2, comb, wg16, wu16, wd16, x1, mod, l, fg, dims, final=(l == depth - 1)))

    y_prompt = xs[0].reshape(batch, seq, D_MODEL)
    y_sample = xs[1].reshape(dec_batch, dec_seq, D_MODEL)
    kv_shape = (batch, depth, seq, A_KV, HEAD_DIM)
    st_shape = (batch, depth, SSM_HEADS, SSM_HEAD_DIM, SSM_STATE)
    return ((y_prompt, y_sample) + tuple(t.reshape(kv_shape) for t in kvp)
            + tuple(t.reshape(st_shape) for t in states))
```

```python
import functools

import jax
import jax.numpy as jnp
import numpy as np
from jax import lax
from jax.experimental import pallas as pl
from jax.experimental.pallas import tpu as pltpu

f32 = jnp.float32
bf16 = jnp.bfloat16
HIGHEST = lax.Precision.HIGHEST

D_MODEL = 1024
GRID_W = 64
HEAD_DIM = 64
A_HEADS = 4
A_KV = 2
WINDOW = 128
B_HEADS = 4
B_KV = 2
ROPE_THETA = 10000.0
ROPE_QUARTER = HEAD_DIM // 4
SSM_HEADS = 8
SSM_HEAD_DIM = 64
D_SSM = SSM_HEADS * SSM_HEAD_DIM
SSM_GROUPS = 2
SSM_STATE = 64
CHUNK = 128
XBC_DIM = D_SSM + 2 * SSM_GROUPS * SSM_STATE
D_AB = (A_HEADS + B_HEADS) * HEAD_DIM
D_Q = A_HEADS * HEAD_DIM
D_KV = A_KV * HEAD_DIM
N_EGROUPS = 4
EXPERTS_PER_GROUP = 4
N_EXPERTS = N_EGROUPS * EXPERTS_PER_GROUP
D_EXPERT = 256
EPS = 1e-6

LANES = 128
C_AQ, C_AK, C_AV, C_BQ, C_BK, C_BV, C_Z, C_XBC, C_DT = 0, 256, 384, 512, 768, 896, 1024, 1536, 2304

SSD_CTX_SEQS = 4

GSEL_LANE = N_EXPERTS
MOE_TM = 1024
MOE_SB = 256
MOE_CAP = 160
MOE_FIN = 256

VMEM_LIMIT = 56 * 1024 * 1024
ANY = pl.BlockSpec(memory_space=pl.ANY)
SMEM = pl.BlockSpec(memory_space=pltpu.SMEM)


def _params(sem, vmem=VMEM_LIMIT):
    return pltpu.CompilerParams(dimension_semantics=sem, vmem_limit_bytes=vmem)


def _dot(a, b, **kw):
    return jnp.dot(a, b, preferred_element_type=f32, **kw)


def _dot_nt(a, b):
    return lax.dot_general(a, b, (((1,), (1,)), ((), ())), preferred_element_type=f32)


def _dot_tn(a, b):
    return lax.dot_general(a, b, (((0,), (0,)), ((), ())), preferred_element_type=f32)


def _silu(x):
    return (0.5 * x) * (1.0 + jnp.tanh(0.5 * x))


def _softplus(x):
    return jnp.maximum(x, 0.0) + jnp.log1p(jnp.exp(-jnp.abs(x)))


def _rms(x):
    return x * lax.rsqrt(jnp.mean(x * x, axis=-1, keepdims=True) + EPS)


class _Tiling:
    def __init__(self, n_p_tok, n_s_tok, dec_seq, tm):
        self.tm = tm
        self.n_p = n_p_tok // tm
        self.n_s = n_s_tok // tm
        self.n = self.n_p + self.n_s
        self.per_seq = dec_seq // tm

    def p_idx(self, i):
        return jnp.minimum(i, self.n_p - 1)

    def s_idx(self, i):
        return jnp.maximum(i - self.n_p, 0)

    def mod_row(self, i):
        return jnp.where(i < self.n_p, 0, 1 + (i - self.n_p) // self.per_seq)

    def seq_pos(self, i):
        return jnp.where(i < self.n_p, 0, (i - self.n_p) % self.per_seq)


def _x_specs(til, split):
    tm = til.tm
    if split:
        return [pl.BlockSpec((tm, D_MODEL), lambda i, *_: (til.p_idx(i), 0)),
                pl.BlockSpec((tm, D_MODEL), lambda i, *_: (til.s_idx(i), 0))]
    return [pl.BlockSpec((tm, D_MODEL), lambda i, *_: (i, 0))]


def _load_x(refs, i, n_p):
    if len(refs) == 2:
        return jnp.where(i < n_p, refs[0][...], refs[1][...])
    return refs[0][...]


def _mod_spec(til, layer, k):
    return pl.BlockSpec((None, None, None, 1, D_MODEL), lambda i, *_: (layer, til.mod_row(i), k, 0, 0))


def _mod_kernel(c_ref, w_ref, b_ref, o_ref):
    s = _silu(c_ref[...])
    o_ref[...] = _dot(s, w_ref[...], precision=HIGHEST) + b_ref[...]


def _modulation(cvec, w_ada, b_ada):
    depth = w_ada.shape[0]
    n = w_ada.shape[2]
    tn = 1536
    return pl.pallas_call(
        _mod_kernel,
        grid=(depth, n // tn),
        in_specs=[
            pl.BlockSpec((8, D_MODEL), lambda l, j: (0, 0)),
            pl.BlockSpec((None, D_MODEL, tn), lambda l, j: (l, 0, j)),
            pl.BlockSpec((None, 1, tn), lambda l, j: (l, 0, j)),
        ],
        out_specs=pl.BlockSpec((None, 8, tn), lambda l, j: (l, 0, j)),
        out_shape=jax.ShapeDtypeStruct((depth, 8, n), f32),
        compiler_params=_params(("arbitrary", "arbitrary")),
        name="adaln_mod",
    )(cvec, w_ada, b_ada.reshape(depth, 1, n))


def _rope(x, cos, sin_even, sin_odd):
    w = x.shape[-1]
    nxt = pltpu.roll(x, w - ROPE_QUARTER, 1)
    prv = pltpu.roll(x, ROPE_QUARTER, 1)
    return x * cos + nxt * sin_even + prv * sin_odd


def _inproj_kernel(*refs, n_x, n_alias, n_p, seqs_per_tile):
    x_refs = refs[:n_x]
    (sh_ref, sc_ref, g_ref, w_ref, wdt_ref, qg_ref, kg_ref, dtb_ref, hm_ref,
     cos_ref, se_ref, so_ref) = refs[n_x:n_x + 12]
    (qa_ref, qb_ref, akp_ref, avp_ref, bkp_ref, bvp_ref, aks_ref, avs_ref, bks_ref, bvs_ref,
     z_ref, xbc_ref, dt_ref) = refs[n_x + 12 + n_alias:]
    i = pl.program_id(0)
    h = _rms(_load_x(x_refs, i, n_p)) * g_ref[...]
    h = h * (1.0 + sc_ref[...]) + sh_ref[...]
    hb = h.astype(bf16)

    def proj(lo, hi):
        return _dot(hb, w_ref[:, lo:hi])

    def head_norm(t, gain):
        w = t.shape[-1]
        sq = t * t
        hi = sq.astype(bf16)
        lo = (sq - hi.astype(f32)).astype(bf16)
        ms_h = (_dot(hi, hm_ref[0:w, 0:w]) + _dot(lo, hm_ref[0:w, 0:w])) * (1.0 / HEAD_DIM)
        return t * lax.rsqrt(ms_h + EPS) * gain

    qa = proj(C_AQ, C_AK)
    ka = proj(C_AK, C_AV)
    va = proj(C_AV, C_BQ)
    qb = head_norm(proj(C_BQ, C_BK), qg_ref[...])
    kb = head_norm(proj(C_BK, C_BV), kg_ref[:, 0:D_KV])
    vb = proj(C_BV, C_Z)
    z_ref[...] = proj(C_Z, C_XBC).astype(bf16)
    xbc_ref[...] = proj(C_XBC, C_DT).astype(bf16)
    dt_ref[...] = _softplus(_dot(hb, wdt_ref[...]) + dtb_ref[...])

    lat = i >= n_p
    cos = jnp.where(lat, cos_ref[...], 1.0)
    se = jnp.where(lat, se_ref[...], 0.0)
    so = jnp.where(lat, so_ref[...], 0.0)
    qa_ref[...] = _rope(qa, cos, se, so).astype(bf16)
    qb_ref[...] = _rope(qb, cos, se, so).astype(bf16)
    aks_ref[...] = _rope(ka, cos[:, :D_KV], se[:, :D_KV], so[:, :D_KV]).astype(bf16)
    bks_ref[...] = _rope(kb, cos[:, :D_KV], se[:, :D_KV], so[:, :D_KV]).astype(bf16)
    avs_ref[...] = va.astype(bf16)
    bvs_ref[...] = vb.astype(bf16)

    @pl.when(i < n_p)
    def _():
        shp = (seqs_per_tile, -1, D_KV)
        akp_ref[...] = ka.reshape(shp)
        avp_ref[...] = va.reshape(shp)
        bkp_ref[...] = kb.reshape(shp)
        bvp_ref[...] = vb.reshape(shp)


def _in_projection(xs, mod, layer, norm_g, w_main, w_dt, qg, kg, dtb, headmat, rope, kv_prev, dims, tm=512):
    batch, seq, dec_batch, dec_seq, depth = dims
    n_p_tok, n_s_tok = batch * seq, dec_batch * dec_seq
    t = n_p_tok + n_s_tok
    til = _Tiling(n_p_tok, n_s_tok, dec_seq, tm)
    spt = tm // seq
    cos, se, so = rope
    n_alias = len(kv_prev)

    def c2(shape):
        return pl.BlockSpec(shape, lambda i: (0, 0))

    def lspec(shape):
        return pl.BlockSpec((None,) + shape, lambda i: (layer,) + (0,) * len(shape))

    rope_spec = pl.BlockSpec((tm, D_Q), lambda i: (til.seq_pos(i), 0))
    tok = lambda w: pl.BlockSpec((tm, w), lambda i: (i, 0))
    kvp = pl.BlockSpec((spt, None, seq, D_KV), lambda i: (til.p_idx(i), layer, 0, 0))
    kvs = pl.BlockSpec((tm, D_KV), lambda i: (til.s_idx(i), 0))
    kvp_shape = jax.ShapeDtypeStruct((batch, depth, seq, D_KV), f32)
    kvs_shape = jax.ShapeDtypeStruct((n_s_tok, D_KV), bf16)
    n_in = len(xs) + 12
    return pl.pallas_call(
        functools.partial(_inproj_kernel, n_x=len(xs), n_alias=n_alias, n_p=til.n_p, seqs_per_tile=spt),
        grid=(til.n,),
        in_specs=_x_specs(til, len(xs) == 2) + [
            _mod_spec(til, layer, 0), _mod_spec(til, layer, 1),
            lspec((1, D_MODEL)), lspec((D_MODEL, C_DT)), lspec((D_MODEL, LANES)),
            lspec((1, D_Q)), lspec((1, D_Q)), lspec((1, LANES)), c2((D_Q, D_Q)),
            rope_spec, rope_spec, rope_spec] + [ANY] * n_alias,
        out_specs=[tok(D_Q), tok(D_Q), kvp, kvp, kvp, kvp, kvs, kvs, kvs, kvs,
                   tok(D_SSM), tok(XBC_DIM), tok(LANES)],
        out_shape=[jax.ShapeDtypeStruct((t, D_Q), bf16), jax.ShapeDtypeStruct((t, D_Q), bf16),
                   kvp_shape, kvp_shape, kvp_shape, kvp_shape,
                   kvs_shape, kvs_shape, kvs_shape, kvs_shape,
                   jax.ShapeDtypeStruct((t, D_SSM), bf16), jax.ShapeDtypeStruct((t, XBC_DIM), bf16),
                   jax.ShapeDtypeStruct((t, LANES), f32)],
        input_output_aliases={n_in + k: 2 + k for k in range(n_alias)},
        compiler_params=_params(("arbitrary",)),
        name="norm_mod_inproj",
    )(*xs, mod, mod, norm_g, w_main, w_dt, qg, kg, dtb, headmat, cos, se, so, *kv_prev)


def _softmax_pv(scores, values, sink):
    m = scores[0].max(axis=-1, keepdims=True)
    for s in scores[1:]:
        m = jnp.maximum(m, s.max(axis=-1, keepdims=True))
    if sink is not None:
        m = jnp.maximum(m, sink)
    den = None
    acc = None
    for s, v in zip(scores, values):
        p = jnp.exp(s - m)
        d = p.sum(axis=-1, keepdims=True)
        o = _dot(p.astype(bf16), v)
        den = d if den is None else den + d
        acc = o if acc is None else acc + o
    if sink is not None:
        den = den + jnp.exp(sink - m)
    return acc / den


def _attn_ctx_kernel(sink_ref, qa_ref, ka_ref, va_ref, qb_ref, kb_ref, vb_ref, o_ref, *, layer):
    scale = HEAD_DIM ** -0.5
    for mixer, (q_ref, k_ref, v_ref) in enumerate(((qa_ref, ka_ref, va_ref), (qb_ref, kb_ref, vb_ref))):
        for kv in range(A_KV):
            ks = slice(kv * HEAD_DIM, (kv + 1) * HEAD_DIM)
            k = k_ref[:, ks].astype(bf16)
            v = v_ref[:, ks].astype(bf16)
            for g in range(A_HEADS // A_KV):
                hd = kv * 2 + g
                hs = slice(hd * HEAD_DIM, (hd + 1) * HEAD_DIM)
                q = q_ref[:, hs] * scale
                s = _dot_nt(q, k)
                sink = sink_ref[layer, hd] if mixer == 0 else None
                o = _softmax_pv([s], [v], sink)
                os_ = slice(mixer * D_Q + hd * HEAD_DIM, mixer * D_Q + (hd + 1) * HEAD_DIM)
                o_ref[:, os_] = o.astype(bf16)


def _attention_ctx(sink, qa, qb, kvp, layer, dims):
    batch, seq, dec_batch, dec_seq, depth = dims
    t = batch * seq + dec_batch * dec_seq
    qspec = pl.BlockSpec((seq, D_Q), lambda b: (b, 0))
    kspec = pl.BlockSpec((None, None, seq, D_KV), lambda b: (b, layer, 0, 0))
    akp, avp, bkp, bvp = kvp
    return pl.pallas_call(
        functools.partial(_attn_ctx_kernel, layer=layer),
        grid=(batch,),
        in_specs=[SMEM, qspec, kspec, kspec, qspec, kspec, kspec],
        out_specs=pl.BlockSpec((seq, D_AB), lambda b: (b, 0)),
        out_shape=jax.ShapeDtypeStruct((t, D_AB), bf16),
        compiler_params=_params(("arbitrary",)),
        name="attn_context",
    )(sink, qa, akp, avp, qb, bkp, bvp)


def _attn_lat_kernel(sink_ref, qa_ref, ka_ref, va_ref, cka_ref, cva_ref,
                     qb_ref, kb_ref, vb_ref, ckb_ref, cvb_ref, alias_ref, o_ref, *, seq, layer):
    del alias_ref
    j = pl.program_id(1)
    scale = HEAD_DIM ** -0.5
    w = WINDOW
    start = pl.multiple_of(jnp.clip((j - 1) * w, 0, seq - 3 * w), w)
    qi = j * w + lax.broadcasted_iota(jnp.int32, (w, 3 * w), 0)
    ki = start + lax.broadcasted_iota(jnp.int32, (w, 3 * w), 1)
    valid = jnp.abs(ki - qi) <= w
    for kv in range(A_KV):
        ks = slice(kv * HEAD_DIM, (kv + 1) * HEAD_DIM)
        ka = ka_ref[pl.ds(start, 3 * w), ks]
        va = va_ref[pl.ds(start, 3 * w), ks]
        cka = cka_ref[:, ks].astype(bf16)
        cva = cva_ref[:, ks].astype(bf16)
        kb = kb_ref[:, ks]
        vb = vb_ref[:, ks]
        ckb = ckb_ref[:, ks].astype(bf16)
        cvb = cvb_ref[:, ks].astype(bf16)
        for g in range(A_HEADS // A_KV):
            hd = kv * 2 + g
            hs = slice(hd * HEAD_DIM, (hd + 1) * HEAD_DIM)
            q = qa_ref[:, hs] * scale
            s_loc = jnp.where(valid, _dot_nt(q, ka), -jnp.inf)
            s_ctx = _dot_nt(q, cka)
            o_ref[:, hs] = _softmax_pv([s_loc, s_ctx], [va, cva], sink_ref[layer, hd]).astype(bf16)
            q = qb_ref[:, hs] * scale
            o = _softmax_pv([_dot_nt(q, kb), _dot_nt(q, ckb)], [vb, cvb], None)
            o_ref[:, slice(D_Q + hd * HEAD_DIM, D_Q + (hd + 1) * HEAD_DIM)] = o.astype(bf16)


def _attention_lat(sink, qa, qb, kvs, caches, oab, layer, dims):
    batch, seq, dec_batch, dec_seq, depth = dims
    w = WINDOW
    nq = dec_seq // w
    q0 = batch * seq // w
    past = caches[0].shape[2]
    qspec = pl.BlockSpec((w, D_Q), lambda b, j: (q0 + b * nq + j, 0))
    kspec = pl.BlockSpec((dec_seq, D_KV), lambda b, j: (b, 0))
    cspec = pl.BlockSpec((None, None, past, D_KV), lambda b, j: (b, layer, 0, 0))
    aks, avs, bks, bvs = kvs
    cka, cva, ckb, cvb = caches
    return pl.pallas_call(
        functools.partial(_attn_lat_kernel, seq=dec_seq, layer=layer),
        grid=(dec_batch, nq),
        in_specs=[SMEM, qspec, kspec, kspec, cspec, cspec, qspec, kspec, kspec, cspec, cspec, ANY],
        out_specs=pl.BlockSpec((w, D_AB), lambda b, j: (q0 + b * nq + j, 0)),
        out_shape=jax.ShapeDtypeStruct(oab.shape, oab.dtype),
        input_output_aliases={11: 0},
        compiler_params=_params(("arbitrary", "arbitrary")),
        name="attn_latent",
    )(sink, qa, aks, avs, cka, cva, qb, bks, bvs, ckb, cvb, oab)


def _ssd_kernel(*refs, latent, n_alias, nseq, n):
    xbc_ref, z_ref, dt_ref, cw_ref, cb_ref, alog_ref, dskip_ref, g_ref, sel_ref = refs[:9]
    if latent:
        h0f_ref, h0b_ref = refs[9:11]
        o_ref = refs[11 + n_alias]
    else:
        o_ref, hf_ref, hb_ref = refs[9 + n_alias:12 + n_alias]
    xc_scr, st_scr, he_scr, lhs_scr, dec_scr = refs[12 + n_alias:]
    nc = n // CHUNK
    L = CHUNK
    ns = SSM_STATE
    nh = SSM_HEADS
    hpg = SSM_HEADS // SSM_GROUPS
    hd_w = SSM_HEAD_DIM

    x = xbc_ref[...].astype(f32)
    t_idx = lax.broadcasted_iota(jnp.int32, x.shape, 0) % n
    prv = jnp.where(t_idx == 0, 0.0, pltpu.roll(x, 1, 0))
    nxt = jnp.where(t_idx == n - 1, 0.0, pltpu.roll(x, nseq * n - 1, 0))
    y = prv * cw_ref[0:1, :] + x * cw_ref[1:2, :] + nxt * cw_ref[2:3, :] + cb_ref[...]
    xc_scr[...] = _silu(y)

    a_neg = -jnp.exp(alog_ref[...])
    r_i = lax.broadcasted_iota(jnp.int32, (L, L), 0)
    c_i = lax.broadcasted_iota(jnp.int32, (L, L), 1)
    lower = r_i >= c_i
    upper = r_i <= c_i
    tril = lower.astype(f32).astype(bf16)

    def split3(v):
        v0 = v.astype(bf16)
        r1 = v - v0.astype(f32)
        v1 = r1.astype(bf16)
        return v0, v1, (r1 - v1.astype(f32)).astype(bf16)

    def prefix_sum(v):
        return sum(_dot(tril, t) for t in split3(v))

    def spread(v):
        return sum(_dot(t, sel_ref[...]) for t in split3(v))

    def stage1(c, carry):
        rows = pl.ds(pl.multiple_of(c * L, L), L)
        xs16 = xc_scr[rows, 0:D_SSM].astype(bf16)
        bm = xc_scr[rows, D_SSM:D_SSM + SSM_GROUPS * ns]
        cm = xc_scr[rows, D_SSM + SSM_GROUPS * ns:XBC_DIM]
        dt = dt_ref[rows, :]
        da = dt * a_neg
        cs = prefix_sum(da)
        tot = cs[L - 1:L, :]
        suf = tot - cs + da
        cs_t, suf_t, dt_t, b_t = cs.T, suf.T, dt.T, bm.T
        tot_c = cs_t[:, L - 1:L]
        wf_t = jnp.exp(tot_c[0:nh] - cs_t[0:nh]) * dt_t[0:nh]
        wb_t = jnp.exp(tot_c[nh:2 * nh] - suf_t[nh:2 * nh]) * dt_t[nh:2 * nh]
        dec_scr[c] = spread(jnp.broadcast_to(jnp.exp(tot), (8, LANES)))
        cm16 = cm.astype(bf16)
        bm16 = bm.astype(bf16)
        for g in range(SSM_GROUPS):
            gs = slice(g * ns, (g + 1) * ns)
            cb = _dot_nt(cm16[:, gs], bm16[:, gs])
            cg = cm[:, gs]
            bg_t = b_t[gs, :]
            for hh in range(hpg):
                hd = g * hpg + hh
                hb_ = nh + hd
                cols = slice(hd * hd_w, (hd + 1) * hd_w)
                lhs1 = jnp.concatenate([bg_t * wf_t[hd:hd + 1, :], bg_t * wb_t[hd:hd + 1, :]], axis=0)
                st_scr[c, :, cols] = _dot(lhs1.astype(bf16), xs16[:, cols])
                csb = jnp.broadcast_to(cs[:, hd:hd + 1], (L, L))
                sfb = jnp.broadcast_to(suf[:, hb_:hb_ + 1], (L, L))
                lf = jnp.exp(jnp.where(lower, csb - cs_t[hd:hd + 1, :], -jnp.inf))
                lb = jnp.exp(jnp.where(upper, sfb - suf_t[hb_:hb_ + 1, :], -jnp.inf))
                m = cb * (lf * dt_t[hd:hd + 1, :] + lb * dt_t[hb_:hb_ + 1, :])
                lhs_scr[c, hd] = jnp.concatenate(
                    [m, cg * jnp.exp(csb[:, 0:ns]), cg * jnp.exp(sfb[:, 0:ns])], axis=1).astype(bf16)
        return carry

    lax.fori_loop(0, nseq * nc, stage1, 0)

    def to_t(h):
        return jnp.concatenate([h, jnp.zeros_like(h)], axis=1).T[0:ns, :]

    def from_t(ht):
        return jnp.concatenate([ht, jnp.zeros_like(ht)], axis=0).T[:, 0:ns]

    for s in range(nseq):
        if latent:
            hf, hb = to_t(h0f_ref[s]), to_t(h0b_ref[s])
        else:
            hf = hb = jnp.zeros((ns, D_SSM), f32)
        for k in range(nc):
            cf = s * nc + k
            cr = s * nc + nc - 1 - k
            he_scr[cf, 0:ns, :] = hf.astype(bf16)
            hf = hf * dec_scr[cf, 0:1, 0:D_SSM] + st_scr[cf, 0:ns, :]
            he_scr[cr, ns:2 * ns, :] = hb.astype(bf16)
            hb = hb * dec_scr[cr, 0:1, D_SSM:2 * D_SSM] + st_scr[cr, ns:2 * ns, :]
        if not latent:
            hf_ref[s] = from_t(hf)
            hb_ref[s] = from_t(hb)

    def stage3(c, carry):
        rows = pl.ds(pl.multiple_of(c * L, L), L)
        xs = xc_scr[rows, 0:D_SSM]
        xs16 = xs.astype(bf16)
        ys = []
        for hd in range(nh):
            cols = slice(hd * hd_w, (hd + 1) * hd_w)
            rhs = jnp.concatenate([xs16[:, cols], he_scr[c, 0:ns, cols], he_scr[c, ns:2 * ns, cols]], axis=0)
            ys.append(_dot(lhs_scr[c, hd], rhs))
        yv = jnp.concatenate(ys, axis=1) + xs * dskip_ref[...]
        yv = yv * _silu(z_ref[rows, :].astype(f32))
        o_ref[rows, :] = (_rms(yv) * g_ref[...]).astype(bf16)
        return carry

    lax.fori_loop(0, nseq * nc, stage3, 0, unroll=2)


def _ssd(xbc, z, dt, consts, layer, dims, *, latent, init=None, oc=None, st_prev=()):
    batch, seq, dec_batch, dec_seq, depth = dims
    t = batch * seq + dec_batch * dec_seq
    if latent:
        n_b, n, nseq = dec_batch, dec_seq, 1
    else:
        n_b, n, nseq = batch, seq, SSD_CTX_SEQS
    rows = nseq * n
    b0 = batch * seq // rows if latent else 0

    def tok(width):
        return pl.BlockSpec((rows, width), lambda b: (b0 + b, 0))

    def lspec(shape):
        return pl.BlockSpec((None,) + shape, lambda b: (layer,) + (0,) * len(shape))

    st = pl.BlockSpec((nseq, None, D_SSM, SSM_STATE), lambda b: (b, layer, 0, 0))
    in_specs = [tok(XBC_DIM), tok(D_SSM), tok(LANES),
                lspec((3, XBC_DIM)), lspec((1, XBC_DIM)), lspec((1, LANES)), lspec((1, D_SSM)),
                lspec((1, D_SSM)), pl.BlockSpec((LANES, 2 * D_SSM), lambda b: (0, 0))]
    args = [xbc, z, dt, *consts, _head_spread_matrix()]
    oc_shape = jax.ShapeDtypeStruct((t, D_SSM), bf16)
    oc_spec = pl.BlockSpec((rows, D_SSM), lambda b: (b0 + b, 0))
    nck = rows // CHUNK
    scratch = [pltpu.VMEM((rows, XBC_DIM), f32),
               pltpu.VMEM((nck, 2 * SSM_STATE, D_SSM), f32),
               pltpu.VMEM((nck, 2 * SSM_STATE, D_SSM), bf16),
               pltpu.VMEM((nck, SSM_HEADS, CHUNK, 2 * CHUNK), bf16),
               pltpu.VMEM((nck, 8, 2 * D_SSM), f32)]
    if latent:
        in_specs += [st, st, ANY]
        args += [init[0], init[1], oc]
        out_specs, out_shape = [oc_spec], [oc_shape]
        aliases = {11: 0}
        n_alias = 1
    else:
        n_alias = len(st_prev)
        in_specs += [ANY] * n_alias
        args += list(st_prev)
        st_shape = jax.ShapeDtypeStruct((batch, depth, D_SSM, SSM_STATE), f32)
        out_specs, out_shape = [oc_spec, st, st], [oc_shape, st_shape, st_shape]
        aliases = {9 + k: 1 + k for k in range(n_alias)}
    return pl.pallas_call(
        functools.partial(_ssd_kernel, latent=latent, n_alias=n_alias, nseq=nseq, n=n),
        grid=(n_b // nseq,),
        in_specs=in_specs, out_specs=out_specs, out_shape=out_shape,
        scratch_shapes=scratch,
        input_output_aliases=aliases,
        compiler_params=_params(("arbitrary",)),
        name="ssd_latent" if latent else "ssd_context",
    )(*args)


def _outproj_kernel(*refs, n_x, n_p):
    x_refs = refs[:n_x]
    (oab_ref, oc_ref, g1_ref, sh_ref, sc_ref, n2_ref, w_ref, wr_ref, br_ref,
     x1_ref, h2_ref, comb_ref) = refs[n_x:]
    i = pl.program_id(0)
    o = _dot(oab_ref[...], w_ref[0:D_AB, :]) + _dot(oc_ref[...], w_ref[D_AB:, :])
    x1 = _load_x(x_refs, i, n_p) + g1_ref[...] * o
    x1_ref[...] = x1
    h2 = _rms(x1) * n2_ref[...]
    h2 = h2 * (1.0 + sc_ref[...]) + sh_ref[...]
    h2b = h2.astype(bf16)
    h2_ref[...] = h2b

    logits = _dot(h2b, wr_ref[...]) + br_ref[...]
    lane = lax.broadcasted_iota(jnp.int32, logits.shape, 1).astype(f32)
    big = float(LANES)
    neg = -jnp.inf
    gmask = (lane >= N_EXPERTS) & (lane < N_EXPERTS + N_EGROUPS)
    gl = jnp.where(gmask, logits, neg)
    gmax = gl.max(axis=-1, keepdims=True)
    gsel = jnp.where(gl == gmax, lane, big).min(axis=-1, keepdims=True) - N_EXPERTS
    gprob = 1.0 / jnp.where(gmask, jnp.exp(logits - gmax), 0.0).sum(axis=-1, keepdims=True)
    emask = (lane >= gsel * EXPERTS_PER_GROUP) & (lane < (gsel + 1) * EXPERTS_PER_GROUP)
    el = jnp.where(emask, logits, neg)
    v1 = el.max(axis=-1, keepdims=True)
    i1 = jnp.where(el == v1, lane, big).min(axis=-1, keepdims=True)
    el2 = jnp.where(lane == i1, neg, el)
    v2 = el2.max(axis=-1, keepdims=True)
    i2 = jnp.where(el2 == v2, lane, big).min(axis=-1, keepdims=True)
    e2 = jnp.exp(v2 - v1)
    den = 1.0 + e2
    comb = jnp.where(lane == i1, gprob / den, 0.0) + jnp.where(lane == i2, gprob * e2 / den, 0.0)
    comb_ref[...] = jnp.where(lane == GSEL_LANE, gsel, comb)


def _out_projection(oab, oc, xs, mod, layer, norm_g, w_out, wr, br, dims, tm=512):
    batch, seq, dec_batch, dec_seq, depth = dims
    n_p_tok, n_s_tok = batch * seq, dec_batch * dec_seq
    t = n_p_tok + n_s_tok
    til = _Tiling(n_p_tok, n_s_tok, dec_seq, tm)

    def lspec(shape):
        return pl.BlockSpec((None,) + shape, lambda i: (layer,) + (0,) * len(shape))

    tok = lambda w: pl.BlockSpec((tm, w), lambda i: (i, 0))
    return pl.pallas_call(
        functools.partial(_outproj_kernel, n_x=len(xs), n_p=til.n_p),
        grid=(til.n,),
        in_specs=_x_specs(til, len(xs) == 2) + [
            tok(D_AB), tok(D_SSM),
            _mod_spec(til, layer, 2), _mod_spec(til, layer, 3), _mod_spec(til, layer, 4),
            lspec((1, D_MODEL)), lspec((D_AB + D_SSM, D_MODEL)), lspec((D_MODEL, LANES)), lspec((1, LANES))],
        out_specs=[tok(D_MODEL), tok(D_MODEL), tok(LANES)],
        out_shape=[jax.ShapeDtypeStruct((t, D_MODEL), f32),
                   jax.ShapeDtypeStruct((t, D_MODEL), bf16),
                   jax.ShapeDtypeStruct((t, LANES), f32)],
        compiler_params=_params(("arbitrary",)),
        name="outproj_norm_router",
    )(*xs, oab, oc, mod, mod, mod, norm_g, w_out, wr, br)


def _expert_ffn(h, weight_of, experts, wg_ref, wu_ref, wd_rows):
    hid = []
    for e in experts:
        a = _dot(h, wg_ref[e])
        u = _dot(h, wu_ref[e])
        hid.append((_silu(a) * u * weight_of(e)).astype(bf16))
    return _dot(jnp.concatenate(hid, axis=1), wd_rows)


def _moe_kernel(h2_ref, comb_ref, wg_ref, wu_ref, wd_ref, x1_ref, g2_ref, fg_ref, *rest, final, n_p):
    outs, y_scr = rest[:-1], rest[-1]
    i = pl.program_id(0)
    k = pl.program_id(1)
    sb, cap = MOE_SB, MOE_CAP
    nsb = MOE_TM // sb
    slots = N_EGROUPS * cap
    gh = EXPERTS_PER_GROUP * D_EXPERT

    @pl.when(k == 0)
    def _():
        comb = comb_ref[...]
        lane = lax.broadcasted_iota(jnp.int32, comb.shape, 1).astype(f32)
        gsel = comb[:, GSEL_LANE:GSEL_LANE + 1]
        mine = lane == gsel
        onehot = mine.astype(f32).astype(bf16)
        r_i = lax.broadcasted_iota(jnp.int32, (sb, sb), 0)
        c_i = lax.broadcasted_iota(jnp.int32, (sb, sb), 1)
        before = (c_i < r_i).astype(f32).astype(bf16)
        ranks = []
        worst = None
        for j in range(nsb):
            rows = slice(j * sb, (j + 1) * sb)
            earlier = _dot(before, onehot[rows])
            own = jnp.where(mine[rows], earlier, 0.0)
            worst = own if worst is None else jnp.maximum(worst, own)
            ranks.append(own.sum(axis=-1, keepdims=True))
        rank = jnp.concatenate(ranks, axis=0)
        fits = jnp.max(worst) < cap

        @pl.when(fits)
        def _():
            hi = comb.astype(bf16)
            lo = (comb - hi.astype(f32)).astype(bf16)
            dest = gsel * cap + rank
            slot_i = lax.broadcasted_iota(jnp.int32, (sb, slots), 1).astype(f32)
            place, packed_h, packed_c = [], [], []
            for j in range(nsb):
                rows = slice(j * sb, (j + 1) * sb)
                pt = (slot_i == dest[rows]).astype(f32).astype(bf16)
                place.append(pt)
                hx = jnp.concatenate([h2_ref[rows, :], hi[rows], lo[rows]], axis=1)
                srt = _dot_tn(pt, hx)
                packed_h.append(srt[:, :D_MODEL].astype(bf16))
                packed_c.append(srt[:, D_MODEL:D_MODEL + LANES] + srt[:, D_MODEL + LANES:])
            ys = []
            for g in range(N_EGROUPS):
                seg = slice(g * cap, (g + 1) * cap)
                hs = jnp.concatenate([p[seg] for p in packed_h], axis=0)
                cw = jnp.concatenate([p[seg] for p in packed_c], axis=0)
                experts = range(g * EXPERTS_PER_GROUP, (g + 1) * EXPERTS_PER_GROUP)
                yg = _expert_ffn(hs, lambda e: cw[:, e:e + 1], experts, wg_ref, wu_ref,
                                 wd_ref[g * gh:(g + 1) * gh, :])
                ys.append(yg.astype(bf16))
            for j in range(nsb):
                ysrt = jnp.concatenate([yg[j * cap:(j + 1) * cap] for yg in ys], axis=0)
                y_scr[j * sb:(j + 1) * sb, :] = _dot(place[j], ysrt).astype(bf16)

        @pl.when(jnp.logical_not(fits))
        def _():
            def body(j, carry):
                rows = pl.ds(pl.multiple_of(j * sb, sb), sb)
                cj = comb_ref[rows, :]
                y = _expert_ffn(h2_ref[rows, :], lambda e: cj[:, e:e + 1], range(N_EXPERTS),
                                wg_ref, wu_ref, wd_ref[...])
                y_scr[rows, :] = y.astype(bf16)
                return carry

            lax.fori_loop(0, nsb, body, 0)

    @pl.when(k > 0)
    def _():
        r0 = pl.multiple_of((k - 1) * MOE_FIN, MOE_FIN)
        x2 = x1_ref[...] + g2_ref[...] * y_scr[pl.ds(r0, MOE_FIN), :].astype(f32)
        if final:
            yp_ref, ys_ref = outs
            x2 = _rms(x2) * fg_ref[...]

            @pl.when(i < n_p)
            def _():
                yp_ref[...] = x2

            @pl.when(i >= n_p)
            def _():
                ys_ref[...] = x2
        else:
            outs[0][...] = x2


def _moe(h2, comb, wg, wu, wd, x1, mod, layer, final_g, dims, final):
    batch, seq, dec_batch, dec_seq, depth = dims
    n_p_tok, n_s_tok = batch * seq, dec_batch * dec_seq
    t = n_p_tok + n_s_tok
    tm = MOE_TM
    til = _Tiling(n_p_tok, n_s_tok, dec_seq, tm)
    nfin = tm // MOE_FIN
    once = pl.Buffered(1)
    tok = lambda w: pl.BlockSpec((tm, w), lambda i, k: (i, 0))

    def fin_block(i, k):
        return i * nfin + jnp.maximum(k - 1, 0)

    fin = lambda idx: pl.BlockSpec((MOE_FIN, D_MODEL), lambda i, k: (idx(i, k), 0))
    if final:
        out_specs = [fin(lambda i, k: jnp.minimum(fin_block(i, k), til.n_p * nfin - 1)),
                     fin(lambda i, k: jnp.maximum(fin_block(i, k) - til.n_p * nfin, 0))]
        out_shape = [jax.ShapeDtypeStruct((n_p_tok, D_MODEL), f32),
                     jax.ShapeDtypeStruct((n_s_tok, D_MODEL), f32)]
    else:
        out_specs = [fin(fin_block)]
        out_shape = [jax.ShapeDtypeStruct((t, D_MODEL), f32)]
    return pl.pallas_call(
        functools.partial(_moe_kernel, final=final, n_p=til.n_p),
        grid=(til.n, 1 + nfin),
        in_specs=[tok(D_MODEL), tok(LANES),
                  pl.BlockSpec((None, N_EXPERTS, D_MODEL, D_EXPERT), lambda i, k: (layer, 0, 0, 0), once),
                  pl.BlockSpec((None, N_EXPERTS, D_MODEL, D_EXPERT), lambda i, k: (layer, 0, 0, 0), once),
                  pl.BlockSpec((None, N_EXPERTS * D_EXPERT, D_MODEL), lambda i, k: (layer, 0, 0), once),
                  fin(fin_block), _mod_spec(til, layer, 5),
                  pl.BlockSpec((1, D_MODEL), lambda i, k: (0, 0))],
        out_specs=out_specs, out_shape=out_shape,
        scratch_shapes=[pltpu.VMEM((tm, D_MODEL), bf16)],
        compiler_params=_params(("arbitrary", "arbitrary")),
        name="moe_ffn",
    )(h2, comb, wg, wu, wd.reshape(depth, N_EXPERTS * D_EXPERT, D_MODEL), x1, mod, final_g)


def _rope_tables(n_rows):
    rows = jnp.repeat(jnp.arange(n_rows), GRID_W).astype(f32)
    cols = jnp.tile(jnp.arange(GRID_W), n_rows).astype(f32)
    inv = ROPE_THETA ** (-jnp.arange(ROPE_QUARTER, dtype=f32) / ROPE_QUARTER)
    ang_r = rows[:, None] * inv
    ang_c = cols[:, None] * inv
    ang = jnp.concatenate([ang_r, ang_r, ang_c, ang_c], axis=-1)
    cos, sin = jnp.cos(ang), jnp.sin(ang)
    even = (np.arange(HEAD_DIM) // ROPE_QUARTER) % 2 == 0
    sin_even = jnp.where(even, -sin, 0.0)
    sin_odd = jnp.where(even, 0.0, sin)
    return tuple(jnp.tile(t, (1, D_Q // HEAD_DIM)) for t in (cos, sin_even, sin_odd))


def _head_sum_matrix():
    m = np.zeros((D_Q, D_Q), np.float32)
    for h in range(D_Q // HEAD_DIM):
        m[h * HEAD_DIM:(h + 1) * HEAD_DIM, h * HEAD_DIM:(h + 1) * HEAD_DIM] = 1.0
    return jnp.asarray(m, dtype=bf16)


def _head_spread_matrix():
    m = np.zeros((LANES, 2 * D_SSM), np.float32)
    for j in range(2 * SSM_HEADS):
        m[j, j * SSM_HEAD_DIM:(j + 1) * SSM_HEAD_DIM] = 1.0
    return jnp.asarray(m, dtype=bf16)


def _pad_last(v, width=LANES):
    pad = [(0, 0)] * (v.ndim - 1) + [(0, width - v.shape[-1])]
    return jnp.pad(v, pad)


@jax.jit
def kernel(x_prompt, x_sample, cache_a_k, cache_a_v, cache_b_k, cache_b_v, state_ssm_fwd, state_ssm_bwd, c, c_ctx, norm1_g, norm2_g, final_norm_g, w_ada, b_ada, w_in, a_sink, q_norm_g, k_norm_g, conv_w, conv_b, dt_bias, a_log, d_skip, ssm_norm_g, w_out, w_router_group, b_router_group, w_router_expert, b_router_expert, w_gate, w_up, w_down):
    batch, seq, _ = x_prompt.shape
    dec_batch, dec_seq, _ = x_sample.shape
    depth = w_in.shape[0]
    past = cache_a_k.shape[2]
    dims = (batch, seq, dec_batch, dec_seq, depth)
    n_p_tok = batch * seq
    n_s_tok = dec_batch * dec_seq

    cvec = jnp.concatenate([c_ctx[None, :], c, jnp.zeros((8 - 1 - dec_batch, D_MODEL), f32)], axis=0)
    mod = _modulation(cvec, w_ada, b_ada).reshape(depth, 8, 6, 1, D_MODEL)

    rope = _rope_tables(dec_seq // GRID_W)
    headmat = _head_sum_matrix()
    w_main = w_in[:, :, :C_DT].astype(bf16)
    w_dt = _pad_last(w_in[:, :, C_DT:]).astype(bf16)
    qg = jnp.tile(q_norm_g, (1, D_Q // HEAD_DIM)).reshape(depth, 1, D_Q)
    kg = jnp.tile(k_norm_g, (1, D_Q // HEAD_DIM)).reshape(depth, 1, D_Q)
    dtb = _pad_last(dt_bias.reshape(depth, 1, 2 * SSM_HEADS))
    n1 = norm1_g.reshape(depth, 1, D_MODEL)
    n2 = norm2_g.reshape(depth, 1, D_MODEL)
    ssd_consts = (conv_w, conv_b.reshape(depth, 1, XBC_DIM),
                  _pad_last(a_log.reshape(depth, 1, 2 * SSM_HEADS)),
                  jnp.repeat(d_skip, SSM_HEAD_DIM, axis=-1).reshape(depth, 1, D_SSM),
                  ssm_norm_g.reshape(depth, 1, D_SSM))
    w_out16 = w_out.astype(bf16)
    wr = _pad_last(jnp.concatenate([w_router_expert, w_router_group], axis=-1)).astype(bf16)
    br = _pad_last(jnp.concatenate([b_router_expert, b_router_group], axis=-1)).reshape(depth, 1, LANES)
    wg16, wu16, wd16 = w_gate.astype(bf16), w_up.astype(bf16), w_down.astype(bf16)
    fg = final_norm_g.reshape(1, D_MODEL)

    caches = tuple(t.reshape(dec_batch, depth, past, D_KV) for t in (cache_a_k, cache_a_v, cache_b_k, cache_b_v))
    init = (state_ssm_fwd.reshape(dec_batch, depth, D_SSM, SSM_STATE),
            state_ssm_bwd.reshape(dec_batch, depth, D_SSM, SSM_STATE))

    xs = (x_prompt.reshape(n_p_tok, D_MODEL), x_sample.reshape(n_s_tok, D_MODEL))
    kvp, states = (), ()
    for l in range(depth):
        outs = _in_projection(xs, mod, l, n1, w_main, w_dt, qg, kg, dtb, headmat, rope, kvp, dims)
        qa, qb = outs[0:2]
        kvp, kvs = tuple(outs[2:6]), tuple(outs[6:10])
        z, xbc, dt = outs[10:13]

        oab = _attention_ctx(a_sink, qa, qb, kvp, l, dims)
        oab = _attention_lat(a_sink, qa, qb, kvs, caches, oab, l, dims)

        oc, hf, hb = _ssd(xbc, z, dt, ssd_consts, l, dims, latent=False, st_prev=states)
        states = (hf, hb)
        oc, = _ssd(xbc, z, dt, ssd_consts, l, dims, latent=True, init=init, oc=oc)

        x1, h2, comb = _out_projection(oab, oc, xs, mod, l, n2, w_out16, wr, br, dims)
        xs = tuple(_moe(h2, comb, wg16, wu16, wd16, x1, mod, l, fg, dims, final=(l == depth - 1)))

    y_prompt = xs[0].reshape(batch, seq, D_MODEL)
    y_sample = xs[1].reshape(dec_batch, dec_seq, D_MODEL)
    kv_shape = (batch, depth, seq, A_KV, HEAD_DIM)
    st_shape = (batch, depth, SSM_HEADS, SSM_HEAD_DIM, SSM_STATE)
    return ((y_prompt, y_sample) + tuple(t.reshape(kv_shape) for t in kvp)
            + tuple(t.reshape(st_shape) for t in states))
```

```python
import functools

import jax
import jax.numpy as jnp
import numpy as np
from jax import lax
from jax.experimental import pallas as pl
from jax.experimental.pallas import tpu as pltpu

f32 = jnp.float32
bf16 = jnp.bfloat16
HIGHEST = lax.Precision.HIGHEST

D_MODEL = 1024
GRID_W = 64
HEAD_DIM = 64
A_HEADS = 4
A_KV = 2
WINDOW = 128
B_HEADS = 4
B_KV = 2
ROPE_THETA = 10000.0
ROPE_QUARTER = HEAD_DIM // 4
SSM_HEADS = 8
SSM_HEAD_DIM = 64
D_SSM = SSM_HEADS * SSM_HEAD_DIM
SSM_GROUPS = 2
SSM_STATE = 64
CHUNK = 128
XBC_DIM = D_SSM + 2 * SSM_GROUPS * SSM_STATE
D_AB = (A_HEADS + B_HEADS) * HEAD_DIM
D_Q = A_HEADS * HEAD_DIM
D_KV = A_KV * HEAD_DIM
N_EGROUPS = 4
EXPERTS_PER_GROUP = 4
N_EXPERTS = N_EGROUPS * EXPERTS_PER_GROUP
D_EXPERT = 256
EPS = 1e-6

LANES = 128
C_AQ, C_AK, C_AV, C_BQ, C_BK, C_BV, C_Z, C_XBC, C_DT = 0, 256, 384, 512, 768, 896, 1024, 1536, 2304

SSD_CTX_SEQS = 4

GSEL_LANE = N_EXPERTS
MOE_TM = 1024
MOE_SB = 256
MOE_CAP = 128
MOE_FIN = 256

VMEM_LIMIT = 56 * 1024 * 1024
ANY = pl.BlockSpec(memory_space=pl.ANY)
SMEM = pl.BlockSpec(memory_space=pltpu.SMEM)


def _params(sem, vmem=VMEM_LIMIT):
    return pltpu.CompilerParams(dimension_semantics=sem, vmem_limit_bytes=vmem)


def _dot(a, b, **kw):
    return jnp.dot(a, b, preferred_element_type=f32, **kw)


def _dot_nt(a, b):
    return lax.dot_general(a, b, (((1,), (1,)), ((), ())), preferred_element_type=f32)


def _dot_tn(a, b):
    return lax.dot_general(a, b, (((0,), (0,)), ((), ())), preferred_element_type=f32)


def _silu(x):
    return (0.5 * x) * (1.0 + jnp.tanh(0.5 * x))


def _softplus(x):
    return jnp.maximum(x, 0.0) + jnp.log1p(jnp.exp(-jnp.abs(x)))


def _rms(x):
    return x * lax.rsqrt(jnp.mean(x * x, axis=-1, keepdims=True) + EPS)


class _Tiling:
    def __init__(self, n_p_tok, n_s_tok, dec_seq, tm):
        self.tm = tm
        self.n_p = n_p_tok // tm
        self.n_s = n_s_tok // tm
        self.n = self.n_p + self.n_s
        self.per_seq = dec_seq // tm

    def p_idx(self, i):
        return jnp.minimum(i, self.n_p - 1)

    def s_idx(self, i):
        return jnp.maximum(i - self.n_p, 0)

    def mod_row(self, i):
        return jnp.where(i < self.n_p, 0, 1 + (i - self.n_p) // self.per_seq)

    def seq_pos(self, i):
        return jnp.where(i < self.n_p, 0, (i - self.n_p) % self.per_seq)


def _x_specs(til, split):
    tm = til.tm
    if split:
        return [pl.BlockSpec((tm, D_MODEL), lambda i, *_: (til.p_idx(i), 0)),
                pl.BlockSpec((tm, D_MODEL), lambda i, *_: (til.s_idx(i), 0))]
    return [pl.BlockSpec((tm, D_MODEL), lambda i, *_: (i, 0))]


def _load_x(refs, i, n_p):
    if len(refs) == 2:
        return jnp.where(i < n_p, refs[0][...], refs[1][...])
    return refs[0][...]


def _mod_spec(til, layer, k):
    return pl.BlockSpec((None, None, None, 1, D_MODEL), lambda i, *_: (layer, til.mod_row(i), k, 0, 0))


def _mod_kernel(c_ref, w_ref, b_ref, o_ref):
    s = _silu(c_ref[...])
    o_ref[...] = _dot(s, w_ref[...], precision=HIGHEST) + b_ref[...]


def _modulation(cvec, w_ada, b_ada):
    depth = w_ada.shape[0]
    n = w_ada.shape[2]
    tn = 1536
    return pl.pallas_call(
        _mod_kernel,
        grid=(depth, n // tn),
        in_specs=[
            pl.BlockSpec((8, D_MODEL), lambda l, j: (0, 0)),
            pl.BlockSpec((None, D_MODEL, tn), lambda l, j: (l, 0, j)),
            pl.BlockSpec((None, 1, tn), lambda l, j: (l, 0, j)),
        ],
        out_specs=pl.BlockSpec((None, 8, tn), lambda l, j: (l, 0, j)),
        out_shape=jax.ShapeDtypeStruct((depth, 8, n), f32),
        compiler_params=_params(("arbitrary", "arbitrary")),
        name="adaln_mod",
    )(cvec, w_ada, b_ada.reshape(depth, 1, n))


def _rope(x, cos, sin_even, sin_odd):
    w = x.shape[-1]
    nxt = pltpu.roll(x, w - ROPE_QUARTER, 1)
    prv = pltpu.roll(x, ROPE_QUARTER, 1)
    return x * cos + nxt * sin_even + prv * sin_odd


def _inproj_kernel(*refs, n_x, n_alias, n_p, seqs_per_tile):
    x_refs = refs[:n_x]
    (sh_ref, sc_ref, g_ref, w_ref, wdt_ref, qg_ref, kg_ref, dtb_ref, hm_ref,
     cos_ref, se_ref, so_ref) = refs[n_x:n_x + 12]
    (qa_ref, qb_ref, akp_ref, avp_ref, bkp_ref, bvp_ref, aks_ref, avs_ref, bks_ref, bvs_ref,
     z_ref, xbc_ref, dt_ref) = refs[n_x + 12 + n_alias:]
    i = pl.program_id(0)
    h = _rms(_load_x(x_refs, i, n_p)) * g_ref[...]
    h = h * (1.0 + sc_ref[...]) + sh_ref[...]
    hb = h.astype(bf16)

    def proj(lo, hi):
        return _dot(hb, w_ref[:, lo:hi])

    def head_norm(t, gain):
        w = t.shape[-1]
        sq = t * t
        hi = sq.astype(bf16)
        lo = (sq - hi.astype(f32)).astype(bf16)
        ms_h = (_dot(hi, hm_ref[0:w, 0:w]) + _dot(lo, hm_ref[0:w, 0:w])) * (1.0 / HEAD_DIM)
        return t * lax.rsqrt(ms_h + EPS) * gain

    qa = proj(C_AQ, C_AK)
    ka = proj(C_AK, C_AV)
    va = proj(C_AV, C_BQ)
    qb = head_norm(proj(C_BQ, C_BK), qg_ref[...])
    kb = head_norm(proj(C_BK, C_BV), kg_ref[:, 0:D_KV])
    vb = proj(C_BV, C_Z)
    z_ref[...] = proj(C_Z, C_XBC).astype(bf16)
    xbc_ref[...] = proj(C_XBC, C_DT).astype(bf16)
    dt_ref[...] = _softplus(_dot(hb, wdt_ref[...]) + dtb_ref[...])

    lat = i >= n_p
    cos = jnp.where(lat, cos_ref[...], 1.0)
    se = jnp.where(lat, se_ref[...], 0.0)
    so = jnp.where(lat, so_ref[...], 0.0)
    qa_ref[...] = _rope(qa, cos, se, so).astype(bf16)
    qb_ref[...] = _rope(qb, cos, se, so).astype(bf16)
    aks_ref[...] = _rope(ka, cos[:, :D_KV], se[:, :D_KV], so[:, :D_KV]).astype(bf16)
    bks_ref[...] = _rope(kb, cos[:, :D_KV], se[:, :D_KV], so[:, :D_KV]).astype(bf16)
    avs_ref[...] = va.astype(bf16)
    bvs_ref[...] = vb.astype(bf16)

    @pl.when(i < n_p)
    def _():
        shp = (seqs_per_tile, -1, D_KV)
        akp_ref[...] = ka.reshape(shp)
        avp_ref[...] = va.reshape(shp)
        bkp_ref[...] = kb.reshape(shp)
        bvp_ref[...] = vb.reshape(shp)


def _in_projection(xs, mod, layer, norm_g, w_main, w_dt, qg, kg, dtb, headmat, rope, kv_prev, dims, tm=512):
    batch, seq, dec_batch, dec_seq, depth = dims
    n_p_tok, n_s_tok = batch * seq, dec_batch * dec_seq
    t = n_p_tok + n_s_tok
    til = _Tiling(n_p_tok, n_s_tok, dec_seq, tm)
    spt = tm // seq
    cos, se, so = rope
    n_alias = len(kv_prev)

    def c2(shape):
        return pl.BlockSpec(shape, lambda i: (0, 0))

    def lspec(shape):
        return pl.BlockSpec((None,) + shape, lambda i: (layer,) + (0,) * len(shape))

    rope_spec = pl.BlockSpec((tm, D_Q), lambda i: (til.seq_pos(i), 0))
    tok = lambda w: pl.BlockSpec((tm, w), lambda i: (i, 0))
    kvp = pl.BlockSpec((spt, None, seq, D_KV), lambda i: (til.p_idx(i), layer, 0, 0))
    kvs = pl.BlockSpec((tm, D_KV), lambda i: (til.s_idx(i), 0))
    kvp_shape = jax.ShapeDtypeStruct((batch, depth, seq, D_KV), f32)
    kvs_shape = jax.ShapeDtypeStruct((n_s_tok, D_KV), bf16)
    n_in = len(xs) + 12
    return pl.pallas_call(
        functools.partial(_inproj_kernel, n_x=len(xs), n_alias=n_alias, n_p=til.n_p, seqs_per_tile=spt),
        grid=(til.n,),
        in_specs=_x_specs(til, len(xs) == 2) + [
            _mod_spec(til, layer, 0), _mod_spec(til, layer, 1),
            lspec((1, D_MODEL)), lspec((D_MODEL, C_DT)), lspec((D_MODEL, LANES)),
            lspec((1, D_Q)), lspec((1, D_Q)), lspec((1, LANES)), c2((D_Q, D_Q)),
            rope_spec, rope_spec, rope_spec] + [ANY] * n_alias,
        out_specs=[tok(D_Q), tok(D_Q), kvp, kvp, kvp, kvp, kvs, kvs, kvs, kvs,
                   tok(D_SSM), tok(XBC_DIM), tok(LANES)],
        out_shape=[jax.ShapeDtypeStruct((t, D_Q), bf16), jax.ShapeDtypeStruct((t, D_Q), bf16),
                   kvp_shape, kvp_shape, kvp_shape, kvp_shape,
                   kvs_shape, kvs_shape, kvs_shape, kvs_shape,
                   jax.ShapeDtypeStruct((t, D_SSM), bf16), jax.ShapeDtypeStruct((t, XBC_DIM), bf16),
                   jax.ShapeDtypeStruct((t, LANES), f32)],
        input_output_aliases={n_in + k: 2 + k for k in range(n_alias)},
        compiler_params=_params(("arbitrary",)),
        name="norm_mod_inproj",
    )(*xs, mod, mod, norm_g, w_main, w_dt, qg, kg, dtb, headmat, cos, se, so, *kv_prev)


def _softmax_pv(scores, values, sink, kv):
    m = scores[0].max(axis=-1, keepdims=True)
    for s in scores[1:]:
        m = jnp.maximum(m, s.max(axis=-1, keepdims=True))
    if sink is not None:
        m = jnp.maximum(m, sink)
    acc = None
    for s, v in zip(scores, values):
        o = _dot(jnp.exp(s - m).astype(bf16), v)
        acc = o if acc is None else acc + o
    if kv == 0:
        num, den = acc[:, :HEAD_DIM], acc[:, HEAD_DIM:HEAD_DIM + 1]
    else:
        num, den = acc[:, HEAD_DIM:], acc[:, 0:1]
    if sink is not None:
        den = den + jnp.exp(sink - m)
    return num / den


def _with_ones(v, kv):
    lane = lax.broadcasted_iota(jnp.int32, v.shape, 1)
    keep = (lane < HEAD_DIM) if kv == 0 else (lane >= HEAD_DIM)
    return jnp.where(keep, v.astype(f32), 1.0).astype(bf16)


def _attn_ctx_kernel(sink_ref, qa_ref, ka_ref, va_ref, qb_ref, kb_ref, vb_ref, o_ref, *, layer):
    scale = HEAD_DIM ** -0.5
    for mixer, (q_ref, k_ref, v_ref) in enumerate(((qa_ref, ka_ref, va_ref), (qb_ref, kb_ref, vb_ref))):
        for kv in range(A_KV):
            ks = slice(kv * HEAD_DIM, (kv + 1) * HEAD_DIM)
            k = k_ref[:, ks].astype(bf16)
            v = _with_ones(v_ref[...], kv)
            for g in range(A_HEADS // A_KV):
                hd = kv * 2 + g
                hs = slice(hd * HEAD_DIM, (hd + 1) * HEAD_DIM)
                q = q_ref[:, hs] * scale
                s = _dot_nt(q, k)
                sink = sink_ref[layer, hd] if mixer == 0 else None
                o = _softmax_pv([s], [v], sink, kv)
                os_ = slice(mixer * D_Q + hd * HEAD_DIM, mixer * D_Q + (hd + 1) * HEAD_DIM)
                o_ref[:, os_] = o.astype(bf16)


def _attention_ctx(sink, qa, qb, kvp, layer, dims):
    batch, seq, dec_batch, dec_seq, depth = dims
    t = batch * seq + dec_batch * dec_seq
    qspec = pl.BlockSpec((seq, D_Q), lambda b: (b, 0))
    kspec = pl.BlockSpec((None, None, seq, D_KV), lambda b: (b, layer, 0, 0))
    akp, avp, bkp, bvp = kvp
    return pl.pallas_call(
        functools.partial(_attn_ctx_kernel, layer=layer),
        grid=(batch,),
        in_specs=[SMEM, qspec, kspec, kspec, qspec, kspec, kspec],
        out_specs=pl.BlockSpec((seq, D_AB), lambda b: (b, 0)),
        out_shape=jax.ShapeDtypeStruct((t, D_AB), bf16),
        compiler_params=_params(("arbitrary",)),
        name="attn_context",
    )(sink, qa, akp, avp, qb, bkp, bvp)


def _attn_lat_kernel(sink_ref, qa_ref, ka_ref, va_ref, cka_ref, cva_ref,
                     qb_ref, kb_ref, vb_ref, ckb_ref, cvb_ref, alias_ref, o_ref, *, seq, layer):
    del alias_ref
    j = pl.program_id(1)
    scale = HEAD_DIM ** -0.5
    w = WINDOW
    start = pl.multiple_of(jnp.clip((j - 1) * w, 0, seq - 3 * w), w)
    qi = j * w + lax.broadcasted_iota(jnp.int32, (w, 3 * w), 0)
    ki = start + lax.broadcasted_iota(jnp.int32, (w, 3 * w), 1)
    valid = jnp.abs(ki - qi) <= w
    for kv in range(A_KV):
        ks = slice(kv * HEAD_DIM, (kv + 1) * HEAD_DIM)
        ka = ka_ref[pl.ds(start, 3 * w), ks]
        va = _with_ones(va_ref[pl.ds(start, 3 * w), :], kv)
        cka = cka_ref[:, ks].astype(bf16)
        cva = _with_ones(cva_ref[...], kv)
        kb = kb_ref[:, ks]
        vb = _with_ones(vb_ref[...], kv)
        ckb = ckb_ref[:, ks].astype(bf16)
        cvb = _with_ones(cvb_ref[...], kv)
        for g in range(A_HEADS // A_KV):
            hd = kv * 2 + g
            hs = slice(hd * HEAD_DIM, (hd + 1) * HEAD_DIM)
            q = qa_ref[:, hs] * scale
            s_loc = jnp.where(valid, _dot_nt(q, ka), -jnp.inf)
            s_ctx = _dot_nt(q, cka)
            o_ref[:, hs] = _softmax_pv([s_loc, s_ctx], [va, cva], sink_ref[layer, hd], kv).astype(bf16)
            q = qb_ref[:, hs] * scale
            o = _softmax_pv([_dot_nt(q, kb), _dot_nt(q, ckb)], [vb, cvb], None, kv)
            o_ref[:, slice(D_Q + hd * HEAD_DIM, D_Q + (hd + 1) * HEAD_DIM)] = o.astype(bf16)


def _attention_lat(sink, qa, qb, kvs, caches, oab, layer, dims):
    batch, seq, dec_batch, dec_seq, depth = dims
    w = WINDOW
    nq = dec_seq // w
    q0 = batch * seq // w
    past = caches[0].shape[2]
    qspec = pl.BlockSpec((w, D_Q), lambda b, j: (q0 + b * nq + j, 0))
    kspec = pl.BlockSpec((dec_seq, D_KV), lambda b, j: (b, 0))
    cspec = pl.BlockSpec((None, None, past, D_KV), lambda b, j: (b, layer, 0, 0))
    aks, avs, bks, bvs = kvs
    cka, cva, ckb, cvb = caches
    return pl.pallas_call(
        functools.partial(_attn_lat_kernel, seq=dec_seq, layer=layer),
        grid=(dec_batch, nq),
        in_specs=[SMEM, qspec, kspec, kspec, cspec, cspec, qspec, kspec, kspec, cspec, cspec, ANY],
        out_specs=pl.BlockSpec((w, D_AB), lambda b, j: (q0 + b * nq + j, 0)),
        out_shape=jax.ShapeDtypeStruct(oab.shape, oab.dtype),
        input_output_aliases={11: 0},
        compiler_params=_params(("arbitrary", "arbitrary")),
        name="attn_latent",
    )(sink, qa, aks, avs, cka, cva, qb, bks, bvs, ckb, cvb, oab)


def _ssd_kernel(*refs, latent, n_alias, nseq, n):
    xbc_ref, z_ref, dt_ref, cw_ref, cb_ref, alog_ref, dskip_ref, g_ref, sel_ref = refs[:9]
    if latent:
        h0f_ref, h0b_ref = refs[9:11]
        o_ref = refs[11 + n_alias]
    else:
        o_ref, hf_ref, hb_ref = refs[9 + n_alias:12 + n_alias]
    xc_scr, st_scr, he_scr, lhs_scr, dec_scr = refs[12 + n_alias:]
    nc = n // CHUNK
    L = CHUNK
    ns = SSM_STATE
    nh = SSM_HEADS
    hpg = SSM_HEADS // SSM_GROUPS
    hd_w = SSM_HEAD_DIM

    x = xbc_ref[...].astype(f32)
    t_idx = lax.broadcasted_iota(jnp.int32, x.shape, 0) % n
    prv = jnp.where(t_idx == 0, 0.0, pltpu.roll(x, 1, 0))
    nxt = jnp.where(t_idx == n - 1, 0.0, pltpu.roll(x, nseq * n - 1, 0))
    y = prv * cw_ref[0:1, :] + x * cw_ref[1:2, :] + nxt * cw_ref[2:3, :] + cb_ref[...]
    xc_scr[...] = _silu(y)

    a_neg = -jnp.exp(alog_ref[...])
    r_i = lax.broadcasted_iota(jnp.int32, (L, L), 0)
    c_i = lax.broadcasted_iota(jnp.int32, (L, L), 1)
    lower = r_i >= c_i
    upper = r_i <= c_i
    tril = lower.astype(f32).astype(bf16)

    def split3(v):
        v0 = v.astype(bf16)
        r1 = v - v0.astype(f32)
        v1 = r1.astype(bf16)
        return v0, v1, (r1 - v1.astype(f32)).astype(bf16)

    def prefix_sum(v):
        return sum(_dot(tril, t) for t in split3(v))

    def spread(v):
        return sum(_dot(t, sel_ref[...]) for t in split3(v))

    def stage1(c, carry):
        rows = pl.ds(pl.multiple_of(c * L, L), L)
        xs16 = xc_scr[rows, 0:D_SSM].astype(bf16)
        bm = xc_scr[rows, D_SSM:D_SSM + SSM_GROUPS * ns]
        cm = xc_scr[rows, D_SSM + SSM_GROUPS * ns:XBC_DIM]
        dt = dt_ref[rows, :]
        da = dt * a_neg
        cs = prefix_sum(da)
        tot = cs[L - 1:L, :]
        suf = tot - cs + da
        cs_t, suf_t, dt_t, b_t = cs.T, suf.T, dt.T, bm.T
        tot_c = cs_t[:, L - 1:L]
        wf_t = jnp.exp(tot_c[0:nh] - cs_t[0:nh]) * dt_t[0:nh]
        wb_t = jnp.exp(tot_c[nh:2 * nh] - suf_t[nh:2 * nh]) * dt_t[nh:2 * nh]
        dec_scr[c] = spread(jnp.broadcast_to(jnp.exp(tot), (8, LANES)))
        cm16 = cm.astype(bf16)
        bm16 = bm.astype(bf16)
        for g in range(SSM_GROUPS):
            gs = slice(g * ns, (g + 1) * ns)
            cb = _dot_nt(cm16[:, gs], bm16[:, gs])
            cg = cm[:, gs]
            bg_t = b_t[gs, :]
            for hh in range(hpg):
                hd = g * hpg + hh
                hb_ = nh + hd
                cols = slice(hd * hd_w, (hd + 1) * hd_w)
                lhs1 = jnp.concatenate([bg_t * wf_t[hd:hd + 1, :], bg_t * wb_t[hd:hd + 1, :]], axis=0)
                st_scr[c, :, cols] = _dot(lhs1.astype(bf16), xs16[:, cols])
                csb = jnp.broadcast_to(cs[:, hd:hd + 1], (L, L))
                sfb = jnp.broadcast_to(suf[:, hb_:hb_ + 1], (L, L))
                lf = jnp.exp(jnp.where(lower, csb - cs_t[hd:hd + 1, :], -jnp.inf))
                lb = jnp.exp(jnp.where(upper, sfb - suf_t[hb_:hb_ + 1, :], -jnp.inf))
                m = cb * (lf * dt_t[hd:hd + 1, :] + lb * dt_t[hb_:hb_ + 1, :])
                lhs_scr[c, hd] = jnp.concatenate(
                    [m, cg * jnp.exp(csb[:, 0:ns]), cg * jnp.exp(sfb[:, 0:ns])], axis=1).astype(bf16)
        return carry

    lax.fori_loop(0, nseq * nc, stage1, 0)

    def to_t(h):
        return jnp.concatenate([h, jnp.zeros_like(h)], axis=1).T[0:ns, :]

    def from_t(ht):
        return jnp.concatenate([ht, jnp.zeros_like(ht)], axis=0).T[:, 0:ns]

    for s in range(nseq):
        if latent:
            hf, hb = to_t(h0f_ref[s]), to_t(h0b_ref[s])
        else:
            hf = hb = jnp.zeros((ns, D_SSM), f32)
        for k in range(nc):
            cf = s * nc + k
            cr = s * nc + nc - 1 - k
            he_scr[cf, 0:ns, :] = hf.astype(bf16)
            hf = hf * dec_scr[cf, 0:1, 0:D_SSM] + st_scr[cf, 0:ns, :]
            he_scr[cr, ns:2 * ns, :] = hb.astype(bf16)
            hb = hb * dec_scr[cr, 0:1, D_SSM:2 * D_SSM] + st_scr[cr, ns:2 * ns, :]
        if not latent:
            hf_ref[s] = from_t(hf)
            hb_ref[s] = from_t(hb)

    def stage3(c, carry):
        rows = pl.ds(pl.multiple_of(c * L, L), L)
        xs = xc_scr[rows, 0:D_SSM]
        xs16 = xs.astype(bf16)
        ys = []
        for hd in range(nh):
            cols = slice(hd * hd_w, (hd + 1) * hd_w)
            rhs = jnp.concatenate([xs16[:, cols], he_scr[c, 0:ns, cols], he_scr[c, ns:2 * ns, cols]], axis=0)
            ys.append(_dot(lhs_scr[c, hd], rhs))
        yv = jnp.concatenate(ys, axis=1) + xs * dskip_ref[...]
        yv = yv * _silu(z_ref[rows, :].astype(f32))
        o_ref[rows, :] = (_rms(yv) * g_ref[...]).astype(bf16)
        return carry

    lax.fori_loop(0, nseq * nc, stage3, 0, unroll=2)


def _ssd(xbc, z, dt, consts, layer, dims, *, latent, init=None, oc=None, st_prev=()):
    batch, seq, dec_batch, dec_seq, depth = dims
    t = batch * seq + dec_batch * dec_seq
    if latent:
        n_b, n, nseq = dec_batch, dec_seq, 1
    else:
        n_b, n, nseq = batch, seq, SSD_CTX_SEQS
    rows = nseq * n
    b0 = batch * seq // rows if latent else 0

    def tok(width):
        return pl.BlockSpec((rows, width), lambda b: (b0 + b, 0))

    def lspec(shape):
        return pl.BlockSpec((None,) + shape, lambda b: (layer,) + (0,) * len(shape))

    st = pl.BlockSpec((nseq, None, D_SSM, SSM_STATE), lambda b: (b, layer, 0, 0))
    in_specs = [tok(XBC_DIM), tok(D_SSM), tok(LANES),
                lspec((3, XBC_DIM)), lspec((1, XBC_DIM)), lspec((1, LANES)), lspec((1, D_SSM)),
                lspec((1, D_SSM)), pl.BlockSpec((LANES, 2 * D_SSM), lambda b: (0, 0))]
    args = [xbc, z, dt, *consts, _head_spread_matrix()]
    oc_shape = jax.ShapeDtypeStruct((t, D_SSM), bf16)
    oc_spec = pl.BlockSpec((rows, D_SSM), lambda b: (b0 + b, 0))
    nck = rows // CHUNK
    scratch = [pltpu.VMEM((rows, XBC_DIM), f32),
               pltpu.VMEM((nck, 2 * SSM_STATE, D_SSM), f32),
               pltpu.VMEM((nck, 2 * SSM_STATE, D_SSM), bf16),
               pltpu.VMEM((nck, SSM_HEADS, CHUNK, 2 * CHUNK), bf16),
               pltpu.VMEM((nck, 8, 2 * D_SSM), f32)]
    if latent:
        in_specs += [st, st, ANY]
        args += [init[0], init[1], oc]
        out_specs, out_shape = [oc_spec], [oc_shape]
        aliases = {11: 0}
        n_alias = 1
    else:
        n_alias = len(st_prev)
        in_specs += [ANY] * n_alias
        args += list(st_prev)
        st_shape = jax.ShapeDtypeStruct((batch, depth, D_SSM, SSM_STATE), f32)
        out_specs, out_shape = [oc_spec, st, st], [oc_shape, st_shape, st_shape]
        aliases = {9 + k: 1 + k for k in range(n_alias)}
    return pl.pallas_call(
        functools.partial(_ssd_kernel, latent=latent, n_alias=n_alias, nseq=nseq, n=n),
        grid=(n_b // nseq,),
        in_specs=in_specs, out_specs=out_specs, out_shape=out_shape,
        scratch_shapes=scratch,
        input_output_aliases=aliases,
        compiler_params=_params(("arbitrary",)),
        name="ssd_latent" if latent else "ssd_context",
    )(*args)


def _outproj_kernel(*refs, n_x, n_p):
    x_refs = refs[:n_x]
    (oab_ref, oc_ref, g1_ref, sh_ref, sc_ref, n2_ref, w_ref, wr_ref, br_ref,
     x1_ref, h2_ref, comb_ref) = refs[n_x:]
    i = pl.program_id(0)
    o = _dot(oab_ref[...], w_ref[0:D_AB, :]) + _dot(oc_ref[...], w_ref[D_AB:, :])
    x1 = _load_x(x_refs, i, n_p) + g1_ref[...] * o
    x1_ref[...] = x1
    h2 = _rms(x1) * n2_ref[...]
    h2 = h2 * (1.0 + sc_ref[...]) + sh_ref[...]
    h2b = h2.astype(bf16)
    h2_ref[...] = h2b

    logits = _dot(h2b, wr_ref[...]) + br_ref[...]
    lane = lax.broadcasted_iota(jnp.int32, logits.shape, 1).astype(f32)
    big = float(LANES)
    neg = -jnp.inf
    gmask = (lane >= N_EXPERTS) & (lane < N_EXPERTS + N_EGROUPS)
    gl = jnp.where(gmask, logits, neg)
    gmax = gl.max(axis=-1, keepdims=True)
    gsel = jnp.where(gl == gmax, lane, big).min(axis=-1, keepdims=True) - N_EXPERTS
    gprob = 1.0 / jnp.where(gmask, jnp.exp(logits - gmax), 0.0).sum(axis=-1, keepdims=True)
    emask = (lane >= gsel * EXPERTS_PER_GROUP) & (lane < (gsel + 1) * EXPERTS_PER_GROUP)
    el = jnp.where(emask, logits, neg)
    v1 = el.max(axis=-1, keepdims=True)
    i1 = jnp.where(el == v1, lane, big).min(axis=-1, keepdims=True)
    el2 = jnp.where(lane == i1, neg, el)
    v2 = el2.max(axis=-1, keepdims=True)
    i2 = jnp.where(el2 == v2, lane, big).min(axis=-1, keepdims=True)
    e2 = jnp.exp(v2 - v1)
    den = 1.0 + e2
    comb = jnp.where(lane == i1, gprob / den, 0.0) + jnp.where(lane == i2, gprob * e2 / den, 0.0)
    comb_ref[...] = jnp.where(lane == GSEL_LANE, gsel, comb)


def _out_projection(oab, oc, xs, mod, layer, norm_g, w_out, wr, br, dims, tm=512):
    batch, seq, dec_batch, dec_seq, depth = dims
    n_p_tok, n_s_tok = batch * seq, dec_batch * dec_seq
    t = n_p_tok + n_s_tok
    til = _Tiling(n_p_tok, n_s_tok, dec_seq, tm)

    def lspec(shape):
        return pl.BlockSpec((None,) + shape, lambda i: (layer,) + (0,) * len(shape))

    tok = lambda w: pl.BlockSpec((tm, w), lambda i: (i, 0))
    return pl.pallas_call(
        functools.partial(_outproj_kernel, n_x=len(xs), n_p=til.n_p),
        grid=(til.n,),
        in_specs=_x_specs(til, len(xs) == 2) + [
            tok(D_AB), tok(D_SSM),
            _mod_spec(til, layer, 2), _mod_spec(til, layer, 3), _mod_spec(til, layer, 4),
            lspec((1, D_MODEL)), lspec((D_AB + D_SSM, D_MODEL)), lspec((D_MODEL, LANES)), lspec((1, LANES))],
        out_specs=[tok(D_MODEL), tok(D_MODEL), tok(LANES)],
        out_shape=[jax.ShapeDtypeStruct((t, D_MODEL), f32),
                   jax.ShapeDtypeStruct((t, D_MODEL), bf16),
                   jax.ShapeDtypeStruct((t, LANES), f32)],
        compiler_params=_params(("arbitrary",)),
        name="outproj_norm_router",
    )(*xs, oab, oc, mod, mod, mod, norm_g, w_out, wr, br)


def _expert_ffn(h, weight_of, experts, wg_ref, wu_ref, wd_rows):
    hid = []
    for e in experts:
        a = _dot(h, wg_ref[e])
        u = _dot(h, wu_ref[e])
        hid.append((_silu(a) * u * weight_of(e)).astype(bf16))
    return _dot(jnp.concatenate(hid, axis=1), wd_rows)


def _moe_kernel(h2_ref, comb_ref, wg_ref, wu_ref, wd_ref, x1_ref, g2_ref, fg_ref, *rest, final, n_p):
    outs, y_scr = rest[:-1], rest[-1]
    i = pl.program_id(0)
    k = pl.program_id(1)
    sb, cap = MOE_SB, MOE_CAP
    nsb = MOE_TM // sb
    slots = N_EGROUPS * cap
    gh = EXPERTS_PER_GROUP * D_EXPERT

    @pl.when(k == 0)
    def _():
        comb = comb_ref[...]
        lane = lax.broadcasted_iota(jnp.int32, comb.shape, 1).astype(f32)
        gsel = comb[:, GSEL_LANE:GSEL_LANE + 1]
        mine = lane == gsel
        onehot = mine.astype(f32).astype(bf16)
        r_i = lax.broadcasted_iota(jnp.int32, (sb, sb), 0)
        c_i = lax.broadcasted_iota(jnp.int32, (sb, sb), 1)
        before = (c_i < r_i).astype(f32).astype(bf16)
        ranks = []
        worst = None
        for j in range(nsb):
            rows = slice(j * sb, (j + 1) * sb)
            earlier = _dot(before, onehot[rows])
            own = jnp.where(mine[rows], earlier, 0.0)
            worst = own if worst is None else jnp.maximum(worst, own)
            ranks.append(own.sum(axis=-1, keepdims=True))
        rank = jnp.concatenate(ranks, axis=0)
        fits = jnp.max(worst) < cap

        @pl.when(fits)
        def _():
            hi = comb.astype(bf16)
            lo = (comb - hi.astype(f32)).astype(bf16)
            dest = gsel * cap + rank
            slot_i = lax.broadcasted_iota(jnp.int32, (sb, slots), 1).astype(f32)
            place, packed_h, packed_c = [], [], []
            for j in range(nsb):
                rows = slice(j * sb, (j + 1) * sb)
                pt = (slot_i == dest[rows]).astype(f32).astype(bf16)
                place.append(pt)
                hx = jnp.concatenate([h2_ref[rows, :], hi[rows], lo[rows]], axis=1)
                srt = _dot_tn(pt, hx)
                packed_h.append(srt[:, :D_MODEL].astype(bf16))
                packed_c.append(srt[:, D_MODEL:D_MODEL + LANES] + srt[:, D_MODEL + LANES:])
            ys = []
            for g in range(N_EGROUPS):
                seg = slice(g * cap, (g + 1) * cap)
                hs = jnp.concatenate([p[seg] for p in packed_h], axis=0)
                cw = jnp.concatenate([p[seg] for p in packed_c], axis=0)
                experts = range(g * EXPERTS_PER_GROUP, (g + 1) * EXPERTS_PER_GROUP)
                yg = _expert_ffn(hs, lambda e: cw[:, e:e + 1], experts, wg_ref, wu_ref,
                                 wd_ref[g * gh:(g + 1) * gh, :])
                ys.append(yg.astype(bf16))
            for j in range(nsb):
                ysrt = jnp.concatenate([yg[j * cap:(j + 1) * cap] for yg in ys], axis=0)
                y_scr[j * sb:(j + 1) * sb, :] = _dot(place[j], ysrt).astype(bf16)

        @pl.when(jnp.logical_not(fits))
        def _():
            def body(j, carry):
                rows = pl.ds(pl.multiple_of(j * sb, sb), sb)
                cj = comb_ref[rows, :]
                y = _expert_ffn(h2_ref[rows, :], lambda e: cj[:, e:e + 1], range(N_EXPERTS),
                                wg_ref, wu_ref, wd_ref[...])
                y_scr[rows, :] = y.astype(bf16)
                return carry

            lax.fori_loop(0, nsb, body, 0)

    @pl.when(k > 0)
    def _():
        r0 = pl.multiple_of((k - 1) * MOE_FIN, MOE_FIN)
        x2 = x1_ref[...] + g2_ref[...] * y_scr[pl.ds(r0, MOE_FIN), :].astype(f32)
        if final:
            yp_ref, ys_ref = outs
            x2 = _rms(x2) * fg_ref[...]

            @pl.when(i < n_p)
            def _():
                yp_ref[...] = x2

            @pl.when(i >= n_p)
            def _():
                ys_ref[...] = x2
        else:
            outs[0][...] = x2


def _moe(h2, comb, wg, wu, wd, x1, mod, layer, final_g, dims, final):
    batch, seq, dec_batch, dec_seq, depth = dims
    n_p_tok, n_s_tok = batch * seq, dec_batch * dec_seq
    t = n_p_tok + n_s_tok
    tm = MOE_TM
    til = _Tiling(n_p_tok, n_s_tok, dec_seq, tm)
    nfin = tm // MOE_FIN
    once = pl.Buffered(1)
    tok = lambda w: pl.BlockSpec((tm, w), lambda i, k: (i, 0))

    def fin_block(i, k):
        return i * nfin + jnp.maximum(k - 1, 0)

    fin = lambda idx: pl.BlockSpec((MOE_FIN, D_MODEL), lambda i, k: (idx(i, k), 0))
    if final:
        out_specs = [fin(lambda i, k: jnp.minimum(fin_block(i, k), til.n_p * nfin - 1)),
                     fin(lambda i, k: jnp.maximum(fin_block(i, k) - til.n_p * nfin, 0))]
        out_shape = [jax.ShapeDtypeStruct((n_p_tok, D_MODEL), f32),
                     jax.ShapeDtypeStruct((n_s_tok, D_MODEL), f32)]
    else:
        out_specs = [fin(fin_block)]
        out_shape = [jax.ShapeDtypeStruct((t, D_MODEL), f32)]
    return pl.pallas_call(
        functools.partial(_moe_kernel, final=final, n_p=til.n_p),
        grid=(til.n, 1 + nfin),
        in_specs=[tok(D_MODEL), tok(LANES),
                  pl.BlockSpec((None, N_EXPERTS, D_MODEL, D_EXPERT), lambda i, k: (layer, 0, 0, 0), once),
                  pl.BlockSpec((None, N_EXPERTS, D_MODEL, D_EXPERT), lambda i, k: (layer, 0, 0, 0), once),
                  pl.BlockSpec((None, N_EXPERTS * D_EXPERT, D_MODEL), lambda i, k: (layer, 0, 0), once),
                  fin(fin_block), _mod_spec(til, layer, 5),
                  pl.BlockSpec((1, D_MODEL), lambda i, k: (0, 0))],
        out_specs=out_specs, out_shape=out_shape,
        scratch_shapes=[pltpu.VMEM((tm, D_MODEL), bf16)],
        compiler_params=_params(("arbitrary", "arbitrary")),
        name="moe_ffn",
    )(h2, comb, wg, wu, wd.reshape(depth, N_EXPERTS * D_EXPERT, D_MODEL), x1, mod, final_g)


def _rope_tables(n_rows):
    rows = jnp.repeat(jnp.arange(n_rows), GRID_W).astype(f32)
    cols = jnp.tile(jnp.arange(GRID_W), n_rows).astype(f32)
    inv = ROPE_THETA ** (-jnp.arange(ROPE_QUARTER, dtype=f32) / ROPE_QUARTER)
    ang_r = rows[:, None] * inv
    ang_c = cols[:, None] * inv
    ang = jnp.concatenate([ang_r, ang_r, ang_c, ang_c], axis=-1)
    cos, sin = jnp.cos(ang), jnp.sin(ang)
    even = (np.arange(HEAD_DIM) // ROPE_QUARTER) % 2 == 0
    sin_even = jnp.where(even, -sin, 0.0)
    sin_odd = jnp.where(even, 0.0, sin)
    return tuple(jnp.tile(t, (1, D_Q // HEAD_DIM)) for t in (cos, sin_even, sin_odd))


def _head_sum_matrix():
    m = np.zeros((D_Q, D_Q), np.float32)
    for h in range(D_Q // HEAD_DIM):
        m[h * HEAD_DIM:(h + 1) * HEAD_DIM, h * HEAD_DIM:(h + 1) * HEAD_DIM] = 1.0
    return jnp.asarray(m, dtype=bf16)


def _head_spread_matrix():
    m = np.zeros((LANES, 2 * D_SSM), np.float32)
    for j in range(2 * SSM_HEADS):
        m[j, j * SSM_HEAD_DIM:(j + 1) * SSM_HEAD_DIM] = 1.0
    return jnp.asarray(m, dtype=bf16)


def _pad_last(v, width=LANES):
    pad = [(0, 0)] * (v.ndim - 1) + [(0, width - v.shape[-1])]
    return jnp.pad(v, pad)


@jax.jit
def kernel(x_prompt, x_sample, cache_a_k, cache_a_v, cache_b_k, cache_b_v, state_ssm_fwd, state_ssm_bwd, c, c_ctx, norm1_g, norm2_g, final_norm_g, w_ada, b_ada, w_in, a_sink, q_norm_g, k_norm_g, conv_w, conv_b, dt_bias, a_log, d_skip, ssm_norm_g, w_out, w_router_group, b_router_group, w_router_expert, b_router_expert, w_gate, w_up, w_down):
    batch, seq, _ = x_prompt.shape
    dec_batch, dec_seq, _ = x_sample.shape
    depth = w_in.shape[0]
    past = cache_a_k.shape[2]
    dims = (batch, seq, dec_batch, dec_seq, depth)
    n_p_tok = batch * seq
    n_s_tok = dec_batch * dec_seq

    cvec = jnp.concatenate([c_ctx[None, :], c, jnp.zeros((8 - 1 - dec_batch, D_MODEL), f32)], axis=0)
    mod = _modulation(cvec, w_ada, b_ada).reshape(depth, 8, 6, 1, D_MODEL)

    rope = _rope_tables(dec_seq // GRID_W)
    headmat = _head_sum_matrix()
    w_main = w_in[:, :, :C_DT].astype(bf16)
    w_dt = _pad_last(w_in[:, :, C_DT:]).astype(bf16)
    qg = jnp.tile(q_norm_g, (1, D_Q // HEAD_DIM)).reshape(depth, 1, D_Q)
    kg = jnp.tile(k_norm_g, (1, D_Q // HEAD_DIM)).reshape(depth, 1, D_Q)
    dtb = _pad_last(dt_bias.reshape(depth, 1, 2 * SSM_HEADS))
    n1 = norm1_g.reshape(depth, 1, D_MODEL)
    n2 = norm2_g.reshape(depth, 1, D_MODEL)
    ssd_consts = (conv_w, conv_b.reshape(depth, 1, XBC_DIM),
                  _pad_last(a_log.reshape(depth, 1, 2 * SSM_HEADS)),
                  jnp.repeat(d_skip, SSM_HEAD_DIM, axis=-1).reshape(depth, 1, D_SSM),
                  ssm_norm_g.reshape(depth, 1, D_SSM))
    w_out16 = w_out.astype(bf16)
    wr = _pad_last(jnp.concatenate([w_router_expert, w_router_group], axis=-1)).astype(bf16)
    br = _pad_last(jnp.concatenate([b_router_expert, b_router_group], axis=-1)).reshape(depth, 1, LANES)
    wg16, wu16, wd16 = w_gate.astype(bf16), w_up.astype(bf16), w_down.astype(bf16)
    fg = final_norm_g.reshape(1, D_MODEL)

    caches = tuple(t.reshape(dec_batch, depth, past, D_KV) for t in (cache_a_k, cache_a_v, cache_b_k, cache_b_v))
    init = (state_ssm_fwd.reshape(dec_batch, depth, D_SSM, SSM_STATE),
            state_ssm_bwd.reshape(dec_batch, depth, D_SSM, SSM_STATE))

    xs = (x_prompt.reshape(n_p_tok, D_MODEL), x_sample.reshape(n_s_tok, D_MODEL))
    kvp, states = (), ()
    for l in range(depth):
        outs = _in_projection(xs, mod, l, n1, w_main, w_dt, qg, kg, dtb, headmat, rope, kvp, dims)
        qa, qb = outs[0:2]
        kvp, kvs = tuple(outs[2:6]), tuple(outs[6:10])
        z, xbc, dt = outs[10:13]

        oab = _attention_ctx(a_sink, qa, qb, kvp, l, dims)
        oab = _attention_lat(a_sink, qa, qb, kvs, caches, oab, l, dims)

        oc, hf, hb = _ssd(xbc, z, dt, ssd_consts, l, dims, latent=False, st_prev=states)
        states = (hf, hb)
        oc, = _ssd(xbc, z, dt, ssd_consts, l, dims, latent=True, init=init, oc=oc)

        x1, h2, comb = _out_projection(oab, oc, xs, mod, l, n2, w_out16, wr, br, dims)
        xs = tuple(_moe(h2, comb, wg16, wu16, wd16, x1, mod, l, fg, dims, final=(l == depth - 1)))

    y_prompt = xs[0].reshape(batch, seq, D_MODEL)
    y_sample = xs[1].reshape(dec_batch, dec_seq, D_MODEL)
    kv_shape = (batch, depth, seq, A_KV, HEAD_DIM)
    st_shape = (batch, depth, SSM_HEADS, SSM_HEAD_DIM, SSM_STATE)
    return ((y_prompt, y_sample) + tuple(t.reshape(kv_shape) for t in kvp)
            + tuple(t.reshape(st_shape) for t in states))
```

```python
import functools

import jax
import jax.numpy as jnp
import numpy as np
from jax import lax
from jax.experimental import pallas as pl
from jax.experimental.pallas import tpu as pltpu

f32 = jnp.float32
bf16 = jnp.bfloat16
HIGHEST = lax.Precision.HIGHEST

D_MODEL = 1024
GRID_W = 64
HEAD_DIM = 64
A_HEADS = 4
A_KV = 2
WINDOW = 128
B_HEADS = 4
B_KV = 2
ROPE_THETA = 10000.0
ROPE_QUARTER = HEAD_DIM // 4
SSM_HEADS = 8
SSM_HEAD_DIM = 64
D_SSM = SSM_HEADS * SSM_HEAD_DIM
SSM_GROUPS = 2
SSM_STATE = 64
CHUNK = 128
XBC_DIM = D_SSM + 2 * SSM_GROUPS * SSM_STATE
D_AB = (A_HEADS + B_HEADS) * HEAD_DIM
D_Q = A_HEADS * HEAD_DIM
D_KV = A_KV * HEAD_DIM
N_EGROUPS = 4
EXPERTS_PER_GROUP = 4
N_EXPERTS = N_EGROUPS * EXPERTS_PER_GROUP
D_EXPERT = 256
EPS = 1e-6

LANES = 128
C_AQ, C_AK, C_AV, C_BQ, C_BK, C_BV, C_Z, C_XBC, C_DT = 0, 256, 384, 512, 768, 896, 1024, 1536, 2304

SSD_CTX_SEQS = 4

GSEL_LANE = N_EXPERTS
MOE_TM = 1024
MOE_SB = 256
MOE_CAP = 128
MOE_FIN = 256

VMEM_LIMIT = 56 * 1024 * 1024
ANY = pl.BlockSpec(memory_space=pl.ANY)
SMEM = pl.BlockSpec(memory_space=pltpu.SMEM)


def _params(sem, vmem=VMEM_LIMIT):
    return pltpu.CompilerParams(dimension_semantics=sem, vmem_limit_bytes=vmem)


def _dot(a, b, **kw):
    return jnp.dot(a, b, preferred_element_type=f32, **kw)


def _dot_nt(a, b):
    return lax.dot_general(a, b, (((1,), (1,)), ((), ())), preferred_element_type=f32)


def _dot_tn(a, b):
    return lax.dot_general(a, b, (((0,), (0,)), ((), ())), preferred_element_type=f32)


def _silu(x):
    return (0.5 * x) * (1.0 + jnp.tanh(0.5 * x))


def _softplus(x):
    return jnp.maximum(x, 0.0) + jnp.log1p(jnp.exp(-jnp.abs(x)))


def _rms(x):
    return x * lax.rsqrt(jnp.mean(x * x, axis=-1, keepdims=True) + EPS)


class _Tiling:
    def __init__(self, n_p_tok, n_s_tok, dec_seq, tm):
        self.tm = tm
        self.n_p = n_p_tok // tm
        self.n_s = n_s_tok // tm
        self.n = self.n_p + self.n_s
        self.per_seq = dec_seq // tm

    def p_idx(self, i):
        return jnp.minimum(i, self.n_p - 1)

    def s_idx(self, i):
        return jnp.maximum(i - self.n_p, 0)

    def mod_row(self, i):
        return jnp.where(i < self.n_p, 0, 1 + (i - self.n_p) // self.per_seq)

    def seq_pos(self, i):
        return jnp.where(i < self.n_p, 0, (i - self.n_p) % self.per_seq)


def _x_specs(til, split):
    tm = til.tm
    if split:
        return [pl.BlockSpec((tm, D_MODEL), lambda i, *_: (til.p_idx(i), 0)),
                pl.BlockSpec((tm, D_MODEL), lambda i, *_: (til.s_idx(i), 0))]
    return [pl.BlockSpec((tm, D_MODEL), lambda i, *_: (i, 0))]


def _load_x(refs, i, n_p):
    if len(refs) == 2:
        return jnp.where(i < n_p, refs[0][...], refs[1][...])
    return refs[0][...]


def _mod_spec(til, layer, k):
    return pl.BlockSpec((None, None, None, 1, D_MODEL), lambda i, *_: (layer, til.mod_row(i), k, 0, 0))


def _mod_kernel(c_ref, w_ref, b_ref, o_ref):
    s = _silu(c_ref[...])
    o_ref[...] = _dot(s, w_ref[...], precision=HIGHEST) + b_ref[...]


def _modulation(cvec, w_ada, b_ada):
    depth = w_ada.shape[0]
    n = w_ada.shape[2]
    tn = 1536
    return pl.pallas_call(
        _mod_kernel,
        grid=(depth, n // tn),
        in_specs=[
            pl.BlockSpec((8, D_MODEL), lambda l, j: (0, 0)),
            pl.BlockSpec((None, D_MODEL, tn), lambda l, j: (l, 0, j)),
            pl.BlockSpec((None, 1, tn), lambda l, j: (l, 0, j)),
        ],
        out_specs=pl.BlockSpec((None, 8, tn), lambda l, j: (l, 0, j)),
        out_shape=jax.ShapeDtypeStruct((depth, 8, n), f32),
        compiler_params=_params(("arbitrary", "arbitrary")),
        name="adaln_mod",
    )(cvec, w_ada, b_ada.reshape(depth, 1, n))


def _rope(x, cos, sin_even, sin_odd):
    w = x.shape[-1]
    nxt = pltpu.roll(x, w - ROPE_QUARTER, 1)
    prv = pltpu.roll(x, ROPE_QUARTER, 1)
    return x * cos + nxt * sin_even + prv * sin_odd


def _inproj_kernel(*refs, n_x, n_alias, n_p, seqs_per_tile):
    x_refs = refs[:n_x]
    (sh_ref, sc_ref, g_ref, w_ref, wdt_ref, qg_ref, kg_ref, dtb_ref, hm_ref,
     cos_ref, se_ref, so_ref) = refs[n_x:n_x + 12]
    (qa_ref, qb_ref, akp_ref, avp_ref, bkp_ref, bvp_ref, aks_ref, avs_ref, bks_ref, bvs_ref,
     z_ref, xbc_ref, dt_ref) = refs[n_x + 12 + n_alias:]
    i = pl.program_id(0)
    h = _rms(_load_x(x_refs, i, n_p)) * g_ref[...]
    h = h * (1.0 + sc_ref[...]) + sh_ref[...]
    hb = h.astype(bf16)

    def proj(lo, hi):
        return _dot(hb, w_ref[:, lo:hi])

    def head_norm(t, gain):
        w = t.shape[-1]
        sq = t * t
        hi = sq.astype(bf16)
        lo = (sq - hi.astype(f32)).astype(bf16)
        ms_h = (_dot(hi, hm_ref[0:w, 0:w]) + _dot(lo, hm_ref[0:w, 0:w])) * (1.0 / HEAD_DIM)
        return t * lax.rsqrt(ms_h + EPS) * gain

    qa = proj(C_AQ, C_AK)
    ka = proj(C_AK, C_AV)
    va = proj(C_AV, C_BQ)
    qb = head_norm(proj(C_BQ, C_BK), qg_ref[...])
    kb = head_norm(proj(C_BK, C_BV), kg_ref[:, 0:D_KV])
    vb = proj(C_BV, C_Z)
    z_ref[...] = proj(C_Z, C_XBC).astype(bf16)
    xbc_ref[...] = proj(C_XBC, C_DT).astype(bf16)
    dt_ref[...] = _softplus(_dot(hb, wdt_ref[...]) + dtb_ref[...])

    lat = i >= n_p
    cos = jnp.where(lat, cos_ref[...], 1.0)
    se = jnp.where(lat, se_ref[...], 0.0)
    so = jnp.where(lat, so_ref[...], 0.0)
    qa_ref[...] = _rope(qa, cos, se, so).astype(bf16)
    qb_ref[...] = _rope(qb, cos, se, so).astype(bf16)
    aks_ref[...] = _rope(ka, cos[:, :D_KV], se[:, :D_KV], so[:, :D_KV]).astype(bf16)
    bks_ref[...] = _rope(kb, cos[:, :D_KV], se[:, :D_KV], so[:, :D_KV]).astype(bf16)
    avs_ref[...] = va.astype(bf16)
    bvs_ref[...] = vb.astype(bf16)

    @pl.when(i < n_p)
    def _():
        shp = (seqs_per_tile, -1, D_KV)
        akp_ref[...] = ka.reshape(shp)
        avp_ref[...] = va.reshape(shp)
        bkp_ref[...] = kb.reshape(shp)
        bvp_ref[...] = vb.reshape(shp)


def _in_projection(xs, mod, layer, norm_g, w_main, w_dt, qg, kg, dtb, headmat, rope, kv_prev, dims, tm=512):
    batch, seq, dec_batch, dec_seq, depth = dims
    n_p_tok, n_s_tok = batch * seq, dec_batch * dec_seq
    t = n_p_tok + n_s_tok
    til = _Tiling(n_p_tok, n_s_tok, dec_seq, tm)
    spt = tm // seq
    cos, se, so = rope
    n_alias = len(kv_prev)

    def c2(shape):
        return pl.BlockSpec(shape, lambda i: (0, 0))

    def lspec(shape):
        return pl.BlockSpec((None,) + shape, lambda i: (layer,) + (0,) * len(shape))

    rope_spec = pl.BlockSpec((tm, D_Q), lambda i: (til.seq_pos(i), 0))
    tok = lambda w: pl.BlockSpec((tm, w), lambda i: (i, 0))
    kvp = pl.BlockSpec((spt, None, seq, D_KV), lambda i: (til.p_idx(i), layer, 0, 0))
    kvs = pl.BlockSpec((tm, D_KV), lambda i: (til.s_idx(i), 0))
    kvp_shape = jax.ShapeDtypeStruct((batch, depth, seq, D_KV), f32)
    kvs_shape = jax.ShapeDtypeStruct((n_s_tok, D_KV), bf16)
    n_in = len(xs) + 12
    return pl.pallas_call(
        functools.partial(_inproj_kernel, n_x=len(xs), n_alias=n_alias, n_p=til.n_p, seqs_per_tile=spt),
        grid=(til.n,),
        in_specs=_x_specs(til, len(xs) == 2) + [
            _mod_spec(til, layer, 0), _mod_spec(til, layer, 1),
            lspec((1, D_MODEL)), lspec((D_MODEL, C_DT)), lspec((D_MODEL, LANES)),
            lspec((1, D_Q)), lspec((1, D_Q)), lspec((1, LANES)), c2((D_Q, D_Q)),
            rope_spec, rope_spec, rope_spec] + [ANY] * n_alias,
        out_specs=[tok(D_Q), tok(D_Q), kvp, kvp, kvp, kvp, kvs, kvs, kvs, kvs,
                   tok(D_SSM), tok(XBC_DIM), tok(LANES)],
        out_shape=[jax.ShapeDtypeStruct((t, D_Q), bf16), jax.ShapeDtypeStruct((t, D_Q), bf16),
                   kvp_shape, kvp_shape, kvp_shape, kvp_shape,
                   kvs_shape, kvs_shape, kvs_shape, kvs_shape,
                   jax.ShapeDtypeStruct((t, D_SSM), bf16), jax.ShapeDtypeStruct((t, XBC_DIM), bf16),
                   jax.ShapeDtypeStruct((t, LANES), f32)],
        input_output_aliases={n_in + k: 2 + k for k in range(n_alias)},
        compiler_params=_params(("arbitrary",)),
        name="norm_mod_inproj",
    )(*xs, mod, mod, norm_g, w_main, w_dt, qg, kg, dtb, headmat, cos, se, so, *kv_prev)


def _softmax_pv(scores, values, sink, kv):
    m = scores[0].max(axis=-1, keepdims=True)
    for s in scores[1:]:
        m = jnp.maximum(m, s.max(axis=-1, keepdims=True))
    if sink is not None:
        m = jnp.maximum(m, sink)
    acc = None
    den = 0.0
    for s, v in zip(scores, values):
        p = jnp.exp(s - m)
        if kv is None:
            den = den + p.sum(axis=-1, keepdims=True)
        o = _dot(p.astype(bf16), v)
        acc = o if acc is None else acc + o
    if kv is None:
        num = acc
    elif kv == 0:
        num, den = acc[:, :HEAD_DIM], acc[:, HEAD_DIM:HEAD_DIM + 1]
    else:
        num, den = acc[:, HEAD_DIM:], acc[:, 0:1]
    if sink is not None:
        den = den + jnp.exp(sink - m)
    return num / den


def _with_ones(v, kv):
    lane = lax.broadcasted_iota(jnp.int32, v.shape, 1)
    keep = (lane < HEAD_DIM) if kv == 0 else (lane >= HEAD_DIM)
    return jnp.where(keep, v.astype(f32), 1.0).astype(bf16)


def _attn_ctx_kernel(sink_ref, qa_ref, ka_ref, va_ref, qb_ref, kb_ref, vb_ref, o_ref, *, layer):
    scale = HEAD_DIM ** -0.5
    for mixer, (q_ref, k_ref, v_ref) in enumerate(((qa_ref, ka_ref, va_ref), (qb_ref, kb_ref, vb_ref))):
        for kv in range(A_KV):
            ks = slice(kv * HEAD_DIM, (kv + 1) * HEAD_DIM)
            k = k_ref[:, ks].astype(bf16)
            v = v_ref[:, ks].astype(bf16)
            for g in range(A_HEADS // A_KV):
                hd = kv * 2 + g
                hs = slice(hd * HEAD_DIM, (hd + 1) * HEAD_DIM)
                q = q_ref[:, hs] * scale
                s = _dot_nt(q, k)
                sink = sink_ref[layer, hd] if mixer == 0 else None
                o = _softmax_pv([s], [v], sink, None)
                os_ = slice(mixer * D_Q + hd * HEAD_DIM, mixer * D_Q + (hd + 1) * HEAD_DIM)
                o_ref[:, os_] = o.astype(bf16)


def _attention_ctx(sink, qa, qb, kvp, layer, dims):
    batch, seq, dec_batch, dec_seq, depth = dims
    t = batch * seq + dec_batch * dec_seq
    qspec = pl.BlockSpec((seq, D_Q), lambda b: (b, 0))
    kspec = pl.BlockSpec((None, None, seq, D_KV), lambda b: (b, layer, 0, 0))
    akp, avp, bkp, bvp = kvp
    return pl.pallas_call(
        functools.partial(_attn_ctx_kernel, layer=layer),
        grid=(batch,),
        in_specs=[SMEM, qspec, kspec, kspec, qspec, kspec, kspec],
        out_specs=pl.BlockSpec((seq, D_AB), lambda b: (b, 0)),
        out_shape=jax.ShapeDtypeStruct((t, D_AB), bf16),
        compiler_params=_params(("arbitrary",)),
        name="attn_context",
    )(sink, qa, akp, avp, qb, bkp, bvp)


def _attn_lat_kernel(sink_ref, qa_ref, ka_ref, va_ref, cka_ref, cva_ref,
                     qb_ref, kb_ref, vb_ref, ckb_ref, cvb_ref, alias_ref, o_ref, *, seq, layer):
    del alias_ref
    j = pl.program_id(1)
    scale = HEAD_DIM ** -0.5
    w = WINDOW
    start = pl.multiple_of(jnp.clip((j - 1) * w, 0, seq - 3 * w), w)
    qi = j * w + lax.broadcasted_iota(jnp.int32, (w, 3 * w), 0)
    ki = start + lax.broadcasted_iota(jnp.int32, (w, 3 * w), 1)
    valid = jnp.abs(ki - qi) <= w
    for kv in range(A_KV):
        ks = slice(kv * HEAD_DIM, (kv + 1) * HEAD_DIM)
        ka = ka_ref[pl.ds(start, 3 * w), ks]
        va = _with_ones(va_ref[pl.ds(start, 3 * w), :], kv)
        cka = cka_ref[:, ks].astype(bf16)
        cva = _with_ones(cva_ref[...], kv)
        kb = kb_ref[:, ks]
        vb = _with_ones(vb_ref[...], kv)
        ckb = ckb_ref[:, ks].astype(bf16)
        cvb = _with_ones(cvb_ref[...], kv)
        for g in range(A_HEADS // A_KV):
            hd = kv * 2 + g
            hs = slice(hd * HEAD_DIM, (hd + 1) * HEAD_DIM)
            q = qa_ref[:, hs] * scale
            s_loc = jnp.where(valid, _dot_nt(q, ka), -jnp.inf)
            s_ctx = _dot_nt(q, cka)
            o_ref[:, hs] = _softmax_pv([s_loc, s_ctx], [va, cva], sink_ref[layer, hd], kv).astype(bf16)
            q = qb_ref[:, hs] * scale
            o = _softmax_pv([_dot_nt(q, kb), _dot_nt(q, ckb)], [vb, cvb], None, kv)
            o_ref[:, slice(D_Q + hd * HEAD_DIM, D_Q + (hd + 1) * HEAD_DIM)] = o.astype(bf16)


def _attention_lat(sink, qa, qb, kvs, caches, oab, layer, dims):
    batch, seq, dec_batch, dec_seq, depth = dims
    w = WINDOW
    nq = dec_seq // w
    q0 = batch * seq // w
    past = caches[0].shape[2]
    qspec = pl.BlockSpec((w, D_Q), lambda b, j: (q0 + b * nq + j, 0))
    kspec = pl.BlockSpec((dec_seq, D_KV), lambda b, j: (b, 0))
    cspec = pl.BlockSpec((None, None, past, D_KV), lambda b, j: (b, layer, 0, 0))
    aks, avs, bks, bvs = kvs
    cka, cva, ckb, cvb = caches
    return pl.pallas_call(
        functools.partial(_attn_lat_kernel, seq=dec_seq, layer=layer),
        grid=(dec_batch, nq),
        in_specs=[SMEM, qspec, kspec, kspec, cspec, cspec, qspec, kspec, kspec, cspec, cspec, ANY],
        out_specs=pl.BlockSpec((w, D_AB), lambda b, j: (q0 + b * nq + j, 0)),
        out_shape=jax.ShapeDtypeStruct(oab.shape, oab.dtype),
        input_output_aliases={11: 0},
        compiler_params=_params(("arbitrary", "arbitrary")),
        name="attn_latent",
    )(sink, qa, aks, avs, cka, cva, qb, bks, bvs, ckb, cvb, oab)


def _ssd_kernel(*refs, latent, n_alias, nseq, n):
    xbc_ref, z_ref, dt_ref, cw_ref, cb_ref, alog_ref, dskip_ref, g_ref, sel_ref = refs[:9]
    if latent:
        h0f_ref, h0b_ref = refs[9:11]
        o_ref = refs[11 + n_alias]
    else:
        o_ref, hf_ref, hb_ref = refs[9 + n_alias:12 + n_alias]
    xc_scr, st_scr, he_scr, lhs_scr, dec_scr = refs[12 + n_alias:]
    nc = n // CHUNK
    L = CHUNK
    ns = SSM_STATE
    nh = SSM_HEADS
    hpg = SSM_HEADS // SSM_GROUPS
    hd_w = SSM_HEAD_DIM

    x = xbc_ref[...].astype(f32)
    t_idx = lax.broadcasted_iota(jnp.int32, x.shape, 0) % n
    prv = jnp.where(t_idx == 0, 0.0, pltpu.roll(x, 1, 0))
    nxt = jnp.where(t_idx == n - 1, 0.0, pltpu.roll(x, nseq * n - 1, 0))
    y = prv * cw_ref[0:1, :] + x * cw_ref[1:2, :] + nxt * cw_ref[2:3, :] + cb_ref[...]
    xc_scr[...] = _silu(y)

    a_neg = -jnp.exp(alog_ref[...])
    r_i = lax.broadcasted_iota(jnp.int32, (L, L), 0)
    c_i = lax.broadcasted_iota(jnp.int32, (L, L), 1)
    lower = r_i >= c_i
    upper = r_i <= c_i
    tril = lower.astype(f32).astype(bf16)

    def split3(v):
        v0 = v.astype(bf16)
        r1 = v - v0.astype(f32)
        v1 = r1.astype(bf16)
        return v0, v1, (r1 - v1.astype(f32)).astype(bf16)

    def prefix_sum(v):
        return sum(_dot(tril, t) for t in split3(v))

    def spread(v):
        return sum(_dot(t, sel_ref[...]) for t in split3(v))

    def stage1(c, carry):
        rows = pl.ds(pl.multiple_of(c * L, L), L)
        xs16 = xc_scr[rows, 0:D_SSM].astype(bf16)
        bm = xc_scr[rows, D_SSM:D_SSM + SSM_GROUPS * ns]
        cm = xc_scr[rows, D_SSM + SSM_GROUPS * ns:XBC_DIM]
        dt = dt_ref[rows, :]
        da = dt * a_neg
        cs = prefix_sum(da)
        tot = cs[L - 1:L, :]
        suf = tot - cs + da
        cs_t, suf_t, dt_t, b_t = cs.T, suf.T, dt.T, bm.T
        tot_c = cs_t[:, L - 1:L]
        wf_t = jnp.exp(tot_c[0:nh] - cs_t[0:nh]) * dt_t[0:nh]
        wb_t = jnp.exp(tot_c[nh:2 * nh] - suf_t[nh:2 * nh]) * dt_t[nh:2 * nh]
        dec_scr[c] = spread(jnp.broadcast_to(jnp.exp(tot), (8, LANES)))
        cm16 = cm.astype(bf16)
        bm16 = bm.astype(bf16)
        for g in range(SSM_GROUPS):
            gs = slice(g * ns, (g + 1) * ns)
            cb = _dot_nt(cm16[:, gs], bm16[:, gs])
            cg = cm[:, gs]
            bg_t = b_t[gs, :]
            for hh in range(hpg):
                hd = g * hpg + hh
                hb_ = nh + hd
                cols = slice(hd * hd_w, (hd + 1) * hd_w)
                lhs1 = jnp.concatenate([bg_t * wf_t[hd:hd + 1, :], bg_t * wb_t[hd:hd + 1, :]], axis=0)
                st_scr[c, :, cols] = _dot(lhs1.astype(bf16), xs16[:, cols])
                csb = jnp.broadcast_to(cs[:, hd:hd + 1], (L, L))
                sfb = jnp.broadcast_to(suf[:, hb_:hb_ + 1], (L, L))
                lf = jnp.exp(jnp.where(lower, csb - cs_t[hd:hd + 1, :], -jnp.inf))
                lb = jnp.exp(jnp.where(upper, sfb - suf_t[hb_:hb_ + 1, :], -jnp.inf))
                m = cb * (lf * dt_t[hd:hd + 1, :] + lb * dt_t[hb_:hb_ + 1, :])
                lhs_scr[c, hd] = jnp.concatenate(
                    [m, cg * jnp.exp(csb[:, 0:ns]), cg * jnp.exp(sfb[:, 0:ns])], axis=1).astype(bf16)
        return carry

    lax.fori_loop(0, nseq * nc, stage1, 0)

    def to_t(h):
        return jnp.concatenate([h, jnp.zeros_like(h)], axis=1).T[0:ns, :]

    def from_t(ht):
        return jnp.concatenate([ht, jnp.zeros_like(ht)], axis=0).T[:, 0:ns]

    for s in range(nseq):
        if latent:
            hf, hb = to_t(h0f_ref[s]), to_t(h0b_ref[s])
        else:
            hf = hb = jnp.zeros((ns, D_SSM), f32)
        for k in range(nc):
            cf = s * nc + k
            cr = s * nc + nc - 1 - k
            he_scr[cf, 0:ns, :] = hf.astype(bf16)
            hf = hf * dec_scr[cf, 0:1, 0:D_SSM] + st_scr[cf, 0:ns, :]
            he_scr[cr, ns:2 * ns, :] = hb.astype(bf16)
            hb = hb * dec_scr[cr, 0:1, D_SSM:2 * D_SSM] + st_scr[cr, ns:2 * ns, :]
        if not latent:
            hf_ref[s] = from_t(hf)
            hb_ref[s] = from_t(hb)

    def stage3(c, carry):
        rows = pl.ds(pl.multiple_of(c * L, L), L)
        xs = xc_scr[rows, 0:D_SSM]
        xs16 = xs.astype(bf16)
        ys = []
        for hd in range(nh):
            cols = slice(hd * hd_w, (hd + 1) * hd_w)
            rhs = jnp.concatenate([xs16[:, cols], he_scr[c, 0:ns, cols], he_scr[c, ns:2 * ns, cols]], axis=0)
            ys.append(_dot(lhs_scr[c, hd], rhs))
        yv = jnp.concatenate(ys, axis=1) + xs * dskip_ref[...]
        yv = yv * _silu(z_ref[rows, :].astype(f32))
        o_ref[rows, :] = (_rms(yv) * g_ref[...]).astype(bf16)
        return carry

    lax.fori_loop(0, nseq * nc, stage3, 0, unroll=2)


def _ssd(xbc, z, dt, consts, layer, dims, *, latent, init=None, oc=None, st_prev=()):
    batch, seq, dec_batch, dec_seq, depth = dims
    t = batch * seq + dec_batch * dec_seq
    if latent:
        n_b, n, nseq = dec_batch, dec_seq, 1
    else:
        n_b, n, nseq = batch, seq, SSD_CTX_SEQS
    rows = nseq * n
    b0 = batch * seq // rows if latent else 0

    def tok(width):
        return pl.BlockSpec((rows, width), lambda b: (b0 + b, 0))

    def lspec(shape):
        return pl.BlockSpec((None,) + shape, lambda b: (layer,) + (0,) * len(shape))

    st = pl.BlockSpec((nseq, None, D_SSM, SSM_STATE), lambda b: (b, layer, 0, 0))
    in_specs = [tok(XBC_DIM), tok(D_SSM), tok(LANES),
                lspec((3, XBC_DIM)), lspec((1, XBC_DIM)), lspec((1, LANES)), lspec((1, D_SSM)),
                lspec((1, D_SSM)), pl.BlockSpec((LANES, 2 * D_SSM), lambda b: (0, 0))]
    args = [xbc, z, dt, *consts, _head_spread_matrix()]
    oc_shape = jax.ShapeDtypeStruct((t, D_SSM), bf16)
    oc_spec = pl.BlockSpec((rows, D_SSM), lambda b: (b0 + b, 0))
    nck = rows // CHUNK
    scratch = [pltpu.VMEM((rows, XBC_DIM), f32),
               pltpu.VMEM((nck, 2 * SSM_STATE, D_SSM), f32),
               pltpu.VMEM((nck, 2 * SSM_STATE, D_SSM), bf16),
               pltpu.VMEM((nck, SSM_HEADS, CHUNK, 2 * CHUNK), bf16),
               pltpu.VMEM((nck, 8, 2 * D_SSM), f32)]
    if latent:
        in_specs += [st, st, ANY]
        args += [init[0], init[1], oc]
        out_specs, out_shape = [oc_spec], [oc_shape]
        aliases = {11: 0}
        n_alias = 1
    else:
        n_alias = len(st_prev)
        in_specs += [ANY] * n_alias
        args += list(st_prev)
        st_shape = jax.ShapeDtypeStruct((batch, depth, D_SSM, SSM_STATE), f32)
        out_specs, out_shape = [oc_spec, st, st], [oc_shape, st_shape, st_shape]
        aliases = {9 + k: 1 + k for k in range(n_alias)}
    return pl.pallas_call(
        functools.partial(_ssd_kernel, latent=latent, n_alias=n_alias, nseq=nseq, n=n),
        grid=(n_b // nseq,),
        in_specs=in_specs, out_specs=out_specs, out_shape=out_shape,
        scratch_shapes=scratch,
        input_output_aliases=aliases,
        compiler_params=_params(("arbitrary",)),
        name="ssd_latent" if latent else "ssd_context",
    )(*args)


def _outproj_kernel(*refs, n_x, n_p):
    x_refs = refs[:n_x]
    (oab_ref, oc_ref, g1_ref, sh_ref, sc_ref, n2_ref, w_ref, wr_ref, br_ref,
     x1_ref, h2_ref, comb_ref) = refs[n_x:]
    i = pl.program_id(0)
    o = _dot(oab_ref[...], w_ref[0:D_AB, :]) + _dot(oc_ref[...], w_ref[D_AB:, :])
    x1 = _load_x(x_refs, i, n_p) + g1_ref[...] * o
    x1_ref[...] = x1
    h2 = _rms(x1) * n2_ref[...]
    h2 = h2 * (1.0 + sc_ref[...]) + sh_ref[...]
    h2b = h2.astype(bf16)
    h2_ref[...] = h2b

    logits = _dot(h2b, wr_ref[...]) + br_ref[...]
    lane = lax.broadcasted_iota(jnp.int32, logits.shape, 1).astype(f32)
    big = float(LANES)
    neg = -jnp.inf
    gmask = (lane >= N_EXPERTS) & (lane < N_EXPERTS + N_EGROUPS)
    gl = jnp.where(gmask, logits, neg)
    gmax = gl.max(axis=-1, keepdims=True)
    gsel = jnp.where(gl == gmax, lane, big).min(axis=-1, keepdims=True) - N_EXPERTS
    gprob = 1.0 / jnp.where(gmask, jnp.exp(logits - gmax), 0.0).sum(axis=-1, keepdims=True)
    emask = (lane >= gsel * EXPERTS_PER_GROUP) & (lane < (gsel + 1) * EXPERTS_PER_GROUP)
    el = jnp.where(emask, logits, neg)
    v1 = el.max(axis=-1, keepdims=True)
    i1 = jnp.where(el == v1, lane, big).min(axis=-1, keepdims=True)
    el2 = jnp.where(lane == i1, neg, el)
    v2 = el2.max(axis=-1, keepdims=True)
    i2 = jnp.where(el2 == v2, lane, big).min(axis=-1, keepdims=True)
    e2 = jnp.exp(v2 - v1)
    den = 1.0 + e2
    comb = jnp.where(lane == i1, gprob / den, 0.0) + jnp.where(lane == i2, gprob * e2 / den, 0.0)
    comb_ref[...] = jnp.where(lane == GSEL_LANE, gsel, comb)


def _out_projection(oab, oc, xs, mod, layer, norm_g, w_out, wr, br, dims, tm=512):
    batch, seq, dec_batch, dec_seq, depth = dims
    n_p_tok, n_s_tok = batch * seq, dec_batch * dec_seq
    t = n_p_tok + n_s_tok
    til = _Tiling(n_p_tok, n_s_tok, dec_seq, tm)

    def lspec(shape):
        return pl.BlockSpec((None,) + shape, lambda i: (layer,) + (0,) * len(shape))

    tok = lambda w: pl.BlockSpec((tm, w), lambda i: (i, 0))
    return pl.pallas_call(
        functools.partial(_outproj_kernel, n_x=len(xs), n_p=til.n_p),
        grid=(til.n,),
        in_specs=_x_specs(til, len(xs) == 2) + [
            tok(D_AB), tok(D_SSM),
            _mod_spec(til, layer, 2), _mod_spec(til, layer, 3), _mod_spec(til, layer, 4),
            lspec((1, D_MODEL)), lspec((D_AB + D_SSM, D_MODEL)), lspec((D_MODEL, LANES)), lspec((1, LANES))],
        out_specs=[tok(D_MODEL), tok(D_MODEL), tok(LANES)],
        out_shape=[jax.ShapeDtypeStruct((t, D_MODEL), f32),
                   jax.ShapeDtypeStruct((t, D_MODEL), bf16),
                   jax.ShapeDtypeStruct((t, LANES), f32)],
        compiler_params=_params(("arbitrary",)),
        name="outproj_norm_router",
    )(*xs, oab, oc, mod, mod, mod, norm_g, w_out, wr, br)


def _expert_ffn(h, weight_of, experts, wg_ref, wu_ref, wd_rows):
    hid = []
    for e in experts:
        a = _dot(h, wg_ref[e])
        u = _dot(h, wu_ref[e])
        hid.append((_silu(a) * u * weight_of(e)).astype(bf16))
    return _dot(jnp.concatenate(hid, axis=1), wd_rows)


def _moe_kernel(h2_ref, comb_ref, wg_ref, wu_ref, wd_ref, x1_ref, g2_ref, fg_ref, *rest, final, n_p):
    outs, y_scr = rest[:-1], rest[-1]
    i = pl.program_id(0)
    k = pl.program_id(1)
    sb, cap = MOE_SB, MOE_CAP
    nsb = MOE_TM // sb
    slots = N_EGROUPS * cap
    gh = EXPERTS_PER_GROUP * D_EXPERT

    @pl.when(k == 0)
    def _():
        comb = comb_ref[...]
        lane = lax.broadcasted_iota(jnp.int32, comb.shape, 1).astype(f32)
        gsel = comb[:, GSEL_LANE:GSEL_LANE + 1]
        mine = lane == gsel
        onehot = mine.astype(f32).astype(bf16)
        r_i = lax.broadcasted_iota(jnp.int32, (sb, sb), 0)
        c_i = lax.broadcasted_iota(jnp.int32, (sb, sb), 1)
        before = (c_i < r_i).astype(f32).astype(bf16)
        ranks = []
        worst = None
        for j in range(nsb):
            rows = slice(j * sb, (j + 1) * sb)
            earlier = _dot(before, onehot[rows])
            own = jnp.where(mine[rows], earlier, 0.0)
            worst = own if worst is None else jnp.maximum(worst, own)
            ranks.append(own.sum(axis=-1, keepdims=True))
        rank = jnp.concatenate(ranks, axis=0)
        fits = jnp.max(worst) < cap

        @pl.when(fits)
        def _():
            hi = comb.astype(bf16)
            lo = (comb - hi.astype(f32)).astype(bf16)
            dest = gsel * cap + rank
            slot_i = lax.broadcasted_iota(jnp.int32, (sb, slots), 1).astype(f32)
            place, packed_h, packed_c = [], [], []
            for j in range(nsb):
                rows = slice(j * sb, (j + 1) * sb)
                pt = (slot_i == dest[rows]).astype(f32).astype(bf16)
                place.append(pt)
                hx = jnp.concatenate([h2_ref[rows, :], hi[rows], lo[rows]], axis=1)
                srt = _dot_tn(pt, hx)
                packed_h.append(srt[:, :D_MODEL].astype(bf16))
                packed_c.append(srt[:, D_MODEL:D_MODEL + LANES] + srt[:, D_MODEL + LANES:])
            ys = []
            for g in range(N_EGROUPS):
                seg = slice(g * cap, (g + 1) * cap)
                hs = jnp.concatenate([p[seg] for p in packed_h], axis=0)
                cw = jnp.concatenate([p[seg] for p in packed_c], axis=0)
                experts = range(g * EXPERTS_PER_GROUP, (g + 1) * EXPERTS_PER_GROUP)
                yg = _expert_ffn(hs, lambda e: cw[:, e:e + 1], experts, wg_ref, wu_ref,
                                 wd_ref[g * gh:(g + 1) * gh, :])
                ys.append(yg.astype(bf16))
            for j in range(nsb):
                ysrt = jnp.concatenate([yg[j * cap:(j + 1) * cap] for yg in ys], axis=0)
                y_scr[j * sb:(j + 1) * sb, :] = _dot(place[j], ysrt).astype(bf16)

        @pl.when(jnp.logical_not(fits))
        def _():
            def body(j, carry):
                rows = pl.ds(pl.multiple_of(j * sb, sb), sb)
                cj = comb_ref[rows, :]
                y = _expert_ffn(h2_ref[rows, :], lambda e: cj[:, e:e + 1], range(N_EXPERTS),
                                wg_ref, wu_ref, wd_ref[...])
                y_scr[rows, :] = y.astype(bf16)
                return carry

            lax.fori_loop(0, nsb, body, 0)

    @pl.when(k > 0)
    def _():
        r0 = pl.multiple_of((k - 1) * MOE_FIN, MOE_FIN)
        x2 = x1_ref[...] + g2_ref[...] * y_scr[pl.ds(r0, MOE_FIN), :].astype(f32)
        if final:
            yp_ref, ys_ref = outs
            x2 = _rms(x2) * fg_ref[...]

            @pl.when(i < n_p)
            def _():
                yp_ref[...] = x2

            @pl.when(i >= n_p)
            def _():
                ys_ref[...] = x2
        else:
            outs[0][...] = x2


def _moe(h2, comb, wg, wu, wd, x1, mod, layer, final_g, dims, final):
    batch, seq, dec_batch, dec_seq, depth = dims
    n_p_tok, n_s_tok = batch * seq, dec_batch * dec_seq
    t = n_p_tok + n_s_tok
    tm = MOE_TM
    til = _Tiling(n_p_tok, n_s_tok, dec_seq, tm)
    nfin = tm // MOE_FIN
    once = pl.Buffered(1)
    tok = lambda w: pl.BlockSpec((tm, w), lambda i, k: (i, 0))

    def fin_block(i, k):
        return i * nfin + jnp.maximum(k - 1, 0)

    fin = lambda idx: pl.BlockSpec((MOE_FIN, D_MODEL), lambda i, k: (idx(i, k), 0))
    if final:
        out_specs = [fin(lambda i, k: jnp.minimum(fin_block(i, k), til.n_p * nfin - 1)),
                     fin(lambda i, k: jnp.maximum(fin_block(i, k) - til.n_p * nfin, 0))]
        out_shape = [jax.ShapeDtypeStruct((n_p_tok, D_MODEL), f32),
                     jax.ShapeDtypeStruct((n_s_tok, D_MODEL), f32)]
    else:
        out_specs = [fin(fin_block)]
        out_shape = [jax.ShapeDtypeStruct((t, D_MODEL), f32)]
    return pl.pallas_call(
        functools.partial(_moe_kernel, final=final, n_p=til.n_p),
        grid=(til.n, 1 + nfin),
        in_specs=[tok(D_MODEL), tok(LANES),
                  pl.BlockSpec((None, N_EXPERTS, D_MODEL, D_EXPERT), lambda i, k: (layer, 0, 0, 0), once),
                  pl.BlockSpec((None, N_EXPERTS, D_MODEL, D_EXPERT), lambda i, k: (layer, 0, 0, 0), once),
                  pl.BlockSpec((None, N_EXPERTS * D_EXPERT, D_MODEL), lambda i, k: (layer, 0, 0), once),
                  fin(fin_block), _mod_spec(til, layer, 5),
                  pl.BlockSpec((1, D_MODEL), lambda i, k: (0, 0))],
        out_specs=out_specs, out_shape=out_shape,
        scratch_shapes=[pltpu.VMEM((tm, D_MODEL), bf16)],
        compiler_params=_params(("arbitrary", "arbitrary")),
        name="moe_ffn",
    )(h2, comb, wg, wu, wd.reshape(depth, N_EXPERTS * D_EXPERT, D_MODEL), x1, mod, final_g)


def _rope_tables(n_rows):
    rows = jnp.repeat(jnp.arange(n_rows), GRID_W).astype(f32)
    cols = jnp.tile(jnp.arange(GRID_W), n_rows).astype(f32)
    inv = ROPE_THETA ** (-jnp.arange(ROPE_QUARTER, dtype=f32) / ROPE_QUARTER)
    ang_r = rows[:, None] * inv
    ang_c = cols[:, None] * inv
    ang = jnp.concatenate([ang_r, ang_r, ang_c, ang_c], axis=-1)
    cos, sin = jnp.cos(ang), jnp.sin(ang)
    even = (np.arange(HEAD_DIM) // ROPE_QUARTER) % 2 == 0
    sin_even = jnp.where(even, -sin, 0.0)
    sin_odd = jnp.where(even, 0.0, sin)
    return tuple(jnp.tile(t, (1, D_Q // HEAD_DIM)) for t in (cos, sin_even, sin_odd))


def _head_sum_matrix():
    m = np.zeros((D_Q, D_Q), np.float32)
    for h in range(D_Q // HEAD_DIM):
        m[h * HEAD_DIM:(h + 1) * HEAD_DIM, h * HEAD_DIM:(h + 1) * HEAD_DIM] = 1.0
    return jnp.asarray(m, dtype=bf16)


def _head_spread_matrix():
    m = np.zeros((LANES, 2 * D_SSM), np.float32)
    for j in range(2 * SSM_HEADS):
        m[j, j * SSM_HEAD_DIM:(j + 1) * SSM_HEAD_DIM] = 1.0
    return jnp.asarray(m, dtype=bf16)


def _pad_last(v, width=LANES):
    pad = [(0, 0)] * (v.ndim - 1) + [(0, width - v.shape[-1])]
    return jnp.pad(v, pad)


@jax.jit
def kernel(x_prompt, x_sample, cache_a_k, cache_a_v, cache_b_k, cache_b_v, state_ssm_fwd, state_ssm_bwd, c, c_ctx, norm1_g, norm2_g, final_norm_g, w_ada, b_ada, w_in, a_sink, q_norm_g, k_norm_g, conv_w, conv_b, dt_bias, a_log, d_skip, ssm_norm_g, w_out, w_router_group, b_router_group, w_router_expert, b_router_expert, w_gate, w_up, w_down):
    batch, seq, _ = x_prompt.shape
    dec_batch, dec_seq, _ = x_sample.shape
    depth = w_in.shape[0]
    past = cache_a_k.shape[2]
    dims = (batch, seq, dec_batch, dec_seq, depth)
    n_p_tok = batch * seq
    n_s_tok = dec_batch * dec_seq

    cvec = jnp.concatenate([c_ctx[None, :], c, jnp.zeros((8 - 1 - dec_batch, D_MODEL), f32)], axis=0)
    mod = _modulation(cvec, w_ada, b_ada).reshape(depth, 8, 6, 1, D_MODEL)

    rope = _rope_tables(dec_seq // GRID_W)
    headmat = _head_sum_matrix()
    w_main = w_in[:, :, :C_DT].astype(bf16)
    w_dt = _pad_last(w_in[:, :, C_DT:]).astype(bf16)
    qg = jnp.tile(q_norm_g, (1, D_Q // HEAD_DIM)).reshape(depth, 1, D_Q)
    kg = jnp.tile(k_norm_g, (1, D_Q // HEAD_DIM)).reshape(depth, 1, D_Q)
    dtb = _pad_last(dt_bias.reshape(depth, 1, 2 * SSM_HEADS))
    n1 = norm1_g.reshape(depth, 1, D_MODEL)
    n2 = norm2_g.reshape(depth, 1, D_MODEL)
    ssd_consts = (conv_w, conv_b.reshape(depth, 1, XBC_DIM),
                  _pad_last(a_log.reshape(depth, 1, 2 * SSM_HEADS)),
                  jnp.repeat(d_skip, SSM_HEAD_DIM, axis=-1).reshape(depth, 1, D_SSM),
                  ssm_norm_g.reshape(depth, 1, D_SSM))
    w_out16 = w_out.astype(bf16)
    wr = _pad_last(jnp.concatenate([w_router_expert, w_router_group], axis=-1)).astype(bf16)
    br = _pad_last(jnp.concatenate([b_router_expert, b_router_group], axis=-1)).reshape(depth, 1, LANES)
    wg16, wu16, wd16 = w_gate.astype(bf16), w_up.astype(bf16), w_down.astype(bf16)
    fg = final_norm_g.reshape(1, D_MODEL)

    caches = tuple(t.reshape(dec_batch, depth, past, D_KV) for t in (cache_a_k, cache_a_v, cache_b_k, cache_b_v))
    init = (state_ssm_fwd.reshape(dec_batch, depth, D_SSM, SSM_STATE),
            state_ssm_bwd.reshape(dec_batch, depth, D_SSM, SSM_STATE))

    xs = (x_prompt.reshape(n_p_tok, D_MODEL), x_sample.reshape(n_s_tok, D_MODEL))
    kvp, states = (), ()
    for l in range(depth):
        outs = _in_projection(xs, mod, l, n1, w_main, w_dt, qg, kg, dtb, headmat, rope, kvp, dims)
        qa, qb = outs[0:2]
        kvp, kvs = tuple(outs[2:6]), tuple(outs[6:10])
        z, xbc, dt = outs[10:13]

        oab = _attention_ctx(a_sink, qa, qb, kvp, l, dims)
        oab = _attention_lat(a_sink, qa, qb, kvs, caches, oab, l, dims)

        oc, hf, hb = _ssd(xbc, z, dt, ssd_consts, l, dims, latent=False, st_prev=states)
        states = (hf, hb)
        oc, = _ssd(xbc, z, dt, ssd_consts, l, dims, latent=True, init=init, oc=oc)

        x1, h2, comb = _out_projection(oab, oc, xs, mod, l, n2, w_out16, wr, br, dims)
        xs = tuple(_moe(h2, comb, wg16, wu16, wd16, x1, mod, l, fg, dims, final=(l == depth - 1)))

    y_prompt = xs[0].reshape(batch, seq, D_MODEL)
    y_sample = xs[1].reshape(dec_batch, dec_seq, D_MODEL)
    kv_shape = (batch, depth, seq, A_KV, HEAD_DIM)
    st_shape = (batch, depth, SSM_HEADS, SSM_HEAD_DIM, SSM_STATE)
    return ((y_prompt, y_sample) + tuple(t.reshape(kv_shape) for t in kvp)
            + tuple(t.reshape(st_shape) for t in states))
```
